```python
import math
import jax, jax.numpy as jnp
from jax import lax
import numpy as np

D_MODEL = 1024
BATCH = 8
SEQ = 4096
DEPTH = 2

CTX_LEN = 256
GRID_W = 64
HEAD_DIM = 64
ATTN_WIDTH = D_MODEL // 2
N_HEADS = ATTN_WIDTH // HEAD_DIM
N_KV_HEADS = N_HEADS // 4
WINDOW = 128
ATTN_BLOCK = 128
ROPE_BASE = 10000.0
POOL_WINDOWS = (2, 4, 8, 16)
POOL_WIDTH = D_MODEL // 4
POOL_GROUP = POOL_WIDTH // 4
RET_HEADS = 4
RET_WIDTH = D_MODEL // 4
RET_DV = RET_WIDTH // RET_HEADS
RET_DK = RET_DV // 2
RET_CHUNK = 128
MIX_WIDTH = ATTN_WIDTH + POOL_WIDTH + RET_WIDTH
IN_SPLITS = (ATTN_WIDTH, N_KV_HEADS * HEAD_DIM, N_KV_HEADS * HEAD_DIM, POOL_WIDTH,
             RET_HEADS * RET_DK, RET_HEADS * RET_DK, RET_WIDTH, RET_WIDTH)
IN_WIDTH = ATTN_WIDTH + 2 * N_KV_HEADS * HEAD_DIM + POOL_WIDTH + 2 * RET_HEADS * RET_DK + 2 * RET_WIDTH
FFN_DIM = ((8 * D_MODEL // 3 + 127) // 128) * 128
N_EXPERTS = 8
TOP_K = 2
LN_EPS = 1e-5
NEG_INF = -1e30

kernel_name = "hybrid_flow_block"


def layer_norm(x, g, b):
    xf = x.astype(jnp.float32)
    mu = jnp.mean(xf, -1, keepdims=True)
    var = jnp.mean(jnp.square(xf - mu), -1, keepdims=True)
    y = (xf - mu) * lax.rsqrt(var + LN_EPS) * g.astype(jnp.float32) + b.astype(jnp.float32)
    return y.astype(x.dtype)


def head_norm(o):
    mu = jnp.mean(o, -1, keepdims=True)
    var = jnp.mean(jnp.square(o - mu), -1, keepdims=True)
    return (o - mu) * lax.rsqrt(var + LN_EPS)


def axial_rope(x, row, col):
    half = HEAD_DIM // 2
    quarter = half // 2
    inv = ROPE_BASE ** (-jnp.arange(quarter, dtype=jnp.float32) / quarter)

    def rot(xp, pos):
        ang = pos.astype(jnp.float32)[:, None] * inv[None, :]
        cos = jnp.cos(ang)[None, :, None, :]
        sin = jnp.sin(ang)[None, :, None, :]
        x1, x2 = xp[..., :quarter], xp[..., quarter:]
        return jnp.concatenate([x1 * cos - x2 * sin, x1 * sin + x2 * cos], -1)

    xf = x.astype(jnp.float32)
    return jnp.concatenate([rot(xf[..., :half], row), rot(xf[..., half:], col)], -1).astype(x.dtype)


def sink_softmax(sink, scores):
    B, KV, G, Q, _ = scores.shape
    sk = jnp.broadcast_to(sink.astype(jnp.float32).reshape(1, KV, G, 1, 1), (B, KV, G, Q, 1))
    p = jax.nn.softmax(jnp.concatenate([sk, scores], -1), axis=-1)
    return p[..., 1:]


def band_attention(q, k, v, kc, vc, sink):
    B, L, H, hd = q.shape
    KV = k.shape[2]
    G = H // KV
    nb = L // ATTN_BLOCK
    span = 3 * ATTN_BLOCK
    scale = hd ** -0.5
    qb = jnp.moveaxis(q.reshape(B, nb, ATTN_BLOCK, KV, G, hd), 1, 0)
    pad = ((0, 0), (ATTN_BLOCK, ATTN_BLOCK), (0, 0), (0, 0))
    kp = jnp.pad(k, pad)
    vp = jnp.pad(v, pad)

    def block(args):
        i, qi = args
        start = i * ATTN_BLOCK
        ki = lax.dynamic_slice_in_dim(kp, start, span, axis=1)
        vi = lax.dynamic_slice_in_dim(vp, start, span, axis=1)
        qpos = start + jnp.arange(ATTN_BLOCK)
        kpos = start - ATTN_BLOCK + jnp.arange(span)
        ok = (kpos[None, :] >= 0) & (kpos[None, :] < L) & (jnp.abs(qpos[:, None] - kpos[None, :]) <= WINDOW)
        s_loc = jnp.einsum('bqkgd,bjkd->bkgqj', qi, ki).astype(jnp.float32) * scale
        s_loc = jnp.where(ok, s_loc, NEG_INF)
        s_ctx = jnp.einsum('bqkgd,bjkd->bkgqj', qi, kc).astype(jnp.float32) * scale
        p = sink_softmax(sink, jnp.concatenate([s_loc, s_ctx], -1)).astype(v.dtype)
        return (jnp.einsum('bkgqj,bjkd->bqkgd', p[..., :span], vi)
                + jnp.einsum('bkgqj,bjkd->bqkgd', p[..., span:], vc))

    o = lax.map(block, (jnp.arange(nb), qb))
    return jnp.moveaxis(o, 0, 1).reshape(B, L, H * hd)


def ctx_attention(qc, kc, vc, sink):
    B, Lc, H, hd = qc.shape
    KV = kc.shape[2]
    G = H // KV
    qr = qc.reshape(B, Lc, KV, G, hd)
    s = jnp.einsum('bqkgd,bjkd->bkgqj', qr, kc).astype(jnp.float32) * hd ** -0.5
    p = sink_softmax(sink, s).astype(vc.dtype)
    return jnp.einsum('bkgqj,bjkd->bqkgd', p, vc).reshape(B, Lc, H * hd)


def pool_mix(u, pool_w, pool_scale):
    B, L, _ = u.shape
    t = jnp.arange(L)
    uf = u.astype(jnp.float32)
    cs = jnp.pad(jnp.cumsum(uf, axis=1), ((0, 0), (1, 0), (0, 0)))
    outs = []
    for g, w in enumerate(POOL_WINDOWS):
        sl = slice(g * POOL_GROUP, (g + 1) * POOL_GROUP)
        ug, csg = uf[..., sl], cs[..., sl]
        lo = jnp.clip(t - w // 2, 0, L)
        hi = jnp.clip(t + w - w // 2, 0, L)
        cnt = (hi - lo).astype(jnp.float32)[None, :, None]
        mean = (jnp.take(csg, hi, axis=1) - jnp.take(csg, lo, axis=1)) / cnt
        outs.append(jnp.einsum('blc,cd->bld', (mean - ug).astype(u.dtype), pool_w[g]))
    return jnp.concatenate(outs, -1) * pool_scale


def retention_scan(q, k, v, log_g, state0):
    B, L, H, dk = q.shape
    dv = v.shape[-1]
    C = RET_CHUNK
    nc = L // C
    qc = q.astype(jnp.float32).reshape(B, nc, C, H, dk)
    kc = k.astype(jnp.float32).reshape(B, nc, C, H, dk)
    vc = v.astype(jnp.float32).reshape(B, nc, C, H, dv)
    n = jnp.arange(C, dtype=jnp.float32)
    rel = n[:, None] - n[None, :]
    dmask = jnp.where(rel >= 0, jnp.exp(log_g[:, None, None] * jnp.maximum(rel, 0.0)), 0.0)
    intra = jnp.einsum('bcnhd,bcmhd->bchnm', qc, kc) * dmask
    o_intra = jnp.einsum('bchnm,bcmhe->bcnhe', intra, vc)
    w_state = jnp.exp(log_g[None, :] * (C - 1 - n)[:, None])
    kv = jnp.einsum('bcmhd,bcmhe->cbhde', kc * w_state[:, :, None], vc)
    chunk_decay = jnp.exp(log_g * C)[None, :, None, None]

    def step(S, kv_i):
        return chunk_decay * S + kv_i, S

    s_fin, s_prev = lax.scan(step, state0, kv)
    q_dec = qc * jnp.exp(log_g[None, :] * (n + 1)[:, None])[:, :, None]
    o_cross = jnp.einsum('bcnhd,cbhde->bcnhe', q_dec, s_prev)
    return (o_intra + o_cross).reshape(B, L, H, dv), s_fin


def retention_state(k, v, log_g):
    L = k.shape[1]
    t = jnp.arange(L, dtype=jnp.float32)
    w = jnp.exp(log_g[None, :] * (L - 1 - t)[:, None])
    return jnp.einsum('blhd,blhe->bhde', k.astype(jnp.float32) * w[None, :, :, None], v.astype(jnp.float32))


def flip(t):
    return jnp.flip(t, axis=1)


def heads(t, n):
    return t.reshape(t.shape[0], t.shape[1], n, -1)


def mixer(h, hc, w_in, sink, pool_w, pool_scale, lg_f, lg_b, w_out, need_ctx):
    B, L, _ = h.shape
    ROWS = L // GRID_W
    row = jnp.repeat(jnp.arange(ROWS), GRID_W)
    col = jnp.tile(jnp.arange(GRID_W), ROWS)
    cuts = [int(s) for s in np.cumsum(IN_SPLITS)[:-1]]
    q, k, v, u, rq, rk, rv, rg = jnp.split(h @ w_in, cuts, axis=-1)
    qc, kc, vc, uc, rqc, rkc, rvc, rgc = jnp.split(hc @ w_in, cuts, axis=-1)

    q = axial_rope(heads(q, N_HEADS), row, col)
    k = axial_rope(heads(k, N_KV_HEADS), row, col)
    v = heads(v, N_KV_HEADS)
    kc, vc = heads(kc, N_KV_HEADS), heads(vc, N_KV_HEADS)
    attn = band_attention(q, k, v, kc, vc, sink)

    pool = pool_mix(u, pool_w, pool_scale)

    lg_f = lg_f.astype(jnp.float32)
    lg_b = lg_b.astype(jnp.float32)
    rscale = RET_DK ** -0.5
    rq, rk, rv = heads(rq, RET_HEADS), heads(rk, RET_HEADS) * rscale, heads(rv, RET_HEADS)
    rqc, rkc, rvc = heads(rqc, RET_HEADS), heads(rkc, RET_HEADS) * rscale, heads(rvc, RET_HEADS)
    if need_ctx:
        zero = jnp.zeros((B, RET_HEADS, RET_DK, RET_DV), jnp.float32)
        ocf, s_f = retention_scan(rqc, rkc, rvc, lg_f, zero)
        ocb, s_b = retention_scan(flip(rqc), flip(rkc), flip(rvc), lg_b, zero)
    else:
        s_f = retention_state(rkc, rvc, lg_f)
        s_b = retention_state(flip(rkc), flip(rvc), lg_b)
    olf, _ = retention_scan(rq, rk, rv, lg_f, s_f)
    olb, _ = retention_scan(flip(rq), flip(rk), flip(rv), lg_b, s_b)
    ret = jax.nn.silu(rg) * head_norm(olf + flip(olb)).reshape(B, L, RET_WIDTH).astype(rg.dtype)

    y = jnp.concatenate([attn, pool, ret], -1) @ w_out
    if not need_ctx:
        return y, None
    Lc = hc.shape[1]
    attn_c = ctx_attention(heads(qc, N_HEADS), kc, vc, sink)
    pool_c = pool_mix(uc, pool_w, pool_scale)
    ret_c = jax.nn.silu(rgc) * head_norm(ocf + flip(ocb)).reshape(B, Lc, RET_WIDTH).astype(rgc.dtype)
    yc = jnp.concatenate([attn_c, pool_c, ret_c], -1) @ w_out
    return y, yc


def swiglu(t, wg, wu, wd):
    return (jax.nn.silu(t @ wg) * (t @ wu)) @ wd


def moe_ffn(t, router, wg, wu, wd):
    shape = t.shape
    tf = t.reshape(-1, shape[-1])
    logits = (tf @ router).astype(jnp.float32)
    top_v, top_i = lax.top_k(logits, TOP_K)
    top_w = jax.nn.softmax(top_v, axis=-1)
    gates = jnp.sum(jax.nn.one_hot(top_i, N_EXPERTS, dtype=jnp.float32) * top_w[..., None], axis=1)
    gates = gates.astype(t.dtype)
    y = jnp.zeros_like(tf)
    for e in range(N_EXPERTS):
        y = y + gates[:, e:e + 1] * swiglu(tf, wg[e], wu[e], wd[e])
    return y.reshape(shape)


def channel_mixer(t, i, ffn_w_gate, ffn_w_up, ffn_w_down, moe_router, moe_w_gate, moe_w_up, moe_w_down):
    j = i // 2
    if i % 2 == 0:
        return swiglu(t, ffn_w_gate[j], ffn_w_up[j], ffn_w_down[j])
    return moe_ffn(t, moe_router[j], moe_w_gate[j], moe_w_up[j], moe_w_down[j])


def setup_inputs(seed: int = 0) -> dict:
    key = jax.random.key(seed)
    ks = jax.random.split(key, 32)
    f32 = jnp.float32

    def nrm(k, shape, s):
        return jax.random.normal(k, shape, f32) * s

    beta = (8.0 * DEPTH) ** -0.25
    nd, nm = (DEPTH + 1) // 2, DEPTH // 2
    base_decay = jnp.log(1.0 - 2.0 ** (-5.0 - jnp.arange(RET_HEADS, dtype=f32)))
    D = D_MODEL
    return {
        "x": nrm(ks[0], (BATCH, SEQ, D), 1.0),
        "c": nrm(ks[1], (BATCH, D), 1.0),
        "ctx": nrm(ks[2], (BATCH, CTX_LEN, D), 1.0),
        "c_ctx": nrm(ks[3], (D,), 1.0),
        "w_mod": nrm(ks[4], (DEPTH, D, 6 * D), D ** -0.5),
        "b_mod": nrm(ks[5], (DEPTH, 6 * D), 0.02),
        "w_in": nrm(ks[6], (DEPTH, D, IN_WIDTH), D ** -0.5),
        "attn_sink": nrm(ks[7], (DEPTH, N_HEADS), 0.5),
        "pool_w": nrm(ks[8], (DEPTH, len(POOL_WINDOWS), POOL_GROUP, POOL_GROUP), POOL_GROUP ** -0.5),
        "pool_scale": 1.0 + nrm(ks[9], (DEPTH, POOL_WIDTH), 0.1),
        "ret_log_decay_fwd": base_decay[None, :] * jnp.exp(nrm(ks[10], (DEPTH, RET_HEADS), 0.1)),
        "ret_log_decay_bwd": base_decay[None, :] * jnp.exp(nrm(ks[11], (DEPTH, RET_HEADS), 0.1)),
        "w_out": nrm(ks[12], (DEPTH, MIX_WIDTH, D), beta * MIX_WIDTH ** -0.5),
        "ln1_g": 1.0 + nrm(ks[13], (DEPTH, D), 0.02),
        "ln1_b": nrm(ks[14], (DEPTH, D), 0.02),
        "ln2_g": 1.0 + nrm(ks[15], (DEPTH, D), 0.02),
        "ln2_b": nrm(ks[16], (DEPTH, D), 0.02),
        "ffn_w_gate": nrm(ks[17], (nd, D, FFN_DIM), D ** -0.5),
        "ffn_w_up": nrm(ks[18], (nd, D, FFN_DIM), D ** -0.5),
        "ffn_w_down": nrm(ks[19], (nd, FFN_DIM, D), beta * FFN_DIM ** -0.5),
        "moe_router": nrm(ks[20], (nm, D, N_EXPERTS), D ** -0.5),
        "moe_w_gate": nrm(ks[21], (nm, N_EXPERTS, D, FFN_DIM), D ** -0.5),
        "moe_w_up": nrm(ks[22], (nm, N_EXPERTS, D, FFN_DIM), D ** -0.5),
        "moe_w_down": nrm(ks[23], (nm, N_EXPERTS, FFN_DIM, D), beta * FFN_DIM ** -0.5),
    }


def reference(x, c, ctx, c_ctx, w_mod, b_mod, w_in, attn_sink, pool_w, pool_scale,
              ret_log_decay_fwd, ret_log_decay_bwd, w_out, ln1_g, ln1_b, ln2_g, ln2_b,
              ffn_w_gate, ffn_w_up, ffn_w_down, moe_router, moe_w_gate, moe_w_up, moe_w_down):
    alpha = (2.0 * DEPTH) ** 0.25
    xc = ctx
    for i in range(DEPTH):
        last = i == DEPTH - 1
        mod = (jax.nn.silu(c) @ w_mod[i] + b_mod[i])[:, None, :]
        modc = (jax.nn.silu(c_ctx) @ w_mod[i] + b_mod[i])[None, None, :]
        sh1, sc1, g1, sh2, sc2, g2 = jnp.split(mod, 6, axis=-1)
        sh1c, sc1c, g1c, sh2c, sc2c, g2c = jnp.split(modc, 6, axis=-1)

        y, yc = mixer(x * (1 + sc1) + sh1, xc * (1 + sc1c) + sh1c, w_in[i], attn_sink[i],
                      pool_w[i], pool_scale[i], ret_log_decay_fwd[i], ret_log_decay_bwd[i],
                      w_out[i], not last)
        x = layer_norm(alpha * x + g1 * y, ln1_g[i], ln1_b[i])
        f = channel_mixer(x * (1 + sc2) + sh2, i, ffn_w_gate, ffn_w_up, ffn_w_down,
                          moe_router, moe_w_gate, moe_w_up, moe_w_down)
        x = layer_norm(alpha * x + g2 * f, ln2_g[i], ln2_b[i])
        if not last:
            xc = layer_norm(alpha * xc + g1c * yc, ln1_g[i], ln1_b[i])
            fc = channel_mixer(xc * (1 + sc2c) + sh2c, i, ffn_w_gate, ffn_w_up, ffn_w_down,
                               moe_router, moe_w_gate, moe_w_up, moe_w_down)
            xc = layer_norm(alpha * xc + g2c * fc, ln2_g[i], ln2_b[i])
    return x
```

```python
import functools
import math

import numpy as np
import jax
import jax.numpy as jnp
from jax import lax
from jax.experimental import pallas as pl
from jax.experimental.pallas import tpu as pltpu

F32 = jnp.float32
BF16 = jnp.bfloat16

D_MODEL = 1024
GRID_W = 64
HEAD_DIM = 64
N_HEADS = 8
N_KV_HEADS = 2
ATTN_WIDTH = N_HEADS * HEAD_DIM
ATTN_BLOCK = 128
ROPE_BASE = 10000.0
POOL_WINDOWS = (2, 4, 8, 16)
POOL_WIDTH = 256
POOL_GROUP = 64
POOL_TILE = 256
POOL_HALO = 16
RET_HEADS = 4
RET_DK = 32
RET_DV = 64
RET_WIDTH = 256
RET_KW = RET_HEADS * RET_DK
RET_CHUNK = 128
FFN_DIM = 2816
FFN_CHUNK = 256
N_EXPERTS = 8
LN_EPS = 1e-5
NEG_INF = -1e30
LANES = 128
W2_WIDTH = 2048

VMEM_LIMIT = 56 * 1024 * 1024


def _cparams(sem):
    return pltpu.CompilerParams(dimension_semantics=sem, vmem_limit_bytes=VMEM_LIMIT)


def _dot(a, b):
    return jnp.dot(a, b, preferred_element_type=F32)


def _dot_nt(a, b):
    return lax.dot_general(a, b, (((1,), (1,)), ((), ())), preferred_element_type=F32)


def _dot_tn(a, b):
    return lax.dot_general(a, b, (((0,), (0,)), ((), ())), preferred_element_type=F32)


def _split_bf16(a):
    hi = a.astype(BF16)
    lo = (a - hi.astype(F32)).astype(BF16)
    return hi, lo


def _dot3(a, b):
    ah, al = _split_bf16(a)
    bh, bl = _split_bf16(b)
    return _dot(ah, bh) + _dot(ah, bl) + _dot(al, bh)


def _layer_norm(z, g, b):
    mu = jnp.mean(z, axis=-1, keepdims=True)
    d = z - mu
    var = jnp.mean(d * d, axis=-1, keepdims=True)
    return d * lax.rsqrt(var + LN_EPS) * g + b


def _mod_kernel(c_ref, w_ref, b_ref, o_ref):
    s = jax.nn.silu(c_ref[...])
    o_ref[...] = _dot3(s, w_ref[...]) + b_ref[...]


def _modulation(c_all, w_mod, b_mod):
    depth, d, n6 = w_mod.shape
    r = c_all.shape[0]
    bn = 1536
    return pl.pallas_call(
        _mod_kernel,
        grid=(depth, n6 // bn),
        in_specs=[
            pl.BlockSpec((r, d), lambda i, j: (0, 0)),
            pl.BlockSpec((None, d, bn), lambda i, j: (i, 0, j)),
            pl.BlockSpec((None, 1, bn), lambda i, j: (i, 0, j)),
        ],
        out_specs=pl.BlockSpec((None, r, bn), lambda i, j: (i, 0, j)),
        out_shape=jax.ShapeDtypeStruct((depth, r, n6), F32),
        compiler_params=_cparams(("arbitrary", "arbitrary")),
        name="modulation",
    )(c_all, w_mod, b_mod.reshape(depth, 1, n6))


_IN_OUT_WIDTHS = (ATTN_WIDTH, 256, 256, POOL_WIDTH, RET_KW, RET_KW, RET_WIDTH, RET_WIDTH)


def _rope_groups(a, cos, s_prev, s_next):
    outs = []
    for g in range(a.shape[1] // LANES):
        ag = a[:, g * LANES:(g + 1) * LANES]
        outs.append(ag * cos + pltpu.roll(ag, 16, 1) * s_prev + pltpu.roll(ag, LANES - 16, 1) * s_next)
    return outs[0] if len(outs) == 1 else jnp.concatenate(outs, axis=1)


def _inproj_kernel(*refs, rope):
    if rope:
        x_ref, mod_ref, w_ref, cos_ref, sp_ref, sn_ref = refs[:6]
        outs = refs[6:]
    else:
        x_ref, mod_ref, w_ref = refs[:3]
        outs = refs[3:]
    q_ref, kd_ref, vd_ref, u_ref, rq_ref, rk_ref, rv_ref, rg_ref = outs
    m = mod_ref[...]
    h = (x_ref[...] * (1.0 + m[1:2, :]) + m[0:1, :]).astype(BF16)

    def mm(lo, hi):
        return _dot(h, w_ref[:, lo:hi])

    q = mm(0, 512)
    kd = mm(512, 768)
    if rope:
        cos, sp, sn = cos_ref[...], sp_ref[...], sn_ref[...]
        q = _rope_groups(q, cos, sp, sn)
        kd = _rope_groups(kd, cos, sp, sn)
    q_ref[...] = q.astype(BF16)
    kd_ref[...] = kd.astype(BF16)
    vd_ref[...] = mm(768, 1024).astype(BF16)
    u_ref[...] = mm(1024, 1280).astype(BF16)
    rq_ref[...] = mm(1280, 1408).astype(BF16)
    rk_ref[...] = (mm(1408, 1536) * (RET_DK ** -0.5)).astype(BF16)
    rv_ref[...] = mm(1536, 1792).astype(BF16)
    rg_ref[...] = mm(1792, 2048).astype(BF16)


def _inproj(x, modtab, mod_row_fn, w2, rope_tabs):
    b, ls, d = x.shape
    tm = min(512, ls)
    nt = ls // tm
    rope = rope_tabs is not None
    in_specs = [
        pl.BlockSpec((None, tm, d), lambda j, bb: (bb, j, 0)),
        pl.BlockSpec((None, 6, d), lambda j, bb: (mod_row_fn(bb), 0, 0)),
        pl.BlockSpec((d, W2_WIDTH), lambda j, bb: (0, 0)),
    ]
    args = [x, modtab, w2]
    if rope:
        in_specs += [pl.BlockSpec((tm, LANES), lambda j, bb: (j, 0))] * 3
        args += list(rope_tabs)
    out_specs = [pl.BlockSpec((None, tm, w), lambda j, bb: (bb, j, 0)) for w in _IN_OUT_WIDTHS]
    out_shape = [jax.ShapeDtypeStruct((b, ls, w), BF16) for w in _IN_OUT_WIDTHS]
    return pl.pallas_call(
        functools.partial(_inproj_kernel, rope=rope),
        grid=(nt, b),
        in_specs=in_specs,
        out_specs=out_specs,
        out_shape=out_shape,
        compiler_params=_cparams(("arbitrary", "arbitrary")),
        name="inproj_rope" if rope else "inproj",
    )(*args)


def _rope_tables(l):
    p = jnp.arange(l)
    row = (p // GRID_W).astype(F32)
    col = (p % GRID_W).astype(F32)
    quarter = HEAD_DIM // 4
    inv = ROPE_BASE ** (-jnp.arange(quarter, dtype=F32) / quarter)
    lane = np.arange(LANES)
    j = lane % HEAD_DIM
    use_row = jnp.asarray(j < HEAD_DIM // 2)
    freq = inv[jnp.asarray(j % quarter)]
    pos = jnp.where(use_row[None, :], row[:, None], col[:, None])
    ang = pos * freq[None, :]
    cos, sin = jnp.cos(ang), jnp.sin(ang)
    second = jnp.asarray((lane % (2 * quarter)) >= quarter)[None, :]
    s_prev = jnp.where(second, sin, 0.0)
    s_next = jnp.where(second, 0.0, -sin)
    return cos, s_prev, s_next


def _attn_heads(sink_ref, q_ref, k_parts, v_parts, o_ref, ok):
    nq = ATTN_BLOCK
    lane = lax.broadcasted_iota(jnp.int32, (1, LANES), 1)
    rid = lax.broadcasted_iota(jnp.int32, (4 * nq, 1), 0) // nq
    for kvh in range(N_KV_HEADS):
        sl = slice(LANES * kvh, LANES * (kvh + 1))
        kall = jnp.concatenate([r[:, sl] for r in k_parts], axis=0) if len(k_parts) > 1 else k_parts[0][:, sl]
        vall = jnp.concatenate([r[:, sl] for r in v_parts], axis=0) if len(v_parts) > 1 else v_parts[0][:, sl]
        zero = jnp.zeros_like(kall)
        k_lo = jnp.where(lane < HEAD_DIM, kall, zero)
        k_hi = jnp.where(lane >= HEAD_DIM, kall, zero)
        qg = jnp.concatenate([q_ref[:, 256 * kvh:256 * kvh + LANES],
                              q_ref[:, 256 * kvh + LANES:256 * kvh + 2 * LANES]], axis=0)
        s = jnp.concatenate([_dot_nt(qg, k_lo), _dot_nt(qg, k_hi)], axis=0)
        if ok is not None:
            s = jnp.where(ok, s, NEG_INF)
        h0 = 4 * kvh
        sk = jnp.where(rid == 0, sink_ref[h0],
                       jnp.where(rid == 1, sink_ref[h0 + 2],
                                 jnp.where(rid == 2, sink_ref[h0 + 1], sink_ref[h0 + 3])))
        m = jnp.maximum(jnp.max(s, axis=1, keepdims=True), sk)
        e = jnp.exp(s - m)
        den = jnp.sum(e, axis=1, keepdims=True) + jnp.exp(sk - m)
        o = _dot(e.astype(BF16), vall) / den
        g0 = jnp.where(lane < HEAD_DIM, o[0:nq], o[2 * nq:3 * nq])
        g1 = jnp.where(lane < HEAD_DIM, o[nq:2 * nq], o[3 * nq:4 * nq])
        o_ref[:, 256 * kvh:256 * kvh + LANES] = g0.astype(BF16)
        o_ref[:, 256 * kvh + LANES:256 * kvh + 2 * LANES] = g1.astype(BF16)


def _attn_lat_kernel(sink_ref, q_ref, kl_ref, kc_ref, kr_ref, kx_ref,
                     vl_ref, vc_ref, vr_ref, vx_ref, o_ref, *, nb, lc):
    i = pl.program_id(1)
    nq = ATTN_BLOCK
    span = 3 * nq
    c_lo = jnp.where(i >= 1, 0, nq)
    c_hi = jnp.where(i <= nb - 2, span, 2 * nq)
    rows = lax.broadcasted_iota(jnp.int32, (4 * nq, span + lc), 0) & (nq - 1)
    cols = lax.broadcasted_iota(jnp.int32, (4 * nq, span + lc), 1)
    ok = (cols >= span) | ((cols >= jnp.maximum(rows, c_lo)) & (cols <= rows + 2 * nq) & (cols < c_hi))
    _attn_heads(sink_ref, q_ref, (kl_ref, kc_ref, kr_ref, kx_ref), (vl_ref, vc_ref, vr_ref, vx_ref), o_ref, ok)


def _attn_ctx_kernel(sink_ref, q_ref, kx_ref, vx_ref, o_ref):
    _attn_heads(sink_ref, q_ref, (kx_ref,), (vx_ref,), o_ref, None)


def _attention_latent(sink, q, kd, vd, kdc, vdc):
    b, l, _ = q.shape
    lc = kdc.shape[1]
    nq = ATTN_BLOCK
    nb = l // nq
    smem = pl.BlockSpec(memory_space=pltpu.SMEM)
    blk = lambda f: pl.BlockSpec((None, nq, 256), f)
    left = lambda bb, i: (bb, jnp.maximum(i - 1, 0), 0)
    cen = lambda bb, i: (bb, i, 0)
    right = lambda bb, i: (bb, jnp.minimum(i + 1, nb - 1), 0)
    ctx = pl.BlockSpec((None, lc, 256), lambda bb, i: (bb, 0, 0))
    return pl.pallas_call(
        functools.partial(_attn_lat_kernel, nb=nb, lc=lc),
        grid=(b, nb),
        in_specs=[smem, pl.BlockSpec((None, nq, ATTN_WIDTH), cen),
                  blk(left), blk(cen), blk(right), ctx,
                  blk(left), blk(cen), blk(right), ctx],
        out_specs=pl.BlockSpec((None, nq, ATTN_WIDTH), cen),
        out_shape=jax.ShapeDtypeStruct((b, l, ATTN_WIDTH), BF16),
        compiler_params=_cparams(("arbitrary", "arbitrary")),
        name="attn_latent",
    )(sink, q, kd, kd, kd, kdc, vd, vd, vd, vdc)


def _attention_ctx(sink, qc, kdc, vdc):
    b, lc, _ = qc.shape
    nq = ATTN_BLOCK
    smem = pl.BlockSpec(memory_space=pltpu.SMEM)
    ctx = pl.BlockSpec((None, lc, 256), lambda bb, i: (bb, 0, 0))
    return pl.pallas_call(
        _attn_ctx_kernel,
        grid=(b, lc // nq),
        in_specs=[smem, pl.BlockSpec((None, nq, ATTN_WIDTH), lambda bb, i: (bb, i, 0)), ctx, ctx],
        out_specs=pl.BlockSpec((None, nq, ATTN_WIDTH), lambda bb, i: (bb, i, 0)),
        out_shape=jax.ShapeDtypeStruct((b, lc, ATTN_WIDTH), BF16),
        compiler_params=_cparams(("arbitrary", "arbitrary")),
        name="attn_ctx",
    )(sink, qc, kdc, vdc)


def _pool_band_matrices():
    r = np.arange(POOL_TILE)[:, None]
    a = np.arange(POOL_TILE + 2 * POOL_HALO)[None, :] - POOL_HALO
    mats = [((a >= r - w // 2) & (a < r + w - w // 2)) for w in POOL_WINDOWS]
    return jnp.asarray(np.stack(mats).astype(np.float32), dtype=BF16)


def _pool_kernel(u_ref, a_ref, w_ref, sc_ref, o_ref, pad_ref, *, ls):
    halo = POOL_HALO
    zeros = jnp.zeros((halo, POOL_WIDTH), BF16)
    pad_ref[0:halo, :] = zeros
    pad_ref[halo + ls:2 * halo + ls, :] = zeros
    pad_ref[halo:halo + ls, :] = u_ref[...]
    lane = lax.broadcasted_iota(jnp.int32, (1, POOL_WIDTH), 1)
    grp = lane // POOL_GROUP
    wl = jnp.where(grp == 0, POOL_WINDOWS[0],
                   jnp.where(grp == 1, POOL_WINDOWS[1],
                             jnp.where(grp == 2, POOL_WINDOWS[2], POOL_WINDOWS[3])))
    half = wl // 2

    def body(j, carry):
        t0 = pl.multiple_of(j * POOL_TILE, POOL_TILE)
        slab = pad_ref[pl.ds(t0, POOL_TILE + 2 * halo), :]
        acc = _dot(a_ref[0], slab)
        for g in range(1, len(POOL_WINDOWS)):
            acc = jnp.where(grp == g, _dot(a_ref[g], slab), acc)
        p = t0 + lax.broadcasted_iota(jnp.int32, (POOL_TILE, 1), 0)
        hi = jnp.minimum(p + (wl - half), ls)
        lo = jnp.maximum(p - half, 0)
        cnt = (hi - lo).astype(F32)
        ut = u_ref[pl.ds(t0, POOL_TILE), :].astype(F32)
        d = (acc / cnt - ut).astype(BF16)
        o_ref[pl.ds(t0, POOL_TILE), :] = (_dot(d, w_ref[...]) * sc_ref[...]).astype(BF16)
        return carry

    lax.fori_loop(0, ls // POOL_TILE, body, 0)


def _pool(u, band, wblk, scale):
    b, ls, _ = u.shape
    return pl.pallas_call(
        functools.partial(_pool_kernel, ls=ls),
        grid=(b,),
        in_specs=[
            pl.BlockSpec((None, ls, POOL_WIDTH), lambda bb: (bb, 0, 0)),
            pl.BlockSpec(band.shape, lambda bb: (0, 0, 0)),
            pl.BlockSpec((POOL_WIDTH, POOL_WIDTH), lambda bb: (0, 0)),
            pl.BlockSpec((1, POOL_WIDTH), lambda bb: (0, 0)),
        ],
        out_specs=pl.BlockSpec((None, ls, POOL_WIDTH), lambda bb: (bb, 0, 0)),
        out_shape=jax.ShapeDtypeStruct((b, ls, POOL_WIDTH), BF16),
        scratch_shapes=[pltpu.VMEM((ls + 2 * POOL_HALO, POOL_WIDTH), BF16)],
        compiler_params=_cparams(("arbitrary",)),
        name="pool",
    )(u, band, wblk, scale)


def _ret_kernel(lg_ref, rq_ref, rk_ref, rv_ref, rg_ref, s0f_ref, s0b_ref, gmat_ref,
                o_ref, sf_ref, sb_ref,
                kvf_ref, kvb_ref, spf_ref, spb_ref, dm_ref, tab_ref, *, nc):
    c_len = RET_CHUNK
    kw, vw = RET_KW, RET_WIDTH

    def per_head(idx, d):
        return jnp.where(idx == 0, lg_ref[d, 0],
                         jnp.where(idx == 1, lg_ref[d, 1],
                                   jnp.where(idx == 2, lg_ref[d, 2], lg_ref[d, 3])))

    hk = lax.broadcasted_iota(jnp.int32, (1, kw), 1) // RET_DK
    hv = lax.broadcasted_iota(jnp.int32, (1, vw), 1) // RET_DV
    n_col = lax.broadcasted_iota(jnp.int32, (c_len, 1), 0).astype(F32)
    lgk_f, lgk_b = per_head(hk, 0), per_head(hk, 1)
    tab_ref[0] = jnp.exp(lgk_f * (c_len - 1.0 - n_col))
    tab_ref[1] = jnp.exp(lgk_b * n_col)
    tab_ref[2] = jnp.exp(lgk_f * (n_col + 1.0))
    tab_ref[3] = jnp.exp(lgk_b * (c_len - n_col))
    hs = lax.broadcasted_iota(jnp.int32, (1, RET_HEADS * c_len), 1) // c_len
    m_idx = (lax.broadcasted_iota(jnp.int32, (c_len, RET_HEADS * c_len), 1) & (c_len - 1)).astype(F32)
    n_idx = lax.broadcasted_iota(jnp.int32, (c_len, RET_HEADS * c_len), 0).astype(F32)
    rel = n_idx - m_idx
    dm_ref[0] = jnp.where(rel >= 0, jnp.exp(per_head(hs, 0) * jnp.maximum(rel, 0.0)), 0.0)
    dm_ref[1] = jnp.where(rel <= 0, jnp.exp(per_head(hs, 1) * jnp.maximum(-rel, 0.0)), 0.0)

    bd = (lax.broadcasted_iota(jnp.int32, (kw, vw), 0) // RET_DK) == (lax.broadcasted_iota(jnp.int32, (kw, vw), 1) // RET_DV)
    hk_col = lax.broadcasted_iota(jnp.int32, (kw, 1), 0) // RET_DK
    cd_f = jnp.exp(per_head(hk_col, 0) * float(c_len))
    cd_b = jnp.exp(per_head(hk_col, 1) * float(c_len))

    def rows(c):
        return pl.ds(pl.multiple_of(c * c_len, c_len), c_len)

    def kv_body(c, carry):
        k = rk_ref[rows(c), :].astype(F32)
        v = rv_ref[rows(c), :]
        kvf_ref[c] = jnp.where(bd, _dot_tn((k * tab_ref[0]).astype(BF16), v), 0.0)
        kvb_ref[c] = jnp.where(bd, _dot_tn((k * tab_ref[1]).astype(BF16), v), 0.0)
        return carry

    lax.fori_loop(0, nc, kv_body, 0)

    def scan_body(j, carry):
        s_f, s_b = carry
        cb = nc - 1 - j
        spf_ref[j] = s_f.astype(BF16)
        spb_ref[cb] = s_b.astype(BF16)
        return cd_f * s_f + kvf_ref[j], cd_b * s_b + kvb_ref[cb]

    s_f, s_b = lax.fori_loop(0, nc, scan_body, (s0f_ref[...], s0b_ref[...]))
    sf_ref[...] = s_f
    sb_ref[...] = s_b

    def out_body(c, carry):
        q = rq_ref[rows(c), :]
        k = rk_ref[rows(c), :]
        v = rv_ref[rows(c), :]
        zk, zv = jnp.zeros_like(k), jnp.zeros_like(v)
        kst = jnp.concatenate([jnp.where(hk == h, k, zk) for h in range(RET_HEADS)], axis=0)
        vst = jnp.concatenate([jnp.where(hv == h, v, zv) for h in range(RET_HEADS)], axis=0)
        sc = _dot_nt(q, kst)
        qf = q.astype(F32)
        o = (_dot((sc * dm_ref[0]).astype(BF16), vst) + _dot((qf * tab_ref[2]).astype(BF16), spf_ref[c])
             + _dot((sc * dm_ref[1]).astype(BF16), vst) + _dot((qf * tab_ref[3]).astype(BF16), spb_ref[c]))
        mu = _dot(o.astype(BF16), gmat_ref[...])
        d = o - mu
        var = _dot((d * d).astype(BF16), gmat_ref[...])
        hn = d * lax.rsqrt(var + LN_EPS)
        g = rg_ref[rows(c), :].astype(F32)
        o_ref[rows(c), :] = (jax.nn.silu(g) * hn).astype(BF16)
        return carry

    lax.fori_loop(0, nc, out_body, 0)


def _retention(lg, rq, rk, rv, rg, s0f, s0b, gmat):
    b, ls, _ = rq.shape
    nc = ls // RET_CHUNK
    seq = lambda w: pl.BlockSpec((None, ls, w), lambda bb: (bb, 0, 0))
    st = pl.BlockSpec((None, RET_KW, RET_WIDTH), lambda bb: (bb, 0, 0))
    return pl.pallas_call(
        functools.partial(_ret_kernel, nc=nc),
        grid=(b,),
        in_specs=[pl.BlockSpec(memory_space=pltpu.SMEM), seq(RET_KW), seq(RET_KW), seq(RET_WIDTH), seq(RET_WIDTH),
                  st, st, pl.BlockSpec((RET_WIDTH, RET_WIDTH), lambda bb: (0, 0))],
        out_specs=[seq(RET_WIDTH), st, st],
        out_shape=[jax.ShapeDtypeStruct((b, ls, RET_WIDTH), BF16),
                   jax.ShapeDtypeStruct((b, RET_KW, RET_WIDTH), F32),
                   jax.ShapeDtypeStruct((b, RET_KW, RET_WIDTH), F32)],
        scratch_shapes=[
            pltpu.VMEM((nc, RET_KW, RET_WIDTH), F32),
            pltpu.VMEM((nc, RET_KW, RET_WIDTH), F32),
            pltpu.VMEM((nc, RET_KW, RET_WIDTH), BF16),
            pltpu.VMEM((nc, RET_KW, RET_WIDTH), BF16),
            pltpu.VMEM((2, RET_CHUNK, RET_HEADS * RET_CHUNK), F32),
            pltpu.VMEM((4, RET_CHUNK, RET_KW), F32),
        ],
        compiler_params=_cparams(("arbitrary",)),
        name="retention",
    )(lg, rq, rk, rv, rg, s0f, s0b, gmat)


def _outproj_kernel(*refs, alpha, moe):
    if moe:
        (attn_ref, pool_ref, ret_ref, x_ref, mod_ref, w_ref, g_ref, b_ref, router_ref,
         x1_ref, t_ref, lg_ref) = refs
    else:
        attn_ref, pool_ref, ret_ref, x_ref, mod_ref, w_ref, g_ref, b_ref, x1_ref, t_ref = refs
    y = (_dot(attn_ref[...], w_ref[0:ATTN_WIDTH, :])
         + _dot(pool_ref[...], w_ref[ATTN_WIDTH:ATTN_WIDTH + POOL_WIDTH, :])
         + _dot(ret_ref[...], w_ref[ATTN_WIDTH + POOL_WIDTH:, :]))
    m = mod_ref[...]
    x1 = _layer_norm(alpha * x_ref[...] + m[2:3, :] * y, g_ref[...], b_ref[...])
    x1_ref[...] = x1
    t = x1 * (1.0 + m[4:5, :]) + m[3:4, :]
    t_ref[...] = t.astype(t_ref.dtype)
    if moe:
        lg_ref[...] = _dot3(t, router_ref[...])


def _outproj_ln(attn, pool, ret, x, modtab, mod_row_fn, w_out, ln_g, ln_b, alpha, router=None):
    b, ls, d = x.shape
    tm = min(512, ls)
    nt = ls // tm
    moe = router is not None
    row = lambda w: pl.BlockSpec((None, tm, w), lambda bb, j: (bb, j, 0))
    const = lambda shape: pl.BlockSpec(shape, lambda bb, j: tuple(0 for _ in shape))
    in_specs = [row(ATTN_WIDTH), row(POOL_WIDTH), row(RET_WIDTH), row(d),
                pl.BlockSpec((None, 6, d), lambda bb, j: (mod_row_fn(bb), 0, 0)),
                const((d, d)), const((1, d)), const((1, d))]
    args = [attn, pool, ret, x, modtab, w_out, ln_g, ln_b]
    out_specs = [row(d), row(d)]
    out_shape = [jax.ShapeDtypeStruct((b, ls, d), F32),
                 jax.ShapeDtypeStruct((b, ls, d), F32 if moe else BF16)]
    if moe:
        in_specs.append(const((d, LANES)))
        args.append(router)
        out_specs.append(row(LANES))
        out_shape.append(jax.ShapeDtypeStruct((b, ls, LANES), F32))
    return pl.pallas_call(
        functools.partial(_outproj_kernel, alpha=alpha, moe=moe),
        grid=(b, nt),
        in_specs=in_specs,
        out_specs=out_specs,
        out_shape=out_shape,
        compiler_params=_cparams(("arbitrary", "arbitrary")),
        name="outproj_ln_moe" if moe else "outproj_ln",
    )(*args)


def _swiglu_rows(tb, wg_ref, wu_ref, wd_ref):
    f = None
    for c in range(FFN_DIM // FFN_CHUNK):
        sl = slice(c * FFN_CHUNK, (c + 1) * FFN_CHUNK)
        g = _dot(tb, wg_ref[:, sl])
        u = _dot(tb, wu_ref[:, sl])
        a = (jax.nn.silu(g) * u).astype(BF16)
        part = _dot(a, wd_ref[sl, :])
        f = part if f is None else f + part
    return f


def _ffn_kernel(t_ref, x1_ref, mod_ref, wg_ref, wu_ref, wd_ref, g_ref, b_ref, o_ref, *, alpha):
    f = _swiglu_rows(t_ref[...], wg_ref, wu_ref, wd_ref)
    m = mod_ref[...]
    o_ref[...] = _layer_norm(alpha * x1_ref[...] + m[5:6, :] * f, g_ref[...], b_ref[...])


def _ffn_ln(t, x1, modtab, mod_row_fn, wg, wu, wd, ln_g, ln_b, alpha):
    b, ls, d = x1.shape
    tm = min(512, ls)
    nt = ls // tm
    row = lambda: pl.BlockSpec((None, tm, d), lambda bb, j: (bb, j, 0))
    const = lambda shape: pl.BlockSpec(shape, lambda bb, j: tuple(0 for _ in shape),
                                       pipeline_mode=pl.Buffered(1))
    return pl.pallas_call(
        functools.partial(_ffn_kernel, alpha=alpha),
        grid=(b, nt),
        in_specs=[row(), row(), pl.BlockSpec((None, 6, d), lambda bb, j: (mod_row_fn(bb), 0, 0)),
                  const((d, FFN_DIM)), const((d, FFN_DIM)), const((FFN_DIM, d)),
                  const((1, d)), const((1, d))],
        out_specs=row(),
        out_shape=jax.ShapeDtypeStruct((b, ls, d), F32),
        compiler_params=_cparams(("arbitrary", "arbitrary")),
        name="ffn_ln",
    )(t, x1, modtab, wg, wu, wd, ln_g, ln_b)


ROUTE_TILE = 512
MOE_TILE = 512
ROW_TILE = 256


def _route_kernel(lg_ref, tri_ref, o_ref, cnt_ref, carry_ref):
    @pl.when(pl.program_id(0) == 0)
    def _():
        carry_ref[...] = jnp.zeros_like(carry_ref)

    tm = lg_ref.shape[0]
    lane = lax.broadcasted_iota(jnp.int32, (tm, LANES), 1)
    l = jnp.where(lane < N_EXPERTS, lg_ref[...], -jnp.inf)
    m1 = jnp.max(l, axis=1, keepdims=True)
    i1 = jnp.min(jnp.where(l == m1, lane, LANES), axis=1, keepdims=True)
    l2 = jnp.where(lane == i1, -jnp.inf, l)
    m2 = jnp.max(l2, axis=1, keepdims=True)
    i2 = jnp.min(jnp.where(l2 == m2, lane, LANES), axis=1, keepdims=True)
    e = jnp.exp(m2 - m1)
    w1 = 1.0 / (1.0 + e)
    w2 = e / (1.0 + e)
    oh = jnp.where((lane == i1) | (lane == i2), 1.0, 0.0)
    carry = carry_ref[0:1, :]
    cum = _dot(tri_ref[...], oh.astype(BF16)) + carry
    r1 = jnp.sum(jnp.where(lane == i1, cum, 0.0), axis=1, keepdims=True)
    r2 = jnp.sum(jnp.where(lane == i2, cum, 0.0), axis=1, keepdims=True)
    new = carry + jnp.sum(oh, axis=0, keepdims=True)
    carry_ref[...] = jnp.broadcast_to(new, carry_ref.shape)
    cnt_ref[...] = jnp.broadcast_to(new, cnt_ref.shape)
    o_ref[...] = jnp.where(lane == 0, i1.astype(F32),
                           jnp.where(lane == 1, i2.astype(F32),
                                     jnp.where(lane == 2, r1,
                                               jnp.where(lane == 3, r2,
                                                         jnp.where(lane == 4, w1,
                                                                   jnp.where(lane == 5, w2, 0.0))))))


def _route(logits):
    n = logits.shape[0]
    tm = min(ROUTE_TILE, n)
    tri = jnp.asarray(np.tril(np.ones((tm, tm), np.float32), -1), dtype=BF16)
    return pl.pallas_call(
        _route_kernel,
        grid=(n // tm,),
        in_specs=[pl.BlockSpec((tm, LANES), lambda i: (i, 0)), pl.BlockSpec((tm, tm), lambda i: (0, 0))],
        out_specs=[pl.BlockSpec((tm, LANES), lambda i: (i, 0)), pl.BlockSpec((8, LANES), lambda i: (0, 0))],
        out_shape=[jax.ShapeDtypeStruct((n, LANES), F32), jax.ShapeDtypeStruct((8, LANES), F32)],
        scratch_shapes=[pltpu.VMEM((8, LANES), F32)],
        compiler_params=_cparams(("arbitrary",)),
        name="route",
    )(logits, tri)


def _row_copy_wait(src_ref, dst_ref, sem):
    pltpu.make_async_copy(src_ref, dst_ref, sem).wait()


def _dispatch_kernel(p1_ref, p2_ref, t_ref, xs_in_ref, xs_ref, sem):
    del xs_in_ref
    tm = t_ref.shape[0]
    base = pl.program_id(0) * tm

    def issue(r, carry):
        src = t_ref.at[pl.ds(r, 1)]
        pltpu.make_async_copy(src, xs_ref.at[pl.ds(p1_ref[base + r], 1)], sem.at[0]).start()
        pltpu.make_async_copy(src, xs_ref.at[pl.ds(p2_ref[base + r], 1)], sem.at[1]).start()
        return carry

    lax.fori_loop(0, tm, issue, 0)
    _row_copy_wait(t_ref, xs_ref.at[pl.ds(0, tm)], sem.at[0])
    _row_copy_wait(t_ref, xs_ref.at[pl.ds(0, tm)], sem.at[1])


def _dispatch(pos1, pos2, t, xs0):
    n, d = t.shape
    tm = min(ROW_TILE, n)
    return pl.pallas_call(
        _dispatch_kernel,
        grid_spec=pltpu.PrefetchScalarGridSpec(
            num_scalar_prefetch=2,
            grid=(n // tm,),
            in_specs=[pl.BlockSpec((tm, d), lambda i, p1, p2: (i, 0)),
                      pl.BlockSpec(memory_space=pl.ANY)],
            out_specs=pl.BlockSpec(memory_space=pl.ANY),
            scratch_shapes=[pltpu.SemaphoreType.DMA((2,))],
        ),
        out_shape=jax.ShapeDtypeStruct(xs0.shape, xs0.dtype),
        input_output_aliases={3: 0},
        compiler_params=_cparams(("arbitrary",)),
        name="dispatch",
    )(pos1, pos2, t, xs0)


def _moe_kernel(te_ref, nt_ref, x_ref, wg_ref, wu_ref, wd_ref, o_ref):
    used = pl.program_id(0) < nt_ref[0]

    @pl.when(used)
    def _():
        o_ref[...] = _swiglu_rows(x_ref[...].astype(BF16), wg_ref, wu_ref, wd_ref)

    @pl.when(jnp.logical_not(used))
    def _():
        o_ref[...] = jnp.zeros_like(o_ref)


def _moe_ffn(tile_expert, ntiles, xs, wg, wu, wd):
    rows, d = xs.shape
    tm = MOE_TILE
    nt_max = rows // tm
    blk = lambda i, te, nt: (jnp.minimum(i, nt[0] - 1), 0)
    wspec = lambda shape: pl.BlockSpec((None,) + shape, lambda i, te, nt: (te[i], 0, 0))
    return pl.pallas_call(
        _moe_kernel,
        grid_spec=pltpu.PrefetchScalarGridSpec(
            num_scalar_prefetch=2,
            grid=(nt_max,),
            in_specs=[pl.BlockSpec((tm, d), blk),
                      wspec((d, FFN_DIM)), wspec((d, FFN_DIM)), wspec((FFN_DIM, d))],
            out_specs=pl.BlockSpec((tm, d), lambda i, te, nt: (i, 0)),
        ),
        out_shape=jax.ShapeDtypeStruct((rows, d), F32),
        compiler_params=_cparams(("arbitrary",)),
        name="moe_ffn",
    )(tile_expert, ntiles, xs, wg, wu, wd)


def _combine_kernel(p1_ref, p2_ref, os_ref, x1_ref, rt_ref, mod_ref, g_ref, b_ref, o_ref, buf_ref, sem, *, alpha):
    tm = x1_ref.shape[0]
    base = pl.program_id(0) * tm

    def issue(r, carry):
        pltpu.make_async_copy(os_ref.at[pl.ds(p1_ref[base + r], 1)], buf_ref.at[0, pl.ds(r, 1)], sem.at[0]).start()
        pltpu.make_async_copy(os_ref.at[pl.ds(p2_ref[base + r], 1)], buf_ref.at[1, pl.ds(r, 1)], sem.at[1]).start()
        return carry

    lax.fori_loop(0, tm, issue, 0)
    _row_copy_wait(os_ref.at[pl.ds(0, tm)], buf_ref.at[0], sem.at[0])
    _row_copy_wait(os_ref.at[pl.ds(0, tm)], buf_ref.at[1], sem.at[1])
    rt = rt_ref[...]
    f = rt[:, 4:5] * buf_ref[0] + rt[:, 5:6] * buf_ref[1]
    m = mod_ref[...]
    o_ref[...] = _layer_norm(alpha * x1_ref[...] + m[5:6, :] * f, g_ref[...], b_ref[...])


def _combine_ln(pos1, pos2, os_, x1, rtab, modtab, l, ln_g, ln_b, alpha):
    n, d = x1.shape
    tm = min(ROW_TILE, n)
    per_b = l // tm
    const = lambda shape: pl.BlockSpec(shape, lambda i, p1, p2: tuple(0 for _ in shape))
    return pl.pallas_call(
        functools.partial(_combine_kernel, alpha=alpha),
        grid_spec=pltpu.PrefetchScalarGridSpec(
            num_scalar_prefetch=2,
            grid=(n // tm,),
            in_specs=[pl.BlockSpec(memory_space=pl.ANY),
                      pl.BlockSpec((tm, d), lambda i, p1, p2: (i, 0)),
                      pl.BlockSpec((tm, LANES), lambda i, p1, p2: (i, 0)),
                      pl.BlockSpec((None, 6, d), lambda i, p1, p2: (i // per_b, 0, 0)),
                      const((1, d)), const((1, d))],
            out_specs=pl.BlockSpec((tm, d), lambda i, p1, p2: (i, 0)),
            scratch_shapes=[pltpu.VMEM((2, tm, d), F32), pltpu.SemaphoreType.DMA((2,))],
        ),
        out_shape=jax.ShapeDtypeStruct((n, d), F32),
        compiler_params=_cparams(("arbitrary",)),
        name="combine_ln",
    )(pos1, pos2, os_, x1, rtab, modtab, ln_g, ln_b)


def _moe_layer(t, logits, x1, modtab, l, router_unused, wg, wu, wd, ln_g, ln_b, alpha):
    del router_unused
    n, d = t.shape
    rtab, cnt = _route(logits)
    counts = cnt[0, :N_EXPERTS].astype(jnp.int32)
    tiles_e = (counts + MOE_TILE - 1) // MOE_TILE
    tile_end = jnp.cumsum(tiles_e)
    tile_start = tile_end - tiles_e
    ntiles = tile_end[-1:]
    nt_max = (2 * n) // MOE_TILE + N_EXPERTS
    tile_expert = jnp.minimum(
        jnp.sum((jnp.arange(nt_max)[:, None] >= tile_end[None, :]).astype(jnp.int32), axis=1),
        N_EXPERTS - 1).astype(jnp.int32)
    e1 = rtab[:, 0].astype(jnp.int32)
    e2 = rtab[:, 1].astype(jnp.int32)
    row_start = tile_start * MOE_TILE
    pos1 = row_start[e1] + rtab[:, 2].astype(jnp.int32)
    pos2 = row_start[e2] + rtab[:, 3].astype(jnp.int32)
    xs0 = jnp.zeros((nt_max * MOE_TILE, d), F32)
    xs = _dispatch(pos1, pos2, t, xs0)
    os_ = _moe_ffn(tile_expert, ntiles.astype(jnp.int32), xs, wg, wu, wd)
    return _combine_ln(pos1, pos2, os_, x1, rtab, modtab, l, ln_g, ln_b, alpha)


def _prep_w_in(w):
    hd = HEAD_DIM
    q = w[:, 0:512] * (hd ** -0.5)
    k0, k1 = w[:, 512:576], w[:, 576:640]
    v0, v1 = w[:, 640:704], w[:, 704:768]
    rest = w[:, 768:]
    return jnp.concatenate([q, k0, k0, k1, k1, v0, v0, v1, v1, rest], axis=1).astype(BF16)


def _block_diag(mats):
    n = len(mats)
    rows = []
    for i, m in enumerate(mats):
        rows.append(jnp.concatenate([m if j == i else jnp.zeros_like(m) for j in range(n)], axis=1))
    return jnp.concatenate(rows, axis=0)


def kernel(x, c, ctx, c_ctx, w_mod, b_mod, w_in, attn_sink, pool_w, pool_scale, ret_log_decay_fwd,
           ret_log_decay_bwd, w_out, ln1_g, ln1_b, ln2_g, ln2_b, ffn_w_gate, ffn_w_up, ffn_w_down,
           moe_router, moe_w_gate, moe_w_up, moe_w_down):
    b, l, d = x.shape
    lc = ctx.shape[1]
    depth = w_in.shape[0]
    alpha = (2.0 * depth) ** 0.25
    assert d == D_MODEL and l % 256 == 0 and lc % 256 == 0 and l % GRID_W == 0

    n_rows = ((b + 1 + 7) // 8) * 8
    c_all = jnp.concatenate([c, c_ctx[None, :], jnp.zeros((n_rows - b - 1, d), F32)], axis=0)
    mod_all = _modulation(c_all, w_mod, b_mod).reshape(depth, n_rows, 6, d)
    lat_row = lambda bb: bb
    ctx_row = lambda bb: b

    rope_tabs = _rope_tables(l)
    band = _pool_band_matrices()
    gmat = jnp.asarray(np.kron(np.eye(RET_HEADS), np.ones((RET_DV, RET_DV))) / RET_DV, dtype=BF16)
    zero_state = jnp.zeros((b, RET_KW, RET_WIDTH), F32)

    xc = ctx
    for i in range(depth):
        last = i == depth - 1
        modtab = mod_all[i]
        w2 = _prep_w_in(w_in[i])
        wo = w_out[i].astype(BF16)
        sink = attn_sink[i].astype(F32)
        pw = _block_diag([pool_w[i, g] for g in range(len(POOL_WINDOWS))]).astype(BF16)
        ps = pool_scale[i].reshape(1, POOL_WIDTH).astype(F32)
        lg = jnp.stack([ret_log_decay_fwd[i], ret_log_decay_bwd[i]]).astype(F32)
        g1, b1 = ln1_g[i].reshape(1, d), ln1_b[i].reshape(1, d)
        g2, b2 = ln2_g[i].reshape(1, d), ln2_b[i].reshape(1, d)

        qc, kdc, vdc, uc, rqc, rkc, rvc, rgc = _inproj(xc, modtab, ctx_row, w2, None)
        ret_c, s_f, s_b = _retention(lg, rqc, rkc, rvc, rgc, zero_state, zero_state, gmat)

        q, kd, vd, u, rq, rk, rv, rg = _inproj(x, modtab, lat_row, w2, rope_tabs)
        attn = _attention_latent(sink, q, kd, vd, kdc, vdc)
        pool = _pool(u, band, pw, ps)
        ret, _, _ = _retention(lg, rq, rk, rv, rg, s_f, s_b, gmat)

        j = i // 2
        if i % 2 == 0:
            wg, wu, wd = ffn_w_gate[j].astype(BF16), ffn_w_up[j].astype(BF16), ffn_w_down[j].astype(BF16)
            x1, t = _outproj_ln(attn, pool, ret, x, modtab, lat_row, wo, g1, b1, alpha)
            x = _ffn_ln(t, x1, modtab, lat_row, wg, wu, wd, g2, b2, alpha)
        else:
            wg, wu, wd = moe_w_gate[j].astype(BF16), moe_w_up[j].astype(BF16), moe_w_down[j].astype(BF16)
            router = jnp.pad(moe_router[j], ((0, 0), (0, LANES - N_EXPERTS)))
            x1, t, logits = _outproj_ln(attn, pool, ret, x, modtab, lat_row, wo, g1, b1, alpha, router=router)
            x = _moe_layer(t.reshape(b * l, d), logits.reshape(b * l, LANES), x1.reshape(b * l, d),
                           modtab, l, None, wg, wu, wd, g2, b2, alpha).reshape(b, l, d)

        if not last:
            attn_c = _attention_ctx(sink, qc, kdc, vdc)
            pool_c = _pool(uc, band, pw, ps)
            if i % 2 == 0:
                x1c, tc = _outproj_ln(attn_c, pool_c, ret_c, xc, modtab, ctx_row, wo, g1, b1, alpha)
                xc = _ffn_ln(tc, x1c, modtab, ctx_row, wg, wu, wd, g2, b2, alpha)
            else:
                x1c, tc, logits_c = _outproj_ln(attn_c, pool_c, ret_c, xc, modtab, ctx_row, wo, g1, b1, alpha,
                                                router=router)
                ctx_mod = jnp.broadcast_to(modtab[b:b + 1], (b, 6, d))
                xc = _moe_layer(tc.reshape(b * lc, d), logits_c.reshape(b * lc, LANES), x1c.reshape(b * lc, d),
                                ctx_mod, lc, None, wg, wu, wd, g2, b2, alpha).reshape(b, lc, d)
    return x
```

```python
import functools
import math

import numpy as np
import jax
import jax.numpy as jnp
from jax import lax
from jax.experimental import pallas as pl
from jax.experimental.pallas import tpu as pltpu

F32 = jnp.float32
BF16 = jnp.bfloat16

D_MODEL = 1024
GRID_W = 64
HEAD_DIM = 64
N_HEADS = 8
N_KV_HEADS = 2
ATTN_WIDTH = N_HEADS * HEAD_DIM
ATTN_BLOCK = 128
ATTN_BLOCKS_PER_STEP = 4
ROPE_BASE = 10000.0
POOL_WINDOWS = (2, 4, 8, 16)
POOL_WIDTH = 256
POOL_GROUP = 64
POOL_TILE = 256
POOL_HALO = 16
RET_HEADS = 4
RET_DK = 32
RET_DV = 64
RET_WIDTH = 256
RET_KW = RET_HEADS * RET_DK
RET_CHUNK = 128
FFN_DIM = 2816
FFN_CHUNK = 256
N_EXPERTS = 8
LN_EPS = 1e-5
NEG_INF = -1e30
LOG2E = math.log2(math.e)
LANES = 128
W2_WIDTH = 2048

VMEM_LIMIT = 56 * 1024 * 1024


def _cparams(sem):
    return pltpu.CompilerParams(dimension_semantics=sem, vmem_limit_bytes=VMEM_LIMIT)


def _dot(a, b):
    return jnp.dot(a, b, preferred_element_type=F32)


def _dot_nt(a, b):
    return lax.dot_general(a, b, (((1,), (1,)), ((), ())), preferred_element_type=F32)


def _dot_tn(a, b):
    return lax.dot_general(a, b, (((0,), (0,)), ((), ())), preferred_element_type=F32)


def _split_bf16(a):
    hi = a.astype(BF16)
    lo = (a - hi.astype(F32)).astype(BF16)
    return hi, lo


def _dot3(a, b):
    ah, al = _split_bf16(a)
    bh, bl = _split_bf16(b)
    return _dot(ah, bh) + _dot(ah, bl) + _dot(al, bh)


SUBLANES = 8
ROW_TILES = D_MODEL // LANES


def _store_tile_rows(ref, idx, val):
    rows = val.shape[0]
    for c in range(ROW_TILES):
        ref[idx + (pl.ds(c, rows, stride=SUBLANES), slice(None))] = val[:, c * LANES:(c + 1) * LANES]


def _load_tile_rows(ref, idx, rows):
    return jnp.concatenate([ref[idx + (pl.ds(c, rows, stride=SUBLANES), slice(None))]
                            for c in range(ROW_TILES)], axis=1)


def _layer_norm(z, g, b):
    mu = jnp.mean(z, axis=-1, keepdims=True)
    d = z - mu
    var = jnp.mean(d * d, axis=-1, keepdims=True)
    return d * lax.rsqrt(var + LN_EPS) * g + b


def _mod_kernel(c_ref, w_ref, b_ref, o_ref):
    s = jax.nn.silu(c_ref[...])
    o_ref[...] = _dot3(s, w_ref[...]) + b_ref[...]


def _modulation(c_all, w_mod, b_mod):
    depth, d, n6 = w_mod.shape
    r = c_all.shape[0]
    bn = 1536
    return pl.pallas_call(
        _mod_kernel,
        grid=(depth, n6 // bn),
        in_specs=[
            pl.BlockSpec((r, d), lambda i, j: (0, 0)),
            pl.BlockSpec((None, d, bn), lambda i, j: (i, 0, j)),
            pl.BlockSpec((None, 1, bn), lambda i, j: (i, 0, j)),
        ],
        out_specs=pl.BlockSpec((None, r, bn), lambda i, j: (i, 0, j)),
        out_shape=jax.ShapeDtypeStruct((depth, r, n6), F32),
        compiler_params=_cparams(("arbitrary", "arbitrary")),
        name="modulation",
    )(c_all, w_mod, b_mod.reshape(depth, 1, n6))


_IN_OUT_WIDTHS = (ATTN_WIDTH, 256, 256, POOL_WIDTH, RET_KW, RET_KW, RET_WIDTH, RET_WIDTH)


def _rope_groups(a, cos, s_prev, s_next):
    outs = []
    for g in range(a.shape[1] // LANES):
        ag = a[:, g * LANES:(g + 1) * LANES]
        outs.append(ag * cos + pltpu.roll(ag, 16, 1) * s_prev + pltpu.roll(ag, LANES - 16, 1) * s_next)
    return outs[0] if len(outs) == 1 else jnp.concatenate(outs, axis=1)


def _inproj_kernel(*refs, rope):
    if rope:
        x_ref, mod_ref, w_ref, cos_ref, sp_ref, sn_ref = refs[:6]
        outs = refs[6:]
    else:
        x_ref, mod_ref, w_ref = refs[:3]
        outs = refs[3:]
    q_ref, kd_ref, vd_ref, u_ref, rq_ref, rk_ref, rv_ref, rg_ref = outs
    m = mod_ref[...]
    h = (x_ref[...] * (1.0 + m[1:2, :]) + m[0:1, :]).astype(BF16)

    def mm(lo, hi):
        return _dot(h, w_ref[:, lo:hi])

    q = mm(0, 512)
    kd = mm(512, 768)
    if rope:
        cos, sp, sn = cos_ref[...], sp_ref[...], sn_ref[...]
        q = _rope_groups(q, cos, sp, sn)
        kd = _rope_groups(kd, cos, sp, sn)
    q_ref[...] = q.astype(BF16)
    kd_ref[...] = kd.astype(BF16)
    vd_ref[...] = mm(768, 1024).astype(BF16)
    u_ref[...] = mm(1024, 1280).astype(BF16)
    rq_ref[...] = mm(1280, 1408).astype(BF16)
    rk_ref[...] = (mm(1408, 1536) * (RET_DK ** -0.5)).astype(BF16)
    rv_ref[...] = mm(1536, 1792).astype(BF16)
    rg_ref[...] = mm(1792, 2048).astype(BF16)


def _inproj(x, modtab, mod_row_fn, w2, rope_tabs):
    b, ls, d = x.shape
    tm = min(512, ls)
    nt = ls // tm
    rope = rope_tabs is not None
    in_specs = [
        pl.BlockSpec((None, tm, d), lambda j, bb: (bb, j, 0)),
        pl.BlockSpec((None, 6, d), lambda j, bb: (mod_row_fn(bb), 0, 0)),
        pl.BlockSpec((d, W2_WIDTH), lambda j, bb: (0, 0)),
    ]
    args = [x, modtab, w2]
    if rope:
        in_specs += [pl.BlockSpec((tm, LANES), lambda j, bb: (j, 0))] * 3
        args += list(rope_tabs)
    out_specs = [pl.BlockSpec((None, tm, w), lambda j, bb: (bb, j, 0)) for w in _IN_OUT_WIDTHS]
    out_shape = [jax.ShapeDtypeStruct((b, ls, w), BF16) for w in _IN_OUT_WIDTHS]
    return pl.pallas_call(
        functools.partial(_inproj_kernel, rope=rope),
        grid=(nt, b),
        in_specs=in_specs,
        out_specs=out_specs,
        out_shape=out_shape,
        compiler_params=_cparams(("arbitrary", "arbitrary")),
        name="inproj_rope" if rope else "inproj",
    )(*args)


def _rope_tables(l):
    p = jnp.arange(l)
    row = (p // GRID_W).astype(F32)
    col = (p % GRID_W).astype(F32)
    quarter = HEAD_DIM // 4
    inv = ROPE_BASE ** (-jnp.arange(quarter, dtype=F32) / quarter)
    lane = np.arange(LANES)
    j = lane % HEAD_DIM
    use_row = jnp.asarray(j < HEAD_DIM // 2)
    freq = inv[jnp.asarray(j % quarter)]
    pos = jnp.where(use_row[None, :], row[:, None], col[:, None])
    ang = pos * freq[None, :]
    cos, sin = jnp.cos(ang), jnp.sin(ang)
    second = jnp.asarray((lane % (2 * quarter)) >= quarter)[None, :]
    s_prev = jnp.where(second, sin, 0.0)
    s_next = jnp.where(second, 0.0, -sin)
    return cos, s_prev, s_next


def _attn_blocks(sink_ref, q_ref, o_ref, blocks):
    nq = ATTN_BLOCK
    lane = lax.broadcasted_iota(jnp.int32, (1, LANES), 1)
    rid = lax.broadcasted_iota(jnp.int32, (4, 1, 1), 0)
    chains = [(row0, kvh, keys[kvh], ok) for row0, keys, ok in blocks for kvh in range(N_KV_HEADS)]

    scores = []
    for row0, kvh, (k_lo, k_hi, _), _ in chains:
        rows = pl.ds(row0, nq)
        qg = jnp.concatenate([q_ref[rows, 256 * kvh:256 * kvh + LANES],
                              q_ref[rows, 256 * kvh + LANES:256 * kvh + 2 * LANES]], axis=0)
        scores.append(jnp.concatenate([_dot_nt(qg, k_lo), _dot_nt(qg, k_hi)], axis=0).reshape(4, nq, -1))

    probs, sink_terms = [], []
    for (row0, kvh, _, ok), s in zip(chains, scores):
        if ok is not None:
            ok_left, ok_right = ok
            s = jnp.concatenate([jnp.where(ok_left[None], s[:, :, 0:nq], NEG_INF), s[:, :, nq:2 * nq],
                                 jnp.where(ok_right[None], s[:, :, 2 * nq:3 * nq], NEG_INF), s[:, :, 3 * nq:]],
                                axis=2)
        h0 = 4 * kvh
        sk = jnp.where(rid == 0, sink_ref[h0],
                       jnp.where(rid == 1, sink_ref[h0 + 2],
                                 jnp.where(rid == 2, sink_ref[h0 + 1], sink_ref[h0 + 3]))) * LOG2E
        m = jnp.maximum(jnp.max(s, axis=2, keepdims=True), sk)
        probs.append(jnp.exp2((s - m).astype(BF16)).reshape(4 * nq, -1))
        sink_terms.append(jnp.exp2(sk - m).reshape(4 * nq, 1))

    outs = [_dot(e, v_ext) for e, (_, _, (_, _, v_ext), _) in zip(probs, chains)]

    for (row0, kvh, _, _), o, es in zip(chains, outs, sink_terms):
        rows = pl.ds(row0, nq)
        swapped = pltpu.roll(o, HEAD_DIM, 1)
        even = o[0:2 * nq] / (swapped[0:2 * nq] + es[0:2 * nq])
        odd = swapped[2 * nq:] / (o[2 * nq:] + es[2 * nq:])
        g0 = jnp.where(lane < HEAD_DIM, even[0:nq], odd[0:nq])
        g1 = jnp.where(lane < HEAD_DIM, even[nq:], odd[nq:])
        o_ref[rows, 256 * kvh:256 * kvh + LANES] = g0.astype(BF16)
        o_ref[rows, 256 * kvh + LANES:256 * kvh + 2 * LANES] = g1.astype(BF16)


def _attn_key_parts(k, v):
    lane = lax.broadcasted_iota(jnp.int32, (1, LANES), 1)
    zero = jnp.zeros_like(k)
    return (jnp.where(lane < HEAD_DIM, k, zero), jnp.where(lane >= HEAD_DIM, k, zero),
            jnp.where(lane < HEAD_DIM, v, jnp.ones_like(v)))


def _attn_lat_kernel(sink_ref, q_ref, kl_ref, kc_ref, kr_ref, kx_ref,
                     vl_ref, vc_ref, vr_ref, vx_ref, o_ref, *, nsteps, nblk):
    i = pl.program_id(1)
    nq = ATTN_BLOCK
    rows = lax.broadcasted_iota(jnp.int32, (nq, nq), 0)
    cols = lax.broadcasted_iota(jnp.int32, (nq, nq), 1)
    upper = cols >= rows
    lower = cols <= rows
    per_head = []
    for kvh in range(N_KV_HEADS):
        sl = slice(LANES * kvh, LANES * (kvh + 1))
        left, cen, right, ctx = [_attn_key_parts(kr[:, sl], vr[:, sl])
                                 for kr, vr in ((kl_ref, vl_ref), (kc_ref, vc_ref), (kr_ref, vr_ref),
                                                (kx_ref, vx_ref))]
        own = [tuple(t[j * nq:(j + 1) * nq] for t in cen) for j in range(nblk)]
        per_head.append(([left] + own + [right], ctx))
    blocks = []
    for j in range(nblk):
        keys = [tuple(jnp.concatenate([p[t] for p in seq[j:j + 3]] + [ctx[t]], axis=0) for t in range(3))
                for seq, ctx in per_head]
        ok_left = upper & (i >= 1) if j == 0 else upper
        ok_right = lower & (i <= nsteps - 2) if j == nblk - 1 else lower
        blocks.append((j * nq, keys, (ok_left, ok_right)))
    _attn_blocks(sink_ref, q_ref, o_ref, blocks)


def _attn_ctx_kernel(sink_ref, q_ref, kx_ref, vx_ref, o_ref):
    keys = [_attn_key_parts(kx_ref[:, LANES * kvh:LANES * (kvh + 1)], vx_ref[:, LANES * kvh:LANES * (kvh + 1)])
            for kvh in range(N_KV_HEADS)]
    _attn_blocks(sink_ref, q_ref, o_ref, [(0, keys, None)])


def _attention_latent(sink, q, kd, vd, kdc, vdc):
    b, l, _ = q.shape
    lc = kdc.shape[1]
    nq = ATTN_BLOCK
    nb = l // nq
    nblk = ATTN_BLOCKS_PER_STEP
    assert nb % nblk == 0
    nsteps = nb // nblk
    smem = pl.BlockSpec(memory_space=pltpu.SMEM)
    edge = lambda f: pl.BlockSpec((None, nq, 256), f)
    left = lambda bb, i: (bb, jnp.maximum(nblk * i - 1, 0), 0)
    cen = lambda bb, i: (bb, i, 0)
    right = lambda bb, i: (bb, jnp.minimum(nblk * i + nblk, nb - 1), 0)
    mid = pl.BlockSpec((None, nblk * nq, 256), cen)
    ctx = pl.BlockSpec((None, lc, 256), lambda bb, i: (bb, 0, 0))
    return pl.pallas_call(
        functools.partial(_attn_lat_kernel, nsteps=nsteps, nblk=nblk),
        grid=(b, nsteps),
        in_specs=[smem, pl.BlockSpec((None, nblk * nq, ATTN_WIDTH), cen),
                  edge(left), mid, edge(right), ctx,
                  edge(left), mid, edge(right), ctx],
        out_specs=pl.BlockSpec((None, nblk * nq, ATTN_WIDTH), cen),
        out_shape=jax.ShapeDtypeStruct((b, l, ATTN_WIDTH), BF16),
        compiler_params=_cparams(("arbitrary", "arbitrary")),
        name="attn_latent",
    )(sink, q, kd, kd, kd, kdc, vd, vd, vd, vdc)


def _attention_ctx(sink, qc, kdc, vdc):
    b, lc, _ = qc.shape
    nq = ATTN_BLOCK
    smem = pl.BlockSpec(memory_space=pltpu.SMEM)
    ctx = pl.BlockSpec((None, lc, 256), lambda bb, i: (bb, 0, 0))
    return pl.pallas_call(
        _attn_ctx_kernel,
        grid=(b, lc // nq),
        in_specs=[smem, pl.BlockSpec((None, nq, ATTN_WIDTH), lambda bb, i: (bb, i, 0)), ctx, ctx],
        out_specs=pl.BlockSpec((None, nq, ATTN_WIDTH), lambda bb, i: (bb, i, 0)),
        out_shape=jax.ShapeDtypeStruct((b, lc, ATTN_WIDTH), BF16),
        compiler_params=_cparams(("arbitrary", "arbitrary")),
        name="attn_ctx",
    )(sink, qc, kdc, vdc)


def _pool_band_matrices():
    r = np.arange(POOL_TILE)[:, None]
    a = np.arange(POOL_TILE + 2 * POOL_HALO)[None, :] - POOL_HALO
    mats = [((a >= r - w // 2) & (a < r + w - w // 2)) for w in POOL_WINDOWS]
    return jnp.asarray(np.stack(mats).astype(np.float32), dtype=BF16)


def _pool_kernel(u_ref, a_ref, w_ref, sc_ref, o_ref, pad_ref, *, ls):
    halo = POOL_HALO
    zeros = jnp.zeros((halo, POOL_WIDTH), BF16)
    pad_ref[0:halo, :] = zeros
    pad_ref[halo + ls:2 * halo + ls, :] = zeros
    pad_ref[halo:halo + ls, :] = u_ref[...]
    lane = lax.broadcasted_iota(jnp.int32, (1, POOL_WIDTH), 1)
    grp = lane // POOL_GROUP
    wl = jnp.where(grp == 0, POOL_WINDOWS[0],
                   jnp.where(grp == 1, POOL_WINDOWS[1],
                             jnp.where(grp == 2, POOL_WINDOWS[2], POOL_WINDOWS[3])))
    half = wl // 2

    def body(j, carry):
        t0 = pl.multiple_of(j * POOL_TILE, POOL_TILE)
        slab = pad_ref[pl.ds(t0, POOL_TILE + 2 * halo), :]
        acc = _dot(a_ref[0], slab)
        for g in range(1, len(POOL_WINDOWS)):
            acc = jnp.where(grp == g, _dot(a_ref[g], slab), acc)
        p = t0 + lax.broadcasted_iota(jnp.int32, (POOL_TILE, 1), 0)
        hi = jnp.minimum(p + (wl - half), ls)
        lo = jnp.maximum(p - half, 0)
        cnt = (hi - lo).astype(F32)
        ut = u_ref[pl.ds(t0, POOL_TILE), :].astype(F32)
        d = (acc / cnt - ut).astype(BF16)
        o_ref[pl.ds(t0, POOL_TILE), :] = (_dot(d, w_ref[...]) * sc_ref[...]).astype(BF16)
        return carry

    lax.fori_loop(0, ls // POOL_TILE, body, 0, unroll=min(2, ls // POOL_TILE))


def _pool(u, band, wblk, scale):
    b, ls, _ = u.shape
    return pl.pallas_call(
        functools.partial(_pool_kernel, ls=ls),
        grid=(b,),
        in_specs=[
            pl.BlockSpec((None, ls, POOL_WIDTH), lambda bb: (bb, 0, 0)),
            pl.BlockSpec(band.shape, lambda bb: (0, 0, 0)),
            pl.BlockSpec((POOL_WIDTH, POOL_WIDTH), lambda bb: (0, 0)),
            pl.BlockSpec((1, POOL_WIDTH), lambda bb: (0, 0)),
        ],
        out_specs=pl.BlockSpec((None, ls, POOL_WIDTH), lambda bb: (bb, 0, 0)),
        out_shape=jax.ShapeDtypeStruct((b, ls, POOL_WIDTH), BF16),
        scratch_shapes=[pltpu.VMEM((ls + 2 * POOL_HALO, POOL_WIDTH), BF16)],
        compiler_params=_cparams(("arbitrary",)),
        name="pool",
    )(u, band, wblk, scale)


def _ret_kernel(lg_ref, rq_ref, rk_ref, rv_ref, rg_ref, s0f_ref, s0b_ref, gmat_ref,
                o_ref, sf_ref, sb_ref,
                kvf_ref, kvb_ref, sp_ref, dm_ref, tab_ref, *, nc):
    c_len = RET_CHUNK
    kw, vw = RET_KW, RET_WIDTH

    def per_head(idx, d):
        return jnp.where(idx == 0, lg_ref[d, 0],
                         jnp.where(idx == 1, lg_ref[d, 1],
                                   jnp.where(idx == 2, lg_ref[d, 2], lg_ref[d, 3])))

    hk = lax.broadcasted_iota(jnp.int32, (1, kw), 1) // RET_DK
    hv = lax.broadcasted_iota(jnp.int32, (1, vw), 1) // RET_DV
    n_col = lax.broadcasted_iota(jnp.int32, (c_len, 1), 0).astype(F32)
    lgk_f, lgk_b = per_head(hk, 0), per_head(hk, 1)
    tab_ref[0] = jnp.exp(lgk_f * (c_len - 1.0 - n_col))
    tab_ref[1] = jnp.exp(lgk_b * n_col)
    tab_ref[2] = jnp.exp(lgk_f * (n_col + 1.0))
    tab_ref[3] = jnp.exp(lgk_b * (c_len - n_col))
    hs = lax.broadcasted_iota(jnp.int32, (1, RET_HEADS * c_len), 1) // c_len
    m_idx = (lax.broadcasted_iota(jnp.int32, (c_len, RET_HEADS * c_len), 1) & (c_len - 1)).astype(F32)
    n_idx = lax.broadcasted_iota(jnp.int32, (c_len, RET_HEADS * c_len), 0).astype(F32)
    rel = n_idx - m_idx
    dm_ref[0] = jnp.where(rel >= 0, jnp.exp(per_head(hs, 0) * jnp.maximum(rel, 0.0)), 0.0)
    dm_ref[1] = jnp.where(rel <= 0, jnp.exp(per_head(hs, 1) * jnp.maximum(-rel, 0.0)), 0.0)

    bd = (lax.broadcasted_iota(jnp.int32, (kw, vw), 0) // RET_DK) == (lax.broadcasted_iota(jnp.int32, (kw, vw), 1) // RET_DV)
    hk_col = lax.broadcasted_iota(jnp.int32, (kw, 1), 0) // RET_DK
    cd_f = jnp.exp(per_head(hk_col, 0) * float(c_len))
    cd_b = jnp.exp(per_head(hk_col, 1) * float(c_len))

    def rows(c):
        return pl.ds(pl.multiple_of(c * c_len, c_len), c_len)

    grp = min(4, nc)
    assert nc % grp == 0

    def kv_body(gi, carry):
        cs = [gi * grp + j for j in range(grp)]
        ks = [rk_ref[rows(c), :].astype(F32) for c in cs]
        vs = [rv_ref[rows(c), :] for c in cs]
        kf = [(k * tab_ref[0]).astype(BF16) for k in ks]
        kb = [(k * tab_ref[1]).astype(BF16) for k in ks]
        pf = [_dot_tn(a, v) for a, v in zip(kf, vs)]
        pb = [_dot_tn(a, v) for a, v in zip(kb, vs)]
        for j, c in enumerate(cs):
            kvf_ref[c] = jnp.where(bd, pf[j], 0.0)
            kvb_ref[c] = jnp.where(bd, pb[j], 0.0)
        return carry

    lax.fori_loop(0, nc // grp, kv_body, 0)

    def scan_body(j, carry):
        s_f, s_b = carry
        cb = nc - 1 - j
        sp_ref[j, 0:kw, :] = s_f.astype(BF16)
        sp_ref[cb, kw:2 * kw, :] = s_b.astype(BF16)
        return cd_f * s_f + kvf_ref[j], cd_b * s_b + kvb_ref[cb]

    s_f, s_b = lax.fori_loop(0, nc, scan_body, (s0f_ref[...], s0b_ref[...]))
    sf_ref[...] = s_f
    sb_ref[...] = s_b

    def out_body(gi, carry):
        cs = [gi * grp + j for j in range(grp)]
        qs = [rq_ref[rows(c), :] for c in cs]
        ks = [rk_ref[rows(c), :] for c in cs]
        vs = [rv_ref[rows(c), :] for c in cs]
        zk, zv = jnp.zeros_like(ks[0]), jnp.zeros_like(vs[0])
        ksts = [jnp.concatenate([jnp.where(hk == h, k, zk) for h in range(RET_HEADS)], axis=0) for k in ks]
        vsts = [jnp.concatenate([jnp.where(hv == h, v, zv) for h in range(RET_HEADS)], axis=0) for v in vs]
        scs = [_dot_nt(q, kst) for q, kst in zip(qs, ksts)]
        p2s = [jnp.concatenate([(sc * dm_ref[0]).astype(BF16), (sc * dm_ref[1]).astype(BF16)], axis=0)
               for sc in scs]
        q2s = [jnp.concatenate([(q.astype(F32) * tab_ref[2]).astype(BF16),
                                (q.astype(F32) * tab_ref[3]).astype(BF16)], axis=1) for q in qs]
        o2s = [_dot(p2, vst) for p2, vst in zip(p2s, vsts)]
        ocs = [_dot(q2, sp_ref[c]) for q2, c in zip(q2s, cs)]
        o = jnp.concatenate([o2[0:c_len] + o2[c_len:] + oc for o2, oc in zip(o2s, ocs)], axis=0)
        mu = _dot(o.astype(BF16), gmat_ref[...])
        d = o - mu
        var = _dot((d * d).astype(BF16), gmat_ref[...])
        hn = d * lax.rsqrt(var + LN_EPS)
        grows = pl.ds(pl.multiple_of(gi * (grp * c_len), grp * c_len), grp * c_len)
        g = rg_ref[grows, :].astype(F32)
        o_ref[grows, :] = (jax.nn.silu(g) * hn).astype(BF16)
        return carry

    lax.fori_loop(0, nc // grp, out_body, 0)


def _retention(lg, rq, rk, rv, rg, s0f, s0b, gmat):
    b, ls, _ = rq.shape
    nc = ls // RET_CHUNK
    seq = lambda w: pl.BlockSpec((None, ls, w), lambda bb: (bb, 0, 0))
    st = pl.BlockSpec((None, RET_KW, RET_WIDTH), lambda bb: (bb, 0, 0))
    return pl.pallas_call(
        functools.partial(_ret_kernel, nc=nc),
        grid=(b,),
        in_specs=[pl.BlockSpec(memory_space=pltpu.SMEM), seq(RET_KW), seq(RET_KW), seq(RET_WIDTH), seq(RET_WIDTH),
                  st, st, pl.BlockSpec((RET_WIDTH, RET_WIDTH), lambda bb: (0, 0))],
        out_specs=[seq(RET_WIDTH), st, st],
        out_shape=[jax.ShapeDtypeStruct((b, ls, RET_WIDTH), BF16),
                   jax.ShapeDtypeStruct((b, RET_KW, RET_WIDTH), F32),
                   jax.ShapeDtypeStruct((b, RET_KW, RET_WIDTH), F32)],
        scratch_shapes=[
            pltpu.VMEM((nc, RET_KW, RET_WIDTH), F32),
            pltpu.VMEM((nc, RET_KW, RET_WIDTH), F32),
            pltpu.VMEM((nc, 2 * RET_KW, RET_WIDTH), BF16),
            pltpu.VMEM((2, RET_CHUNK, RET_HEADS * RET_CHUNK), F32),
            pltpu.VMEM((4, RET_CHUNK, RET_KW), F32),
        ],
        compiler_params=_cparams(("arbitrary",)),
        name="retention",
    )(lg, rq, rk, rv, rg, s0f, s0b, gmat)


def _outproj_kernel(*refs, alpha, moe):
    if moe:
        (attn_ref, pool_ref, ret_ref, x_ref, mod_ref, w_ref, g_ref, b_ref, router_ref,
         x1_ref, t_ref, lg_ref) = refs
    else:
        attn_ref, pool_ref, ret_ref, x_ref, mod_ref, w_ref, g_ref, b_ref, x1_ref, t_ref = refs
    y = (_dot(attn_ref[...], w_ref[0:ATTN_WIDTH, :])
         + _dot(pool_ref[...], w_ref[ATTN_WIDTH:ATTN_WIDTH + POOL_WIDTH, :])
         + _dot(ret_ref[...], w_ref[ATTN_WIDTH + POOL_WIDTH:, :]))
    m = mod_ref[...]
    x1 = _layer_norm(alpha * x_ref[...] + m[2:3, :] * y, g_ref[...], b_ref[...])
    x1_ref[...] = x1
    t = x1 * (1.0 + m[4:5, :]) + m[3:4, :]
    if moe:
        _store_tile_rows(t_ref, (), t)
        lg_ref[...] = _dot(t.astype(BF16), router_ref[...])
    else:
        t_ref[...] = t.astype(BF16)


def _outproj_ln(attn, pool, ret, x, modtab, mod_row_fn, w_out, ln_g, ln_b, alpha, router=None):
    b, ls, d = x.shape
    tm = min(512, ls)
    nt = ls // tm
    moe = router is not None
    row = lambda w: pl.BlockSpec((None, tm, w), lambda bb, j: (bb, j, 0))
    const = lambda shape: pl.BlockSpec(shape, lambda bb, j: tuple(0 for _ in shape))
    in_specs = [row(ATTN_WIDTH), row(POOL_WIDTH), row(RET_WIDTH), row(d),
                pl.BlockSpec((None, 6, d), lambda bb, j: (mod_row_fn(bb), 0, 0)),
                const((d, d)), const((1, d)), const((1, d))]
    args = [attn, pool, ret, x, modtab, w_out, ln_g, ln_b]
    out_specs = [row(d), row(d)]
    out_shape = [jax.ShapeDtypeStruct((b, ls, d), F32), jax.ShapeDtypeStruct((b, ls, d), BF16)]
    if moe:
        out_specs[1] = pl.BlockSpec((None, tm * SUBLANES, LANES), lambda bb, j: (bb, j, 0))
        out_shape[1] = jax.ShapeDtypeStruct((b, ls * SUBLANES, LANES), F32)
        in_specs.append(const((d, LANES)))
        args.append(router)
        out_specs.append(row(LANES))
        out_shape.append(jax.ShapeDtypeStruct((b, ls, LANES), F32))
    return pl.pallas_call(
        functools.partial(_outproj_kernel, alpha=alpha, moe=moe),
        grid=(b, nt),
        in_specs=in_specs,
        out_specs=out_specs,
        out_shape=out_shape,
        compiler_params=_cparams(("arbitrary", "arbitrary")),
        name="outproj_ln_moe" if moe else "outproj_ln",
    )(*args)


N_FFN_CHUNKS = FFN_DIM // FFN_CHUNK


def _swiglu_rows(tb, wg_ref, wu_ref, wd_ref, between=None):
    f = None
    for c in range(N_FFN_CHUNKS):
        sl = slice(c * FFN_CHUNK, (c + 1) * FFN_CHUNK)
        g = _dot(tb, wg_ref[:, sl])
        u = _dot(tb, wu_ref[:, sl])
        a = (jax.nn.silu(g) * u).astype(BF16)
        part = _dot(a, wd_ref[sl, :])
        f = part if f is None else f + part
        if between is not None:
            between(c)
    return f


def _ffn_kernel(t_ref, x1_ref, mod_ref, wg_ref, wu_ref, wd_ref, g_ref, b_ref, o_ref, *, alpha):
    f = _swiglu_rows(t_ref[...], wg_ref, wu_ref, wd_ref)
    m = mod_ref[...]
    o_ref[...] = _layer_norm(alpha * x1_ref[...] + m[5:6, :] * f, g_ref[...], b_ref[...])


def _ffn_ln(t, x1, modtab, mod_row_fn, wg, wu, wd, ln_g, ln_b, alpha):
    b, ls, d = x1.shape
    tm = min(512, ls)
    nt = ls // tm
    row = lambda: pl.BlockSpec((None, tm, d), lambda bb, j: (bb, j, 0))
    const = lambda shape: pl.BlockSpec(shape, lambda bb, j: tuple(0 for _ in shape),
                                       pipeline_mode=pl.Buffered(1))
    return pl.pallas_call(
        functools.partial(_ffn_kernel, alpha=alpha),
        grid=(b, nt),
        in_specs=[row(), row(), pl.BlockSpec((None, 6, d), lambda bb, j: (mod_row_fn(bb), 0, 0)),
                  const((d, FFN_DIM)), const((d, FFN_DIM)), const((FFN_DIM, d)),
                  const((1, d)), const((1, d))],
        out_specs=row(),
        out_shape=jax.ShapeDtypeStruct((b, ls, d), F32),
        compiler_params=_cparams(("arbitrary", "arbitrary")),
        name="ffn_ln",
    )(t, x1, modtab, wg, wu, wd, ln_g, ln_b)


ROUTE_TILE = 512
MOE_TILE = 512
ROW_TILE = 256


def _route_kernel(lg_ref, tri_ref, o_ref, ot_ref, cnt_ref, carry_ref):
    @pl.when(pl.program_id(0) == 0)
    def _():
        carry_ref[...] = jnp.zeros_like(carry_ref)

    tm = lg_ref.shape[0]
    lane = lax.broadcasted_iota(jnp.int32, (tm, LANES), 1)
    l = jnp.where(lane < N_EXPERTS, lg_ref[...], -jnp.inf)
    m1 = jnp.max(l, axis=1, keepdims=True)
    i1 = jnp.min(jnp.where(l == m1, lane, LANES), axis=1, keepdims=True)
    l2 = jnp.where(lane == i1, -jnp.inf, l)
    m2 = jnp.max(l2, axis=1, keepdims=True)
    i2 = jnp.min(jnp.where(l2 == m2, lane, LANES), axis=1, keepdims=True)
    e = jnp.exp(m2 - m1)
    w1 = 1.0 / (1.0 + e)
    w2 = e / (1.0 + e)
    oh = jnp.where((lane == i1) | (lane == i2), 1.0, 0.0)
    carry = carry_ref[0:1, :]
    cum = _dot(tri_ref[...], oh.astype(BF16)) + carry
    r1 = jnp.sum(jnp.where(lane == i1, cum, 0.0), axis=1, keepdims=True)
    r2 = jnp.sum(jnp.where(lane == i2, cum, 0.0), axis=1, keepdims=True)
    new = carry + jnp.sum(oh, axis=0, keepdims=True)
    carry_ref[...] = jnp.broadcast_to(new, carry_ref.shape)
    cnt_ref[...] = jnp.broadcast_to(new, cnt_ref.shape)
    tab = jnp.where(lane == 0, i1.astype(F32),
                    jnp.where(lane == 1, i2.astype(F32),
                              jnp.where(lane == 2, r1,
                                        jnp.where(lane == 3, r2,
                                                  jnp.where(lane == 4, w1,
                                                            jnp.where(lane == 5, w2, 0.0))))))
    o_ref[...] = tab
    ot_ref[...] = tab.T[0:8, :]


def _route(logits):
    n = logits.shape[0]
    tm = min(ROUTE_TILE, n)
    tri = jnp.asarray(np.tril(np.ones((tm, tm), np.float32), -1), dtype=BF16)
    return pl.pallas_call(
        _route_kernel,
        grid=(n // tm,),
        in_specs=[pl.BlockSpec((tm, LANES), lambda i: (i, 0)), pl.BlockSpec((tm, tm), lambda i: (0, 0))],
        out_specs=[pl.BlockSpec((tm, LANES), lambda i: (i, 0)), pl.BlockSpec((8, tm), lambda i: (0, i)),
                   pl.BlockSpec((8, LANES), lambda i: (0, 0))],
        out_shape=[jax.ShapeDtypeStruct((n, LANES), F32), jax.ShapeDtypeStruct((8, n), F32),
                   jax.ShapeDtypeStruct((8, LANES), F32)],
        scratch_shapes=[pltpu.VMEM((8, LANES), F32)],
        compiler_params=_cparams(("arbitrary",)),
        name="route",
    )(logits, tri)


def _invert_kernel(p1_ref, p2_ref, lo_ref, hi_ref, dst_ref, *, n):
    for e in range(N_EXPERTS + 1):
        def fill(s, carry):
            dst_ref[s] = 2 * n + (s & (MOE_TILE - 1))
            return carry

        lax.fori_loop(lo_ref[e], hi_ref[e], fill, 0)

    def body(t, carry):
        dst_ref[p1_ref[t]] = t
        dst_ref[p2_ref[t]] = n + t
        return carry

    lax.fori_loop(0, n, body, 0, unroll=16)


def _invert(pos1, pos2, fill_lo, fill_hi, nslots):
    n = pos1.shape[0]
    smem = pl.BlockSpec(memory_space=pltpu.SMEM)
    return pl.pallas_call(
        functools.partial(_invert_kernel, n=n),
        in_specs=[smem, smem, smem, smem],
        out_specs=smem,
        out_shape=jax.ShapeDtypeStruct((nslots,), jnp.int32),
        name="invert",
    )(pos1, pos2, fill_lo, fill_hi)


def _slot_source_row(v, n):
    if n & (n - 1) == 0:
        return v & (n - 1)
    return jnp.where(v >= 2 * n, v - 2 * n, jnp.where(v >= n, v - n, v))


def _moe_kernel(te_ref, nt_ref, inv_ref, t_ref, wg_ref, wu_ref, wd_ref, y_ref, xbuf, obuf, gsem, ssem, *, n):
    del te_ref
    i = pl.program_id(0)
    nt = nt_ref[0]
    tm = MOE_TILE
    slot = i & 1
    other = 1 - slot

    def row_tile(r):
        return pl.ds(pl.multiple_of(r * SUBLANES, SUBLANES), SUBLANES)

    def gather_row(tile, r, buf):
        src = _slot_source_row(inv_ref[tile * tm + r], n)
        pltpu.make_async_copy(t_ref.at[row_tile(src)], xbuf.at[buf, row_tile(r)], gsem.at[buf]).start()

    def scatter_row(tile, r, buf, real):
        dst = jnp.where(real, inv_ref[tile * tm + r], 2 * n + r)
        pltpu.make_async_copy(obuf.at[buf, row_tile(r)], y_ref.at[row_tile(dst)], ssem.at[buf]).start()

    def wait_gather(buf):
        pltpu.make_async_copy(t_ref.at[pl.ds(0, tm * SUBLANES)], xbuf.at[buf], gsem.at[buf]).wait()

    def wait_scatter(buf):
        pltpu.make_async_copy(obuf.at[buf], y_ref.at[pl.ds(0, tm * SUBLANES)], ssem.at[buf]).wait()

    @pl.when(i == 0)
    def _():
        obuf[1] = jnp.zeros(obuf.shape[1:], obuf.dtype)

        def first(r, carry):
            gather_row(0, r, 0)
            return carry

        lax.fori_loop(0, tm, first, 0, unroll=8)

    @pl.when(i < nt)
    def _():
        wait_gather(slot)
        nxt = jnp.minimum(i + 1, nt - 1)
        prv = jnp.maximum(i - 1, 0)
        real = i >= 1
        per = -(-tm // N_FFN_CHUNKS)

        def between(c):
            for r in range(c * per, min((c + 1) * per, tm)):
                gather_row(nxt, r, other)
                scatter_row(prv, r, other, real)

        x = _load_tile_rows(xbuf, (slot,), tm).astype(BF16)
        _store_tile_rows(obuf, (slot,), _swiglu_rows(x, wg_ref, wu_ref, wd_ref, between))
        wait_scatter(other)

    @pl.when(i == nt)
    def _():
        last = nt - 1
        buf = last & 1

        def tail(r, carry):
            scatter_row(last, r, buf, True)
            return carry

        lax.fori_loop(0, tm, tail, 0, unroll=8)
        wait_scatter(buf)
        wait_gather(nt & 1)


def _moe_ffn(tile_expert, ntiles, inv, t, wg, wu, wd):
    n = t.shape[0] // SUBLANES
    d = D_MODEL
    tm = MOE_TILE
    assert n >= tm
    nt_max = inv.shape[0] // tm
    wspec = lambda shape: pl.BlockSpec((None,) + shape, lambda i, te, nt, iv: (te[i], 0, 0))
    return pl.pallas_call(
        functools.partial(_moe_kernel, n=n),
        grid_spec=pltpu.PrefetchScalarGridSpec(
            num_scalar_prefetch=3,
            grid=(nt_max,),
            in_specs=[pl.BlockSpec(memory_space=pl.ANY),
                      wspec((d, FFN_DIM)), wspec((d, FFN_DIM)), wspec((FFN_DIM, d))],
            out_specs=pl.BlockSpec(memory_space=pl.ANY),
            scratch_shapes=[pltpu.VMEM((2, tm * SUBLANES, LANES), F32), pltpu.VMEM((2, tm * SUBLANES, LANES), F32),
                            pltpu.SemaphoreType.DMA((2,)), pltpu.SemaphoreType.DMA((2,))],
        ),
        out_shape=jax.ShapeDtypeStruct(((2 * n + tm) * SUBLANES, LANES), F32),
        compiler_params=_cparams(("arbitrary",)),
        name="moe_ffn",
    )(tile_expert, ntiles, inv, t, wg, wu, wd)


def _combine_kernel(y1_ref, y2_ref, x1_ref, rt_ref, mod_ref, g_ref, b_ref, o_ref, *, alpha):
    rt = rt_ref[...]
    tm = x1_ref.shape[0]
    f = rt[:, 4:5] * _load_tile_rows(y1_ref, (), tm) + rt[:, 5:6] * _load_tile_rows(y2_ref, (), tm)
    m = mod_ref[...]
    o_ref[...] = _layer_norm(alpha * x1_ref[...] + m[5:6, :] * f, g_ref[...], b_ref[...])


def _combine_ln(y, x1, rtab, modtab, l, ln_g, ln_b, alpha):
    n, d = x1.shape
    tm = min(ROW_TILE, n)
    per_b = l // tm
    nblk = n // tm
    const = lambda shape: pl.BlockSpec(shape, lambda i: tuple(0 for _ in shape))
    return pl.pallas_call(
        functools.partial(_combine_kernel, alpha=alpha),
        grid=(nblk,),
        in_specs=[pl.BlockSpec((tm * SUBLANES, LANES), lambda i: (i, 0)),
                  pl.BlockSpec((tm * SUBLANES, LANES), lambda i: (nblk + i, 0)),
                  pl.BlockSpec((tm, d), lambda i: (i, 0)),
                  pl.BlockSpec((tm, LANES), lambda i: (i, 0)),
                  pl.BlockSpec((None, 6, d), lambda i: (i // per_b, 0, 0)),
                  const((1, d)), const((1, d))],
        out_specs=pl.BlockSpec((tm, d), lambda i: (i, 0)),
        out_shape=jax.ShapeDtypeStruct((n, d), F32),
        compiler_params=_cparams(("arbitrary",)),
        name="combine_ln",
    )(y, y, x1, rtab, modtab, ln_g, ln_b)


def _moe_layer(t, logits, x1, modtab, l, wg, wu, wd, ln_g, ln_b, alpha):
    n = x1.shape[0]
    rtab, rtab_t, cnt = _route(logits)
    counts = cnt[0, :N_EXPERTS].astype(jnp.int32)
    tiles_e = (counts + MOE_TILE - 1) // MOE_TILE
    tile_end = jnp.cumsum(tiles_e)
    tile_start = tile_end - tiles_e
    ntiles = tile_end[-1:]
    nt_max = (2 * n) // MOE_TILE + N_EXPERTS + 1
    tile_expert = jnp.minimum(
        jnp.sum((jnp.arange(nt_max)[:, None] >= tile_end[None, :]).astype(jnp.int32), axis=1),
        N_EXPERTS - 1).astype(jnp.int32)
    e1 = rtab_t[0].astype(jnp.int32)
    e2 = rtab_t[1].astype(jnp.int32)
    row_start = tile_start * MOE_TILE
    pos1 = row_start[e1] + rtab_t[2].astype(jnp.int32)
    pos2 = row_start[e2] + rtab_t[3].astype(jnp.int32)
    fill_lo = jnp.concatenate([row_start + counts, tile_end[-1:] * MOE_TILE]).astype(jnp.int32)
    fill_hi = jnp.concatenate([tile_end * MOE_TILE, jnp.full((1,), nt_max * MOE_TILE)]).astype(jnp.int32)
    inv = _invert(pos1, pos2, fill_lo, fill_hi, nt_max * MOE_TILE)
    y = _moe_ffn(tile_expert, ntiles.astype(jnp.int32), inv, t, wg, wu, wd)
    return _combine_ln(y, x1, rtab, modtab, l, ln_g, ln_b, alpha)


def _prep_w_in(w):
    hd = HEAD_DIM
    q = w[:, 0:512] * (hd ** -0.5 * LOG2E)
    k0, k1 = w[:, 512:576], w[:, 576:640]
    v0, v1 = w[:, 640:704], w[:, 704:768]
    rest = w[:, 768:]
    return jnp.concatenate([q, k0, k0, k1, k1, v0, v0, v1, v1, rest], axis=1).astype(BF16)


def _block_diag(mats):
    n = len(mats)
    rows = []
    for i, m in enumerate(mats):
        rows.append(jnp.concatenate([m if j == i else jnp.zeros_like(m) for j in range(n)], axis=1))
    return jnp.concatenate(rows, axis=0)


def kernel(x, c, ctx, c_ctx, w_mod, b_mod, w_in, attn_sink, pool_w, pool_scale, ret_log_decay_fwd,
           ret_log_decay_bwd, w_out, ln1_g, ln1_b, ln2_g, ln2_b, ffn_w_gate, ffn_w_up, ffn_w_down,
           moe_router, moe_w_gate, moe_w_up, moe_w_down):
    b, l, d = x.shape
    lc = ctx.shape[1]
    depth = w_in.shape[0]
    alpha = (2.0 * depth) ** 0.25
    assert d == D_MODEL and l % 256 == 0 and lc % 256 == 0 and l % GRID_W == 0

    n_rows = ((b + 1 + 7) // 8) * 8
    c_all = jnp.concatenate([c, c_ctx[None, :], jnp.zeros((n_rows - b - 1, d), F32)], axis=0)
    mod_all = _modulation(c_all, w_mod, b_mod).reshape(depth, n_rows, 6, d)
    lat_row = lambda bb: bb
    ctx_row = lambda bb: b

    rope_tabs = _rope_tables(l)
    band = _pool_band_matrices()
    gmat = jnp.asarray(np.kron(np.eye(RET_HEADS), np.ones((RET_DV, RET_DV))) / RET_DV, dtype=BF16)
    zero_state = jnp.zeros((b, RET_KW, RET_WIDTH), F32)

    xc = ctx
    for i in range(depth):
        last = i == depth - 1
        modtab = mod_all[i]
        w2 = _prep_w_in(w_in[i])
        wo = w_out[i].astype(BF16)
        sink = attn_sink[i].astype(F32)
        pw = _block_diag([pool_w[i, g] for g in range(len(POOL_WINDOWS))]).astype(BF16)
        ps = pool_scale[i].reshape(1, POOL_WIDTH).astype(F32)
        lg = jnp.stack([ret_log_decay_fwd[i], ret_log_decay_bwd[i]]).astype(F32)
        g1, b1 = ln1_g[i].reshape(1, d), ln1_b[i].reshape(1, d)
        g2, b2 = ln2_g[i].reshape(1, d), ln2_b[i].reshape(1, d)

        qc, kdc, vdc, uc, rqc, rkc, rvc, rgc = _inproj(xc, modtab, ctx_row, w2, None)
        ret_c, s_f, s_b = _retention(lg, rqc, rkc, rvc, rgc, zero_state, zero_state, gmat)

        q, kd, vd, u, rq, rk, rv, rg = _inproj(x, modtab, lat_row, w2, rope_tabs)
        attn = _attention_latent(sink, q, kd, vd, kdc, vdc)
        pool = _pool(u, band, pw, ps)
        ret, _, _ = _retention(lg, rq, rk, rv, rg, s_f, s_b, gmat)

        j = i // 2
        if i % 2 == 0:
            wg, wu, wd = ffn_w_gate[j].astype(BF16), ffn_w_up[j].astype(BF16), ffn_w_down[j].astype(BF16)
            x1, t = _outproj_ln(attn, pool, ret, x, modtab, lat_row, wo, g1, b1, alpha)
            x = _ffn_ln(t, x1, modtab, lat_row, wg, wu, wd, g2, b2, alpha)
        else:
            wg, wu, wd = moe_w_gate[j].astype(BF16), moe_w_up[j].astype(BF16), moe_w_down[j].astype(BF16)
            router = jnp.pad(moe_router[j], ((0, 0), (0, LANES - N_EXPERTS))).astype(BF16)
            x1, t, logits = _outproj_ln(attn, pool, ret, x, modtab, lat_row, wo, g1, b1, alpha, router=router)
            x = _moe_layer(t.reshape(b * l * SUBLANES, LANES), logits.reshape(b * l, LANES), x1.reshape(b * l, d),
                           modtab, l, wg, wu, wd, g2, b2, alpha).reshape(b, l, d)

        if not last:
            attn_c = _attention_ctx(sink, qc, kdc, vdc)
            pool_c = _pool(uc, band, pw, ps)
            if i % 2 == 0:
                x1c, tc = _outproj_ln(attn_c, pool_c, ret_c, xc, modtab, ctx_row, wo, g1, b1, alpha)
                xc = _ffn_ln(tc, x1c, modtab, ctx_row, wg, wu, wd, g2, b2, alpha)
            else:
                x1c, tc, logits_c = _outproj_ln(attn_c, pool_c, ret_c, xc, modtab, ctx_row, wo, g1, b1, alpha,
                                                router=router)
                ctx_mod = jnp.broadcast_to(modtab[b:b + 1], (b, 6, d))
                xc = _moe_layer(tc.reshape(b * lc * SUBLANES, LANES), logits_c.reshape(b * lc, LANES),
                                x1c.reshape(b * lc, d),
                                ctx_mod, lc, wg, wu, wd, g2, b2, alpha).reshape(b, lc, d)
    return x
```

```python
import functools
import math

import numpy as np
import jax
import jax.numpy as jnp
from jax import lax
from jax.experimental import pallas as pl
from jax.experimental.pallas import tpu as pltpu

F32 = jnp.float32
BF16 = jnp.bfloat16

D_MODEL = 1024
GRID_W = 64
HEAD_DIM = 64
N_HEADS = 8
N_KV_HEADS = 2
ATTN_WIDTH = N_HEADS * HEAD_DIM
ATTN_BLOCK = 128
ATTN_BLOCKS_PER_STEP = 4
ROPE_BASE = 10000.0
POOL_WINDOWS = (2, 4, 8, 16)
POOL_WIDTH = 256
POOL_GROUP = 64
POOL_TILE = 256
POOL_HALO = 16
RET_HEADS = 4
RET_DK = 32
RET_DV = 64
RET_WIDTH = 256
RET_KW = RET_HEADS * RET_DK
RET_CHUNK = 128
FFN_DIM = 2816
FFN_CHUNK = 256
N_EXPERTS = 8
LN_EPS = 1e-5
NEG_INF = -1e30
LOG2E = math.log2(math.e)
LANES = 128
W2_WIDTH = 2048

VMEM_LIMIT = 56 * 1024 * 1024


def _cparams(sem):
    return pltpu.CompilerParams(dimension_semantics=sem, vmem_limit_bytes=VMEM_LIMIT)


def _dot(a, b):
    return jnp.dot(a, b, preferred_element_type=F32)


def _dot_nt(a, b):
    return lax.dot_general(a, b, (((1,), (1,)), ((), ())), preferred_element_type=F32)


def _dot_tn(a, b):
    return lax.dot_general(a, b, (((0,), (0,)), ((), ())), preferred_element_type=F32)


def _split_bf16(a):
    hi = a.astype(BF16)
    lo = (a - hi.astype(F32)).astype(BF16)
    return hi, lo


def _dot3(a, b):
    ah, al = _split_bf16(a)
    bh, bl = _split_bf16(b)
    return _dot(ah, bh) + _dot(ah, bl) + _dot(al, bh)


SUBLANES = 8
ROW_TILES = D_MODEL // LANES


def _store_tile_rows(ref, idx, val):
    rows = val.shape[0]
    for c in range(ROW_TILES):
        ref[idx + (pl.ds(c, rows, stride=SUBLANES), slice(None))] = val[:, c * LANES:(c + 1) * LANES]


def _load_tile_rows(ref, idx, rows):
    return jnp.concatenate([ref[idx + (pl.ds(c, rows, stride=SUBLANES), slice(None))]
                            for c in range(ROW_TILES)], axis=1)


def _layer_norm(z, g, b):
    mu = jnp.mean(z, axis=-1, keepdims=True)
    d = z - mu
    var = jnp.mean(d * d, axis=-1, keepdims=True)
    return d * lax.rsqrt(var + LN_EPS) * g + b


def _mod_kernel(c_ref, w_ref, b_ref, o_ref):
    s = jax.nn.silu(c_ref[...])
    o_ref[...] = _dot3(s, w_ref[...]) + b_ref[...]


def _modulation(c_all, w_mod, b_mod):
    depth, d, n6 = w_mod.shape
    r = c_all.shape[0]
    bn = 1536
    return pl.pallas_call(
        _mod_kernel,
        grid=(depth, n6 // bn),
        in_specs=[
            pl.BlockSpec((r, d), lambda i, j: (0, 0)),
            pl.BlockSpec((None, d, bn), lambda i, j: (i, 0, j)),
            pl.BlockSpec((None, 1, bn), lambda i, j: (i, 0, j)),
        ],
        out_specs=pl.BlockSpec((None, r, bn), lambda i, j: (i, 0, j)),
        out_shape=jax.ShapeDtypeStruct((depth, r, n6), F32),
        compiler_params=_cparams(("arbitrary", "arbitrary")),
        name="modulation",
    )(c_all, w_mod, b_mod.reshape(depth, 1, n6))


_IN_OUT_WIDTHS = (ATTN_WIDTH, 256, 256, POOL_WIDTH, RET_KW, RET_KW, RET_WIDTH, RET_WIDTH)


def _rope_groups(a, cos, s_prev, s_next):
    outs = []
    for g in range(a.shape[1] // LANES):
        ag = a[:, g * LANES:(g + 1) * LANES]
        outs.append(ag * cos + pltpu.roll(ag, 16, 1) * s_prev + pltpu.roll(ag, LANES - 16, 1) * s_next)
    return outs[0] if len(outs) == 1 else jnp.concatenate(outs, axis=1)


def _inproj_kernel(*refs, rope):
    if rope:
        x_ref, mod_ref, w_ref, cos_ref, sp_ref, sn_ref = refs[:6]
        outs = refs[6:]
    else:
        x_ref, mod_ref, w_ref = refs[:3]
        outs = refs[3:]
    q_ref, kd_ref, vd_ref, u_ref, rq_ref, rk_ref, rv_ref, rg_ref = outs
    m = mod_ref[...]
    h = (x_ref[...] * (1.0 + m[1:2, :]) + m[0:1, :]).astype(BF16)

    def mm(lo, hi):
        return _dot(h, w_ref[:, lo:hi])

    q = mm(0, 512)
    kd = mm(512, 768)
    if rope:
        cos, sp, sn = cos_ref[...], sp_ref[...], sn_ref[...]
        q = _rope_groups(q, cos, sp, sn)
        kd = _rope_groups(kd, cos, sp, sn)
    q_ref[...] = q.astype(BF16)
    kd_ref[...] = kd.astype(BF16)
    vd_ref[...] = mm(768, 1024).astype(BF16)
    u_ref[...] = mm(1024, 1280).astype(BF16)
    rq_ref[...] = mm(1280, 1408).astype(BF16)
    rk_ref[...] = (mm(1408, 1536) * (RET_DK ** -0.5)).astype(BF16)
    rv_ref[...] = mm(1536, 1792).astype(BF16)
    rg_ref[...] = mm(1792, 2048).astype(BF16)


def _inproj(x, modtab, mod_row_fn, w2, rope_tabs):
    b, ls, d = x.shape
    tm = min(1024, ls)
    nt = ls // tm
    rope = rope_tabs is not None
    in_specs = [
        pl.BlockSpec((None, tm, d), lambda j, bb: (bb, j, 0)),
        pl.BlockSpec((None, 6, d), lambda j, bb: (mod_row_fn(bb), 0, 0)),
        pl.BlockSpec((d, W2_WIDTH), lambda j, bb: (0, 0)),
    ]
    args = [x, modtab, w2]
    if rope:
        in_specs += [pl.BlockSpec((tm, LANES), lambda j, bb: (j, 0))] * 3
        args += list(rope_tabs)
    out_specs = [pl.BlockSpec((None, tm, w), lambda j, bb: (bb, j, 0)) for w in _IN_OUT_WIDTHS]
    out_shape = [jax.ShapeDtypeStruct((b, ls, w), BF16) for w in _IN_OUT_WIDTHS]
    return pl.pallas_call(
        functools.partial(_inproj_kernel, rope=rope),
        grid=(nt, b),
        in_specs=in_specs,
        out_specs=out_specs,
        out_shape=out_shape,
        compiler_params=_cparams(("arbitrary", "arbitrary")),
        name="inproj_rope" if rope else "inproj",
    )(*args)


def _rope_tables(l):
    p = jnp.arange(l)
    row = (p // GRID_W).astype(F32)
    col = (p % GRID_W).astype(F32)
    quarter = HEAD_DIM // 4
    inv = ROPE_BASE ** (-jnp.arange(quarter, dtype=F32) / quarter)
    lane = np.arange(LANES)
    j = lane % HEAD_DIM
    use_row = jnp.asarray(j < HEAD_DIM // 2)
    freq = inv[jnp.asarray(j % quarter)]
    pos = jnp.where(use_row[None, :], row[:, None], col[:, None])
    ang = pos * freq[None, :]
    cos, sin = jnp.cos(ang), jnp.sin(ang)
    second = jnp.asarray((lane % (2 * quarter)) >= quarter)[None, :]
    s_prev = jnp.where(second, sin, 0.0)
    s_next = jnp.where(second, 0.0, -sin)
    return cos, s_prev, s_next


def _attn_blocks(sink_ref, q_ref, o_ref, blocks):
    nq = ATTN_BLOCK
    lane = lax.broadcasted_iota(jnp.int32, (1, LANES), 1)
    rid = lax.broadcasted_iota(jnp.int32, (4, 1, 1), 0)
    chains = [(row0, kvh, keys[kvh], ok) for row0, keys, ok in blocks for kvh in range(N_KV_HEADS)]

    scores = []
    for row0, kvh, (k_lo, k_hi, _), _ in chains:
        rows = pl.ds(row0, nq)
        qg = jnp.concatenate([q_ref[rows, 256 * kvh:256 * kvh + LANES],
                              q_ref[rows, 256 * kvh + LANES:256 * kvh + 2 * LANES]], axis=0)
        scores.append(jnp.concatenate([_dot_nt(qg, k_lo), _dot_nt(qg, k_hi)], axis=0).reshape(4, nq, -1))

    probs, sink_terms = [], []
    for (row0, kvh, _, ok), s in zip(chains, scores):
        if ok is not None:
            ok_left, ok_right = ok
            s = jnp.concatenate([jnp.where(ok_left[None], s[:, :, 0:nq], NEG_INF), s[:, :, nq:2 * nq],
                                 jnp.where(ok_right[None], s[:, :, 2 * nq:3 * nq], NEG_INF), s[:, :, 3 * nq:]],
                                axis=2)
        h0 = 4 * kvh
        sk = jnp.where(rid == 0, sink_ref[h0],
                       jnp.where(rid == 1, sink_ref[h0 + 2],
                                 jnp.where(rid == 2, sink_ref[h0 + 1], sink_ref[h0 + 3]))) * LOG2E
        m = jnp.maximum(jnp.max(s, axis=2, keepdims=True), sk)
        probs.append(jnp.exp2((s - m).astype(BF16)).reshape(4 * nq, -1))
        sink_terms.append(jnp.exp2(sk - m).reshape(4 * nq, 1))

    outs = [_dot(e, v_ext) for e, (_, _, (_, _, v_ext), _) in zip(probs, chains)]

    for (row0, kvh, _, _), o, es in zip(chains, outs, sink_terms):
        rows = pl.ds(row0, nq)
        swapped = pltpu.roll(o, HEAD_DIM, 1)
        even = o[0:2 * nq] / (swapped[0:2 * nq] + es[0:2 * nq])
        odd = swapped[2 * nq:] / (o[2 * nq:] + es[2 * nq:])
        g0 = jnp.where(lane < HEAD_DIM, even[0:nq], odd[0:nq])
        g1 = jnp.where(lane < HEAD_DIM, even[nq:], odd[nq:])
        o_ref[rows, 256 * kvh:256 * kvh + LANES] = g0.astype(BF16)
        o_ref[rows, 256 * kvh + LANES:256 * kvh + 2 * LANES] = g1.astype(BF16)


def _attn_key_parts(k, v):
    lane = lax.broadcasted_iota(jnp.int32, (1, LANES), 1)
    zero = jnp.zeros_like(k)
    return (jnp.where(lane < HEAD_DIM, k, zero), jnp.where(lane >= HEAD_DIM, k, zero),
            jnp.where(lane < HEAD_DIM, v, jnp.ones_like(v)))


def _attn_lat_kernel(sink_ref, q_ref, kl_ref, kc_ref, kr_ref, kx_ref,
                     vl_ref, vc_ref, vr_ref, vx_ref, o_ref, *, nsteps, nblk):
    i = pl.program_id(1)
    nq = ATTN_BLOCK
    rows = lax.broadcasted_iota(jnp.int32, (nq, nq), 0)
    cols = lax.broadcasted_iota(jnp.int32, (nq, nq), 1)
    upper = cols >= rows
    lower = cols <= rows
    per_head = []
    for kvh in range(N_KV_HEADS):
        sl = slice(LANES * kvh, LANES * (kvh + 1))
        left, cen, right, ctx = [_attn_key_parts(kr[:, sl], vr[:, sl])
                                 for kr, vr in ((kl_ref, vl_ref), (kc_ref, vc_ref), (kr_ref, vr_ref),
                                                (kx_ref, vx_ref))]
        own = [tuple(t[j * nq:(j + 1) * nq] for t in cen) for j in range(nblk)]
        per_head.append(([left] + own + [right], ctx))
    blocks = []
    for j in range(nblk):
        keys = [tuple(jnp.concatenate([p[t] for p in seq[j:j + 3]] + [ctx[t]], axis=0) for t in range(3))
                for seq, ctx in per_head]
        ok_left = upper & (i >= 1) if j == 0 else upper
        ok_right = lower & (i <= nsteps - 2) if j == nblk - 1 else lower
        blocks.append((j * nq, keys, (ok_left, ok_right)))
    _attn_blocks(sink_ref, q_ref, o_ref, blocks)


def _attn_ctx_kernel(sink_ref, q_ref, kx_ref, vx_ref, o_ref):
    keys = [_attn_key_parts(kx_ref[:, LANES * kvh:LANES * (kvh + 1)], vx_ref[:, LANES * kvh:LANES * (kvh + 1)])
            for kvh in range(N_KV_HEADS)]
    _attn_blocks(sink_ref, q_ref, o_ref, [(0, keys, None)])


def _attention_latent(sink, q, kd, vd, kdc, vdc):
    b, l, _ = q.shape
    lc = kdc.shape[1]
    nq = ATTN_BLOCK
    nb = l // nq
    nblk = ATTN_BLOCKS_PER_STEP
    assert nb % nblk == 0
    nsteps = nb // nblk
    smem = pl.BlockSpec(memory_space=pltpu.SMEM)
    edge = lambda f: pl.BlockSpec((None, nq, 256), f)
    left = lambda bb, i: (bb, jnp.maximum(nblk * i - 1, 0), 0)
    cen = lambda bb, i: (bb, i, 0)
    right = lambda bb, i: (bb, jnp.minimum(nblk * i + nblk, nb - 1), 0)
    mid = pl.BlockSpec((None, nblk * nq, 256), cen)
    ctx = pl.BlockSpec((None, lc, 256), lambda bb, i: (bb, 0, 0))
    return pl.pallas_call(
        functools.partial(_attn_lat_kernel, nsteps=nsteps, nblk=nblk),
        grid=(b, nsteps),
        in_specs=[smem, pl.BlockSpec((None, nblk * nq, ATTN_WIDTH), cen),
                  edge(left), mid, edge(right), ctx,
                  edge(left), mid, edge(right), ctx],
        out_specs=pl.BlockSpec((None, nblk * nq, ATTN_WIDTH), cen),
        out_shape=jax.ShapeDtypeStruct((b, l, ATTN_WIDTH), BF16),
        compiler_params=_cparams(("arbitrary", "arbitrary")),
        name="attn_latent",
    )(sink, q, kd, kd, kd, kdc, vd, vd, vd, vdc)


def _attention_ctx(sink, qc, kdc, vdc):
    b, lc, _ = qc.shape
    nq = ATTN_BLOCK
    smem = pl.BlockSpec(memory_space=pltpu.SMEM)
    ctx = pl.BlockSpec((None, lc, 256), lambda bb, i: (bb, 0, 0))
    return pl.pallas_call(
        _attn_ctx_kernel,
        grid=(b, lc // nq),
        in_specs=[smem, pl.BlockSpec((None, nq, ATTN_WIDTH), lambda bb, i: (bb, i, 0)), ctx, ctx],
        out_specs=pl.BlockSpec((None, nq, ATTN_WIDTH), lambda bb, i: (bb, i, 0)),
        out_shape=jax.ShapeDtypeStruct((b, lc, ATTN_WIDTH), BF16),
        compiler_params=_cparams(("arbitrary", "arbitrary")),
        name="attn_ctx",
    )(sink, qc, kdc, vdc)


def _pool_band_matrices():
    r = np.arange(POOL_TILE)[:, None]
    a = np.arange(POOL_TILE + 2 * POOL_HALO)[None, :] - POOL_HALO
    mats = [((a >= r - w // 2) & (a < r + w - w // 2)) for w in POOL_WINDOWS]
    return jnp.asarray(np.concatenate(mats, axis=0).astype(np.float32), dtype=BF16)


def _pool_kernel(u_ref, a_ref, w_ref, sc_ref, o_ref, pad_ref, *, ls):
    halo = POOL_HALO
    zeros = jnp.zeros((halo, POOL_WIDTH), BF16)
    pad_ref[0:halo, :] = zeros
    pad_ref[halo + ls:2 * halo + ls, :] = zeros
    pad_ref[halo:halo + ls, :] = u_ref[...]
    lane = lax.broadcasted_iota(jnp.int32, (1, POOL_WIDTH), 1)
    grp = lane // POOL_GROUP
    wl = jnp.where(grp == 0, POOL_WINDOWS[0],
                   jnp.where(grp == 1, POOL_WINDOWS[1],
                             jnp.where(grp == 2, POOL_WINDOWS[2], POOL_WINDOWS[3])))
    half = wl // 2

    ntile = ls // POOL_TILE
    grp_tiles = min(4, ntile)
    assert ntile % grp_tiles == 0
    nw = len(POOL_WINDOWS)

    def body(gi, carry):
        t0s = [pl.multiple_of((gi * grp_tiles + j) * POOL_TILE, POOL_TILE) for j in range(grp_tiles)]
        sums = [_dot(a_ref[...], pad_ref[pl.ds(t0, POOL_TILE + 2 * halo), :]) for t0 in t0s]
        ds = []
        for t0, sm in zip(t0s, sums):
            acc = sm[0:POOL_TILE]
            for g in range(1, nw):
                acc = jnp.where(grp == g, sm[g * POOL_TILE:(g + 1) * POOL_TILE], acc)
            p = t0 + lax.broadcasted_iota(jnp.int32, (POOL_TILE, 1), 0)
            hi = jnp.minimum(p + (wl - half), ls)
            lo = jnp.maximum(p - half, 0)
            cnt = (hi - lo).astype(F32)
            ut = u_ref[pl.ds(t0, POOL_TILE), :].astype(F32)
            ds.append((acc / cnt - ut).astype(BF16))
        rows = pl.ds(t0s[0], grp_tiles * POOL_TILE)
        o_ref[rows, :] = (_dot(jnp.concatenate(ds, axis=0), w_ref[...]) * sc_ref[...]).astype(BF16)
        return carry

    lax.fori_loop(0, ntile // grp_tiles, body, 0)


def _pool(u, band, wblk, scale):
    b, ls, _ = u.shape
    return pl.pallas_call(
        functools.partial(_pool_kernel, ls=ls),
        grid=(b,),
        in_specs=[
            pl.BlockSpec((None, ls, POOL_WIDTH), lambda bb: (bb, 0, 0)),
            pl.BlockSpec(band.shape, lambda bb: (0, 0)),
            pl.BlockSpec((POOL_WIDTH, POOL_WIDTH), lambda bb: (0, 0)),
            pl.BlockSpec((1, POOL_WIDTH), lambda bb: (0, 0)),
        ],
        out_specs=pl.BlockSpec((None, ls, POOL_WIDTH), lambda bb: (bb, 0, 0)),
        out_shape=jax.ShapeDtypeStruct((b, ls, POOL_WIDTH), BF16),
        scratch_shapes=[pltpu.VMEM((ls + 2 * POOL_HALO, POOL_WIDTH), BF16)],
        compiler_params=_cparams(("arbitrary",)),
        name="pool",
    )(u, band, wblk, scale)


def _ret_kernel(lg_ref, rq_ref, rk_ref, rv_ref, rg_ref, s0f_ref, s0b_ref, gmat_ref,
                o_ref, sf_ref, sb_ref,
                kvf_ref, kvb_ref, sp_ref, dm_ref, tab_ref, *, nc):
    c_len = RET_CHUNK
    kw, vw = RET_KW, RET_WIDTH

    def per_head(idx, d):
        return jnp.where(idx == 0, lg_ref[d, 0],
                         jnp.where(idx == 1, lg_ref[d, 1],
                                   jnp.where(idx == 2, lg_ref[d, 2], lg_ref[d, 3])))

    hk = lax.broadcasted_iota(jnp.int32, (1, kw), 1) // RET_DK
    hv = lax.broadcasted_iota(jnp.int32, (1, vw), 1) // RET_DV
    n_col = lax.broadcasted_iota(jnp.int32, (c_len, 1), 0).astype(F32)
    lgk_f, lgk_b = per_head(hk, 0), per_head(hk, 1)
    tab_ref[0] = jnp.exp(lgk_f * (c_len - 1.0 - n_col))
    tab_ref[1] = jnp.exp(lgk_b * n_col)
    tab_ref[2] = jnp.exp(lgk_f * (n_col + 1.0))
    tab_ref[3] = jnp.exp(lgk_b * (c_len - n_col))
    hs = lax.broadcasted_iota(jnp.int32, (1, RET_HEADS * c_len), 1) // c_len
    m_idx = (lax.broadcasted_iota(jnp.int32, (c_len, RET_HEADS * c_len), 1) & (c_len - 1)).astype(F32)
    n_idx = lax.broadcasted_iota(jnp.int32, (c_len, RET_HEADS * c_len), 0).astype(F32)
    rel = n_idx - m_idx
    dm_ref[0] = jnp.where(rel >= 0, jnp.exp(per_head(hs, 0) * jnp.maximum(rel, 0.0)), 0.0)
    dm_ref[1] = jnp.where(rel <= 0, jnp.exp(per_head(hs, 1) * jnp.maximum(-rel, 0.0)), 0.0)

    bd = (lax.broadcasted_iota(jnp.int32, (kw, vw), 0) // RET_DK) == (lax.broadcasted_iota(jnp.int32, (kw, vw), 1) // RET_DV)
    hk_col = lax.broadcasted_iota(jnp.int32, (kw, 1), 0) // RET_DK
    cd_f = jnp.exp(per_head(hk_col, 0) * float(c_len))
    cd_b = jnp.exp(per_head(hk_col, 1) * float(c_len))

    def rows(c):
        return pl.ds(pl.multiple_of(c * c_len, c_len), c_len)

    grp = min(4, nc)
    assert nc % grp == 0

    def kv_body(gi, carry):
        cs = [gi * grp + j for j in range(grp)]
        ks = [rk_ref[rows(c), :].astype(F32) for c in cs]
        vs = [rv_ref[rows(c), :] for c in cs]
        kf = [(k * tab_ref[0]).astype(BF16) for k in ks]
        kb = [(k * tab_ref[1]).astype(BF16) for k in ks]
        pf = [_dot_tn(a, v) for a, v in zip(kf, vs)]
        pb = [_dot_tn(a, v) for a, v in zip(kb, vs)]
        for j, c in enumerate(cs):
            kvf_ref[c] = jnp.where(bd, pf[j], 0.0)
            kvb_ref[c] = jnp.where(bd, pb[j], 0.0)
        return carry

    lax.fori_loop(0, nc // grp, kv_body, 0)

    def scan_body(j, carry):
        s_f, s_b = carry
        cb = nc - 1 - j
        sp_ref[j, 0:kw, :] = s_f.astype(BF16)
        sp_ref[cb, kw:2 * kw, :] = s_b.astype(BF16)
        return cd_f * s_f + kvf_ref[j], cd_b * s_b + kvb_ref[cb]

    s_f, s_b = lax.fori_loop(0, nc, scan_body, (s0f_ref[...], s0b_ref[...]))
    sf_ref[...] = s_f
    sb_ref[...] = s_b

    def out_body(gi, carry):
        cs = [gi * grp + j for j in range(grp)]
        qs = [rq_ref[rows(c), :] for c in cs]
        ks = [rk_ref[rows(c), :] for c in cs]
        vs = [rv_ref[rows(c), :] for c in cs]
        zk, zv = jnp.zeros_like(ks[0]), jnp.zeros_like(vs[0])
        ksts = [jnp.concatenate([jnp.where(hk == h, k, zk) for h in range(RET_HEADS)], axis=0) for k in ks]
        vsts = [jnp.concatenate([jnp.where(hv == h, v, zv) for h in range(RET_HEADS)], axis=0) for v in vs]
        scs = [_dot_nt(q, kst) for q, kst in zip(qs, ksts)]
        p2s = [jnp.concatenate([(sc * dm_ref[0]).astype(BF16), (sc * dm_ref[1]).astype(BF16)], axis=0)
               for sc in scs]
        q2s = [jnp.concatenate([(q.astype(F32) * tab_ref[2]).astype(BF16),
                                (q.astype(F32) * tab_ref[3]).astype(BF16)], axis=1) for q in qs]
        o2s = [_dot(p2, vst) for p2, vst in zip(p2s, vsts)]
        ocs = [_dot(q2, sp_ref[c]) for q2, c in zip(q2s, cs)]
        o = jnp.concatenate([o2[0:c_len] + o2[c_len:] + oc for o2, oc in zip(o2s, ocs)], axis=0)
        mu = _dot(o.astype(BF16), gmat_ref[...])
        d = o - mu
        var = _dot((d * d).astype(BF16), gmat_ref[...])
        hn = d * lax.rsqrt(var + LN_EPS)
        grows = pl.ds(pl.multiple_of(gi * (grp * c_len), grp * c_len), grp * c_len)
        g = rg_ref[grows, :].astype(F32)
        o_ref[grows, :] = (jax.nn.silu(g) * hn).astype(BF16)
        return carry

    lax.fori_loop(0, nc // grp, out_body, 0)


def _retention(lg, rq, rk, rv, rg, s0f, s0b, gmat):
    b, ls, _ = rq.shape
    nc = ls // RET_CHUNK
    seq = lambda w: pl.BlockSpec((None, ls, w), lambda bb: (bb, 0, 0))
    st = pl.BlockSpec((None, RET_KW, RET_WIDTH), lambda bb: (bb, 0, 0))
    return pl.pallas_call(
        functools.partial(_ret_kernel, nc=nc),
        grid=(b,),
        in_specs=[pl.BlockSpec(memory_space=pltpu.SMEM), seq(RET_KW), seq(RET_KW), seq(RET_WIDTH), seq(RET_WIDTH),
                  st, st, pl.BlockSpec((RET_WIDTH, RET_WIDTH), lambda bb: (0, 0))],
        out_specs=[seq(RET_WIDTH), st, st],
        out_shape=[jax.ShapeDtypeStruct((b, ls, RET_WIDTH), BF16),
                   jax.ShapeDtypeStruct((b, RET_KW, RET_WIDTH), F32),
                   jax.ShapeDtypeStruct((b, RET_KW, RET_WIDTH), F32)],
        scratch_shapes=[
            pltpu.VMEM((nc, RET_KW, RET_WIDTH), F32),
            pltpu.VMEM((nc, RET_KW, RET_WIDTH), F32),
            pltpu.VMEM((nc, 2 * RET_KW, RET_WIDTH), BF16),
            pltpu.VMEM((2, RET_CHUNK, RET_HEADS * RET_CHUNK), F32),
            pltpu.VMEM((4, RET_CHUNK, RET_KW), F32),
        ],
        compiler_params=_cparams(("arbitrary",)),
        name="retention",
    )(lg, rq, rk, rv, rg, s0f, s0b, gmat)


def _mixer_out_ln1(attn_ref, pool_ref, ret_ref, x_ref, m, w_ref, g_ref, b_ref, alpha):
    y = (_dot(attn_ref[...], w_ref[0:ATTN_WIDTH, :])
         + _dot(pool_ref[...], w_ref[ATTN_WIDTH:ATTN_WIDTH + POOL_WIDTH, :])
         + _dot(ret_ref[...], w_ref[ATTN_WIDTH + POOL_WIDTH:, :]))
    x1 = _layer_norm(alpha * x_ref[...] + m[2:3, :] * y, g_ref[...], b_ref[...])
    return x1, x1 * (1.0 + m[4:5, :]) + m[3:4, :]


def _outproj_kernel(attn_ref, pool_ref, ret_ref, x_ref, mod_ref, w_ref, g_ref, b_ref, router_ref,
                    x1_ref, t_ref, lg_ref, *, alpha):
    x1, t = _mixer_out_ln1(attn_ref, pool_ref, ret_ref, x_ref, mod_ref[...], w_ref, g_ref, b_ref, alpha)
    x1_ref[...] = x1
    _store_tile_rows(t_ref, (), t)
    lg_ref[...] = _dot(t.astype(BF16), router_ref[...])


def _mixer_row_specs(tm, d):
    row = lambda w: pl.BlockSpec((None, tm, w), lambda bb, j: (bb, j, 0))
    return [row(ATTN_WIDTH), row(POOL_WIDTH), row(RET_WIDTH), row(d)]


def _outproj_ln_route(attn, pool, ret, x, modtab, mod_row_fn, w_out, ln_g, ln_b, alpha, router):
    b, ls, d = x.shape
    tm = min(512, ls)
    nt = ls // tm
    row = lambda w: pl.BlockSpec((None, tm, w), lambda bb, j: (bb, j, 0))
    const = lambda shape: pl.BlockSpec(shape, lambda bb, j: tuple(0 for _ in shape))
    return pl.pallas_call(
        functools.partial(_outproj_kernel, alpha=alpha),
        grid=(b, nt),
        in_specs=_mixer_row_specs(tm, d) + [
            pl.BlockSpec((None, 6, d), lambda bb, j: (mod_row_fn(bb), 0, 0)),
            const((d, d)), const((1, d)), const((1, d)), const((d, LANES))],
        out_specs=[row(d), pl.BlockSpec((None, tm * SUBLANES, LANES), lambda bb, j: (bb, j, 0)), row(LANES)],
        out_shape=[jax.ShapeDtypeStruct((b, ls, d), F32),
                   jax.ShapeDtypeStruct((b, ls * SUBLANES, LANES), F32),
                   jax.ShapeDtypeStruct((b, ls, LANES), F32)],
        compiler_params=_cparams(("arbitrary", "arbitrary")),
        name="outproj_ln_route",
    )(attn, pool, ret, x, modtab, w_out, ln_g, ln_b, router)


N_FFN_CHUNKS = FFN_DIM // FFN_CHUNK
N_FFN_GAPS = 3 * N_FFN_CHUNKS


def _swiglu_rows(tb, wg_ref, wu_ref, wd_ref, between=None):
    gap = (lambda j: None) if between is None else between
    f = None
    for c in range(N_FFN_CHUNKS):
        sl = slice(c * FFN_CHUNK, (c + 1) * FFN_CHUNK)
        g = _dot(tb, wg_ref[:, sl])
        gap(3 * c)
        u = _dot(tb, wu_ref[:, sl])
        gap(3 * c + 1)
        a = (jax.nn.silu(g) * u).astype(BF16)
        part = _dot(a, wd_ref[sl, :])
        gap(3 * c + 2)
        f = part if f is None else f + part
    return f


def _dense_post_kernel(attn_ref, pool_ref, ret_ref, x_ref, mod_ref, w_ref, g1_ref, b1_ref,
                       wg_ref, wu_ref, wd_ref, g2_ref, b2_ref, o_ref, *, alpha):
    m = mod_ref[...]
    x1, t = _mixer_out_ln1(attn_ref, pool_ref, ret_ref, x_ref, m, w_ref, g1_ref, b1_ref, alpha)
    f = _swiglu_rows(t.astype(BF16), wg_ref, wu_ref, wd_ref)
    o_ref[...] = _layer_norm(alpha * x1 + m[5:6, :] * f, g2_ref[...], b2_ref[...])


def _dense_post(attn, pool, ret, x, modtab, mod_row_fn, w_out, g1, b1, wg, wu, wd, g2, b2, alpha):
    b, ls, d = x.shape
    tm = min(512, ls)
    nt = ls // tm
    const = lambda shape: pl.BlockSpec(shape, lambda bb, j: tuple(0 for _ in shape),
                                       pipeline_mode=pl.Buffered(1))
    return pl.pallas_call(
        functools.partial(_dense_post_kernel, alpha=alpha),
        grid=(b, nt),
        in_specs=_mixer_row_specs(tm, d) + [
            pl.BlockSpec((None, 6, d), lambda bb, j: (mod_row_fn(bb), 0, 0)),
            const((d, d)), const((1, d)), const((1, d)),
            const((d, FFN_DIM)), const((d, FFN_DIM)), const((FFN_DIM, d)),
            const((1, d)), const((1, d))],
        out_specs=pl.BlockSpec((None, tm, d), lambda bb, j: (bb, j, 0)),
        out_shape=jax.ShapeDtypeStruct((b, ls, d), F32),
        compiler_params=_cparams(("arbitrary", "arbitrary")),
        name="dense_post",
    )(attn, pool, ret, x, modtab, w_out, g1, b1, wg, wu, wd, g2, b2)


ROUTE_TILE = 512
MOE_TILE = 512
ROW_TILE = 256


def _route_kernel(lg_ref, tri_ref, o_ref, ot_ref, cnt_ref, carry_ref):
    @pl.when(pl.program_id(0) == 0)
    def _():
        carry_ref[...] = jnp.zeros_like(carry_ref)

    tm = lg_ref.shape[0]
    lane = lax.broadcasted_iota(jnp.int32, (tm, LANES), 1)
    l = jnp.where(lane < N_EXPERTS, lg_ref[...], -jnp.inf)
    m1 = jnp.max(l, axis=1, keepdims=True)
    i1 = jnp.min(jnp.where(l == m1, lane, LANES), axis=1, keepdims=True)
    l2 = jnp.where(lane == i1, -jnp.inf, l)
    m2 = jnp.max(l2, axis=1, keepdims=True)
    i2 = jnp.min(jnp.where(l2 == m2, lane, LANES), axis=1, keepdims=True)
    e = jnp.exp(m2 - m1)
    w1 = 1.0 / (1.0 + e)
    w2 = e / (1.0 + e)
    oh = jnp.where((lane == i1) | (lane == i2), 1.0, 0.0)
    carry = carry_ref[0:1, :]
    cum = _dot(tri_ref[...], oh.astype(BF16)) + carry
    r1 = jnp.sum(jnp.where(lane == i1, cum, 0.0), axis=1, keepdims=True)
    r2 = jnp.sum(jnp.where(lane == i2, cum, 0.0), axis=1, keepdims=True)
    new = carry + jnp.sum(oh, axis=0, keepdims=True)
    carry_ref[...] = jnp.broadcast_to(new, carry_ref.shape)
    cnt_ref[...] = jnp.broadcast_to(new, cnt_ref.shape)
    tab = jnp.where(lane == 0, i1.astype(F32),
                    jnp.where(lane == 1, i2.astype(F32),
                              jnp.where(lane == 2, r1,
                                        jnp.where(lane == 3, r2,
                                                  jnp.where(lane == 4, w1,
                                                            jnp.where(lane == 5, w2, 0.0))))))
    o_ref[...] = tab
    ot_ref[...] = tab.T[0:8, :]


def _route(logits):
    n = logits.shape[0]
    tm = min(ROUTE_TILE, n)
    tri = jnp.asarray(np.tril(np.ones((tm, tm), np.float32), -1), dtype=BF16)
    return pl.pallas_call(
        _route_kernel,
        grid=(n // tm,),
        in_specs=[pl.BlockSpec((tm, LANES), lambda i: (i, 0)), pl.BlockSpec((tm, tm), lambda i: (0, 0))],
        out_specs=[pl.BlockSpec((tm, LANES), lambda i: (i, 0)), pl.BlockSpec((8, tm), lambda i: (0, i)),
                   pl.BlockSpec((8, LANES), lambda i: (0, 0))],
        out_shape=[jax.ShapeDtypeStruct((n, LANES), F32), jax.ShapeDtypeStruct((8, n), F32),
                   jax.ShapeDtypeStruct((8, LANES), F32)],
        scratch_shapes=[pltpu.VMEM((8, LANES), F32)],
        compiler_params=_cparams(("arbitrary",)),
        name="route",
    )(logits, tri)


def _invert_kernel(p1_ref, p2_ref, lo_ref, hi_ref, dst_ref, *, n):
    for e in range(N_EXPERTS + 1):
        def fill(s, carry):
            dst_ref[s] = 2 * n + (s & (MOE_TILE - 1))
            return carry

        lax.fori_loop(lo_ref[e], hi_ref[e], fill, 0)

    def body(t, carry):
        dst_ref[p1_ref[t]] = t
        dst_ref[p2_ref[t]] = n + t
        return carry

    lax.fori_loop(0, n, body, 0, unroll=16)


def _invert(pos1, pos2, fill_lo, fill_hi, nslots):
    n = pos1.shape[0]
    smem = pl.BlockSpec(memory_space=pltpu.SMEM)
    return pl.pallas_call(
        functools.partial(_invert_kernel, n=n),
        in_specs=[smem, smem, smem, smem],
        out_specs=smem,
        out_shape=jax.ShapeDtypeStruct((nslots,), jnp.int32),
        name="invert",
    )(pos1, pos2, fill_lo, fill_hi)


def _slot_source_row(v, n):
    if n & (n - 1) == 0:
        return v & (n - 1)
    return jnp.where(v >= 2 * n, v - 2 * n, jnp.where(v >= n, v - n, v))


def _moe_kernel(te_ref, nt_ref, inv_ref, t_ref, wg_ref, wu_ref, wd_ref, y_ref, xbuf, obuf, gsem, ssem, *, n):
    del te_ref
    i = pl.program_id(0)
    nt = nt_ref[0]
    tm = MOE_TILE
    slot = i & 1
    other = 1 - slot

    def row_tile(r):
        return pl.ds(pl.multiple_of(r * SUBLANES, SUBLANES), SUBLANES)

    def gather_row(tile, r, buf):
        src = _slot_source_row(inv_ref[tile * tm + r], n)
        pltpu.make_async_copy(t_ref.at[row_tile(src)], xbuf.at[buf, row_tile(r)], gsem.at[buf]).start()

    def scatter_row(tile, r, buf, real):
        dst = jnp.where(real, inv_ref[tile * tm + r], 2 * n + r)
        pltpu.make_async_copy(obuf.at[buf, row_tile(r)], y_ref.at[row_tile(dst)], ssem.at[buf]).start()

    def wait_gather(buf):
        pltpu.make_async_copy(t_ref.at[pl.ds(0, tm * SUBLANES)], xbuf.at[buf], gsem.at[buf]).wait()

    def wait_scatter(buf):
        pltpu.make_async_copy(obuf.at[buf], y_ref.at[pl.ds(0, tm * SUBLANES)], ssem.at[buf]).wait()

    @pl.when(i == 0)
    def _():
        obuf[1] = jnp.zeros(obuf.shape[1:], obuf.dtype)

        def first(r, carry):
            gather_row(0, r, 0)
            return carry

        lax.fori_loop(0, tm, first, 0, unroll=8)

    def main(cur, oth):
        wait_gather(cur)
        nxt = jnp.minimum(i + 1, nt - 1)
        prv = jnp.maximum(i - 1, 0)
        real = i >= 1
        per = -(-tm // N_FFN_GAPS)

        def between(j):
            for r in range(j * per, min((j + 1) * per, tm)):
                gather_row(nxt, r, oth)
                scatter_row(prv, r, oth, real)

        x = _load_tile_rows(xbuf, (cur,), tm).astype(BF16)
        _store_tile_rows(obuf, (cur,), _swiglu_rows(x, wg_ref, wu_ref, wd_ref, between))
        wait_scatter(oth)

    @pl.when((i < nt) & (slot == 0))
    def _():
        main(0, 1)

    @pl.when((i < nt) & (slot == 1))
    def _():
        main(1, 0)

    @pl.when(i == nt)
    def _():
        last = nt - 1
        buf = last & 1

        def tail(r, carry):
            scatter_row(last, r, buf, True)
            return carry

        lax.fori_loop(0, tm, tail, 0, unroll=8)
        wait_scatter(buf)
        wait_gather(nt & 1)


def _moe_ffn(tile_expert, ntiles, inv, t, wg, wu, wd):
    n = t.shape[0] // SUBLANES
    d = D_MODEL
    tm = MOE_TILE
    assert n >= tm
    nt_max = inv.shape[0] // tm
    wspec = lambda shape: pl.BlockSpec((None,) + shape, lambda i, te, nt, iv: (te[i], 0, 0))
    return pl.pallas_call(
        functools.partial(_moe_kernel, n=n),
        grid_spec=pltpu.PrefetchScalarGridSpec(
            num_scalar_prefetch=3,
            grid=(nt_max,),
            in_specs=[pl.BlockSpec(memory_space=pl.ANY),
                      wspec((d, FFN_DIM)), wspec((d, FFN_DIM)), wspec((FFN_DIM, d))],
            out_specs=pl.BlockSpec(memory_space=pl.ANY),
            scratch_shapes=[pltpu.VMEM((2, tm * SUBLANES, LANES), F32), pltpu.VMEM((2, tm * SUBLANES, LANES), F32),
                            pltpu.SemaphoreType.DMA((2,)), pltpu.SemaphoreType.DMA((2,))],
        ),
        out_shape=jax.ShapeDtypeStruct(((2 * n + tm) * SUBLANES, LANES), F32),
        compiler_params=_cparams(("arbitrary",)),
        name="moe_ffn",
    )(tile_expert, ntiles, inv, t, wg, wu, wd)


def _combine_kernel(y1_ref, y2_ref, x1_ref, rt_ref, mod_ref, g_ref, b_ref, o_ref, *, alpha):
    rt = rt_ref[...]
    tm = x1_ref.shape[0]
    f = rt[:, 4:5] * _load_tile_rows(y1_ref, (), tm) + rt[:, 5:6] * _load_tile_rows(y2_ref, (), tm)
    m = mod_ref[...]
    o_ref[...] = _layer_norm(alpha * x1_ref[...] + m[5:6, :] * f, g_ref[...], b_ref[...])


def _combine_ln(y, x1, rtab, modtab, l, ln_g, ln_b, alpha):
    n, d = x1.shape
    tm = min(ROW_TILE, n)
    per_b = l // tm
    nblk = n // tm
    const = lambda shape: pl.BlockSpec(shape, lambda i: tuple(0 for _ in shape))
    return pl.pallas_call(
        functools.partial(_combine_kernel, alpha=alpha),
        grid=(nblk,),
        in_specs=[pl.BlockSpec((tm * SUBLANES, LANES), lambda i: (i, 0)),
                  pl.BlockSpec((tm * SUBLANES, LANES), lambda i: (nblk + i, 0)),
                  pl.BlockSpec((tm, d), lambda i: (i, 0)),
                  pl.BlockSpec((tm, LANES), lambda i: (i, 0)),
                  pl.BlockSpec((None, 6, d), lambda i: (i // per_b, 0, 0)),
                  const((1, d)), const((1, d))],
        out_specs=pl.BlockSpec((tm, d), lambda i: (i, 0)),
        out_shape=jax.ShapeDtypeStruct((n, d), F32),
        compiler_params=_cparams(("arbitrary",)),
        name="combine_ln",
    )(y, y, x1, rtab, modtab, ln_g, ln_b)


def _moe_layer(t, logits, x1, modtab, l, wg, wu, wd, ln_g, ln_b, alpha):
    n = x1.shape[0]
    rtab, rtab_t, cnt = _route(logits)
    counts = cnt[0, :N_EXPERTS].astype(jnp.int32)
    tiles_e = (counts + MOE_TILE - 1) // MOE_TILE
    tile_end = jnp.cumsum(tiles_e)
    tile_start = tile_end - tiles_e
    ntiles = tile_end[-1:]
    nt_max = (2 * n) // MOE_TILE + N_EXPERTS + 1
    tile_expert = jnp.minimum(
        jnp.sum((jnp.arange(nt_max)[:, None] >= tile_end[None, :]).astype(jnp.int32), axis=1),
        N_EXPERTS - 1).astype(jnp.int32)
    e1 = rtab_t[0].astype(jnp.int32)
    e2 = rtab_t[1].astype(jnp.int32)
    row_start = tile_start * MOE_TILE
    pos1 = row_start[e1] + rtab_t[2].astype(jnp.int32)
    pos2 = row_start[e2] + rtab_t[3].astype(jnp.int32)
    fill_lo = jnp.concatenate([row_start + counts, tile_end[-1:] * MOE_TILE]).astype(jnp.int32)
    fill_hi = jnp.concatenate([tile_end * MOE_TILE, jnp.full((1,), nt_max * MOE_TILE)]).astype(jnp.int32)
    inv = _invert(pos1, pos2, fill_lo, fill_hi, nt_max * MOE_TILE)
    y = _moe_ffn(tile_expert, ntiles.astype(jnp.int32), inv, t, wg, wu, wd)
    return _combine_ln(y, x1, rtab, modtab, l, ln_g, ln_b, alpha)


def _prep_w_in(w):
    hd = HEAD_DIM
    q = w[:, 0:512] * (hd ** -0.5 * LOG2E)
    k0, k1 = w[:, 512:576], w[:, 576:640]
    v0, v1 = w[:, 640:704], w[:, 704:768]
    rest = w[:, 768:]
    return jnp.concatenate([q, k0, k0, k1, k1, v0, v0, v1, v1, rest], axis=1).astype(BF16)


def _block_diag(mats):
    n = len(mats)
    rows = []
    for i, m in enumerate(mats):
        rows.append(jnp.concatenate([m if j == i else jnp.zeros_like(m) for j in range(n)], axis=1))
    return jnp.concatenate(rows, axis=0)


def kernel(x, c, ctx, c_ctx, w_mod, b_mod, w_in, attn_sink, pool_w, pool_scale, ret_log_decay_fwd,
           ret_log_decay_bwd, w_out, ln1_g, ln1_b, ln2_g, ln2_b, ffn_w_gate, ffn_w_up, ffn_w_down,
           moe_router, moe_w_gate, moe_w_up, moe_w_down):
    b, l, d = x.shape
    lc = ctx.shape[1]
    depth = w_in.shape[0]
    alpha = (2.0 * depth) ** 0.25
    assert d == D_MODEL and l % 256 == 0 and lc % 256 == 0 and l % GRID_W == 0

    n_rows = ((b + 1 + 7) // 8) * 8
    c_all = jnp.concatenate([c, c_ctx[None, :], jnp.zeros((n_rows - b - 1, d), F32)], axis=0)
    mod_all = _modulation(c_all, w_mod, b_mod).reshape(depth, n_rows, 6, d)
    lat_row = lambda bb: bb
    ctx_row = lambda bb: b

    rope_tabs = _rope_tables(l)
    band = _pool_band_matrices()
    gmat = jnp.asarray(np.kron(np.eye(RET_HEADS), np.ones((RET_DV, RET_DV))) / RET_DV, dtype=BF16)
    zero_state = jnp.zeros((b, RET_KW, RET_WIDTH), F32)

    xc = ctx
    for i in range(depth):
        last = i == depth - 1
        modtab = mod_all[i]
        w2 = _prep_w_in(w_in[i])
        wo = w_out[i].astype(BF16)
        sink = attn_sink[i].astype(F32)
        pw = _block_diag([pool_w[i, g] for g in range(len(POOL_WINDOWS))]).astype(BF16)
        ps = pool_scale[i].reshape(1, POOL_WIDTH).astype(F32)
        lg = jnp.stack([ret_log_decay_fwd[i], ret_log_decay_bwd[i]]).astype(F32)
        g1, b1 = ln1_g[i].reshape(1, d), ln1_b[i].reshape(1, d)
        g2, b2 = ln2_g[i].reshape(1, d), ln2_b[i].reshape(1, d)

        qc, kdc, vdc, uc, rqc, rkc, rvc, rgc = _inproj(xc, modtab, ctx_row, w2, None)
        ret_c, s_f, s_b = _retention(lg, rqc, rkc, rvc, rgc, zero_state, zero_state, gmat)

        q, kd, vd, u, rq, rk, rv, rg = _inproj(x, modtab, lat_row, w2, rope_tabs)
        attn = _attention_latent(sink, q, kd, vd, kdc, vdc)
        pool = _pool(u, band, pw, ps)
        ret, _, _ = _retention(lg, rq, rk, rv, rg, s_f, s_b, gmat)

        j = i // 2
        if i % 2 == 0:
            wg, wu, wd = ffn_w_gate[j].astype(BF16), ffn_w_up[j].astype(BF16), ffn_w_down[j].astype(BF16)
            x = _dense_post(attn, pool, ret, x, modtab, lat_row, wo, g1, b1, wg, wu, wd, g2, b2, alpha)
        else:
            wg, wu, wd = moe_w_gate[j].astype(BF16), moe_w_up[j].astype(BF16), moe_w_down[j].astype(BF16)
            router = jnp.pad(moe_router[j], ((0, 0), (0, LANES - N_EXPERTS))).astype(BF16)
            x1, t, logits = _outproj_ln_route(attn, pool, ret, x, modtab, lat_row, wo, g1, b1, alpha, router)
            x = _moe_layer(t.reshape(b * l * SUBLANES, LANES), logits.reshape(b * l, LANES), x1.reshape(b * l, d),
                           modtab, l, wg, wu, wd, g2, b2, alpha).reshape(b, l, d)

        if not last:
            attn_c = _attention_ctx(sink, qc, kdc, vdc)
            pool_c = _pool(uc, band, pw, ps)
            if i % 2 == 0:
                xc = _dense_post(attn_c, pool_c, ret_c, xc, modtab, ctx_row, wo, g1, b1, wg, wu, wd, g2, b2,
                                 alpha)
            else:
                x1c, tc, logits_c = _outproj_ln_route(attn_c, pool_c, ret_c, xc, modtab, ctx_row, wo, g1, b1,
                                                      alpha, router)
                ctx_mod = jnp.broadcast_to(modtab[b:b + 1], (b, 6, d))
                xc = _moe_layer(tc.reshape(b * lc * SUBLANES, LANES), logits_c.reshape(b * lc, LANES),
                                x1c.reshape(b * lc, d),
                                ctx_mod, lc, wg, wu, wd, g2, b2, alpha).reshape(b, lc, d)
    return x
```

```python
import functools
import math

import numpy as np
import jax
import jax.numpy as jnp
from jax import lax
from jax.experimental import pallas as pl
from jax.experimental.pallas import tpu as pltpu

F32 = jnp.float32
BF16 = jnp.bfloat16

D_MODEL = 1024
GRID_W = 64
HEAD_DIM = 64
N_HEADS = 8
N_KV_HEADS = 2
ATTN_WIDTH = N_HEADS * HEAD_DIM
ATTN_BLOCK = 128
ATTN_BLOCKS_PER_STEP = 4
ROPE_BASE = 10000.0
POOL_WINDOWS = (2, 4, 8, 16)
POOL_WIDTH = 256
POOL_GROUP = 64
POOL_TILE = 256
POOL_HALO = 16
RET_HEADS = 4
RET_DK = 32
RET_DV = 64
RET_WIDTH = 256
RET_KW = RET_HEADS * RET_DK
RET_CHUNK = 128
FFN_DIM = 2816
FFN_CHUNK = 256
N_EXPERTS = 8
LN_EPS = 1e-5
NEG_INF = -1e30
LOG2E = math.log2(math.e)
LANES = 128
IN_WIDTH = 1792

VMEM_LIMIT = 56 * 1024 * 1024


def _cparams(sem):
    return pltpu.CompilerParams(dimension_semantics=sem, vmem_limit_bytes=VMEM_LIMIT)


def _dot(a, b):
    return jnp.dot(a, b, preferred_element_type=F32)


def _dot_nt(a, b):
    return lax.dot_general(a, b, (((1,), (1,)), ((), ())), preferred_element_type=F32)


def _dot_tn(a, b):
    return lax.dot_general(a, b, (((0,), (0,)), ((), ())), preferred_element_type=F32)


def _split_bf16(a):
    hi = a.astype(BF16)
    lo = (a - hi.astype(F32)).astype(BF16)
    return hi, lo


def _dot3(a, b):
    ah, al = _split_bf16(a)
    bh, bl = _split_bf16(b)
    return _dot(ah, bh) + _dot(ah, bl) + _dot(al, bh)


SUBLANES = 8
ROW_TILES = D_MODEL // LANES


def _store_tile_rows(ref, idx, val):
    rows = val.shape[0]
    for c in range(ROW_TILES):
        ref[idx + (pl.ds(c, rows, stride=SUBLANES), slice(None))] = val[:, c * LANES:(c + 1) * LANES]


def _load_tile_rows(ref, idx, rows):
    return jnp.concatenate([ref[idx + (pl.ds(c, rows, stride=SUBLANES), slice(None))]
                            for c in range(ROW_TILES)], axis=1)


def _layer_norm(z, g, b):
    mu = jnp.mean(z, axis=-1, keepdims=True)
    d = z - mu
    var = jnp.mean(d * d, axis=-1, keepdims=True)
    return d * lax.rsqrt(var + LN_EPS) * g + b


def _mod_kernel(c_ref, w_ref, b_ref, o_ref):
    s = jax.nn.silu(c_ref[...])
    o_ref[...] = _dot3(s, w_ref[...]) + b_ref[...]


def _modulation(c_all, w_mod, b_mod):
    depth, d, n6 = w_mod.shape
    r = c_all.shape[0]
    bn = 1536
    return pl.pallas_call(
        _mod_kernel,
        grid=(depth, n6 // bn),
        in_specs=[
            pl.BlockSpec((r, d), lambda i, j: (0, 0)),
            pl.BlockSpec((None, d, bn), lambda i, j: (i, 0, j)),
            pl.BlockSpec((None, 1, bn), lambda i, j: (i, 0, j)),
        ],
        out_specs=pl.BlockSpec((None, r, bn), lambda i, j: (i, 0, j)),
        out_shape=jax.ShapeDtypeStruct((depth, r, n6), F32),
        compiler_params=_cparams(("arbitrary", "arbitrary")),
        name="modulation",
    )(c_all, w_mod, b_mod.reshape(depth, 1, n6))


KV_WIDTH = N_KV_HEADS * HEAD_DIM
_IN_OUT_WIDTHS = (ATTN_WIDTH, KV_WIDTH, KV_WIDTH, POOL_WIDTH, RET_KW, RET_KW, RET_WIDTH, RET_WIDTH)
_IN_COL_STARTS = tuple(int(c) for c in np.cumsum((0,) + _IN_OUT_WIDTHS))


def _rope_groups(a, cos, s_prev, s_next):
    outs = []
    for g in range(a.shape[1] // LANES):
        ag = a[:, g * LANES:(g + 1) * LANES]
        outs.append(ag * cos + pltpu.roll(ag, 16, 1) * s_prev + pltpu.roll(ag, LANES - 16, 1) * s_next)
    return outs[0] if len(outs) == 1 else jnp.concatenate(outs, axis=1)


def _inproj_kernel(*refs, rope):
    if rope:
        x_ref, mod_ref, w_ref, cos_ref, sp_ref, sn_ref = refs[:6]
        outs = refs[6:]
    else:
        x_ref, mod_ref, w_ref = refs[:3]
        outs = refs[3:]
    q_ref, k_ref, v_ref, u_ref, rq_ref, rk_ref, rv_ref, rg_ref = outs
    m = mod_ref[...]
    h = (x_ref[...] * (1.0 + m[1:2, :]) + m[0:1, :]).astype(BF16)

    def mm(j0, j1):
        return _dot(h, w_ref[:, _IN_COL_STARTS[j0]:_IN_COL_STARTS[j1]])

    q = mm(0, 1)
    kv = mm(1, 3)
    k, v = kv[:, :KV_WIDTH], kv[:, KV_WIDTH:]
    if rope:
        cos, sp, sn = cos_ref[...], sp_ref[...], sn_ref[...]
        q = _rope_groups(q, cos, sp, sn)
        k = _rope_groups(k, cos, sp, sn)
    q_ref[...] = q.astype(BF16)
    k_ref[...] = k.astype(BF16)
    v_ref[...] = v.astype(BF16)
    u_ref[...] = mm(3, 4).astype(BF16)
    rqk = mm(4, 6)
    rq_ref[...] = rqk[:, :RET_KW].astype(BF16)
    rk_ref[...] = (rqk[:, RET_KW:] * (RET_DK ** -0.5)).astype(BF16)
    rv_ref[...] = mm(6, 7).astype(BF16)
    rg_ref[...] = mm(7, 8).astype(BF16)


def _inproj(x, modtab, mod_row_fn, w2, rope_tabs):
    b, ls, d = x.shape
    tm = min(1024, ls)
    nt = ls // tm
    rope = rope_tabs is not None
    in_specs = [
        pl.BlockSpec((None, tm, d), lambda j, bb: (bb, j, 0)),
        pl.BlockSpec((None, 6, d), lambda j, bb: (mod_row_fn(bb), 0, 0)),
        pl.BlockSpec((d, IN_WIDTH), lambda j, bb: (0, 0)),
    ]
    args = [x, modtab, w2]
    if rope:
        in_specs += [pl.BlockSpec((tm, LANES), lambda j, bb: (j, 0))] * 3
        args += list(rope_tabs)
    out_specs = [pl.BlockSpec((None, tm, w), lambda j, bb: (bb, j, 0)) for w in _IN_OUT_WIDTHS]
    out_shape = [jax.ShapeDtypeStruct((b, ls, w), BF16) for w in _IN_OUT_WIDTHS]
    return pl.pallas_call(
        functools.partial(_inproj_kernel, rope=rope),
        grid=(nt, b),
        in_specs=in_specs,
        out_specs=out_specs,
        out_shape=out_shape,
        compiler_params=_cparams(("arbitrary", "arbitrary")),
        name="inproj_rope" if rope else "inproj",
    )(*args)


def _rope_tables(l):
    p = jnp.arange(l)
    row = (p // GRID_W).astype(F32)
    col = (p % GRID_W).astype(F32)
    quarter = HEAD_DIM // 4
    inv = ROPE_BASE ** (-jnp.arange(quarter, dtype=F32) / quarter)
    lane = np.arange(LANES)
    j = lane % HEAD_DIM
    use_row = jnp.asarray(j < HEAD_DIM // 2)
    freq = inv[jnp.asarray(j % quarter)]
    pos = jnp.where(use_row[None, :], row[:, None], col[:, None])
    ang = pos * freq[None, :]
    cos, sin = jnp.cos(ang), jnp.sin(ang)
    second = jnp.asarray((lane % (2 * quarter)) >= quarter)[None, :]
    s_prev = jnp.where(second, sin, 0.0)
    s_next = jnp.where(second, 0.0, -sin)
    return cos, s_prev, s_next


def _attn_blocks(sink_ref, q_ref, o_ref, blocks):
    nq = ATTN_BLOCK
    lane = lax.broadcasted_iota(jnp.int32, (1, LANES), 1)
    rid = lax.broadcasted_iota(jnp.int32, (4, 1, 1), 0)
    chains = [(row0, kvh, keys[kvh], ok) for row0, keys, ok in blocks for kvh in range(N_KV_HEADS)]

    scores = []
    for row0, kvh, (k_lo, k_hi, _), _ in chains:
        rows = pl.ds(row0, nq)
        qg = jnp.concatenate([q_ref[rows, 256 * kvh:256 * kvh + LANES],
                              q_ref[rows, 256 * kvh + LANES:256 * kvh + 2 * LANES]], axis=0)
        scores.append(jnp.concatenate([_dot_nt(qg, k_lo), _dot_nt(qg, k_hi)], axis=0).reshape(4, nq, -1))

    probs, sink_terms = [], []
    for (row0, kvh, _, ok), s in zip(chains, scores):
        if ok is not None:
            ok_left, ok_right = ok
            s = jnp.concatenate([jnp.where(ok_left[None], s[:, :, 0:nq], NEG_INF), s[:, :, nq:2 * nq],
                                 jnp.where(ok_right[None], s[:, :, 2 * nq:3 * nq], NEG_INF), s[:, :, 3 * nq:]],
                                axis=2)
        h0 = 4 * kvh
        sk = jnp.where(rid == 0, sink_ref[h0],
                       jnp.where(rid == 1, sink_ref[h0 + 2],
                                 jnp.where(rid == 2, sink_ref[h0 + 1], sink_ref[h0 + 3]))) * LOG2E
        m = jnp.maximum(jnp.max(s, axis=2, keepdims=True), sk)
        probs.append(jnp.exp2((s - m).astype(BF16)).reshape(4 * nq, -1))
        sink_terms.append(jnp.exp2(sk - m).reshape(4 * nq, 1))

    outs = [_dot(e, v_ext) for e, (_, _, (_, _, v_ext), _) in zip(probs, chains)]

    for (row0, kvh, _, _), o, es in zip(chains, outs, sink_terms):
        rows = pl.ds(row0, nq)
        swapped = pltpu.roll(o, HEAD_DIM, 1)
        even = o[0:2 * nq] / (swapped[0:2 * nq] + es[0:2 * nq])
        odd = swapped[2 * nq:] / (o[2 * nq:] + es[2 * nq:])
        g0 = jnp.where(lane < HEAD_DIM, even[0:nq], odd[0:nq])
        g1 = jnp.where(lane < HEAD_DIM, even[nq:], odd[nq:])
        o_ref[rows, 256 * kvh:256 * kvh + LANES] = g0.astype(BF16)
        o_ref[rows, 256 * kvh + LANES:256 * kvh + 2 * LANES] = g1.astype(BF16)


def _attn_key_parts(k, v):
    lane = lax.broadcasted_iota(jnp.int32, (1, LANES), 1)
    low = lane < HEAD_DIM
    swap = lambda a: jnp.concatenate([a[:, HEAD_DIM:], a[:, :HEAD_DIM]], axis=1)
    ks, vs = swap(k), swap(v)
    zero, one = jnp.zeros_like(k), jnp.ones_like(v)
    return [(jnp.where(low, k, zero), jnp.where(low, zero, ks), jnp.where(low, v, one)),
            (jnp.where(low, ks, zero), jnp.where(low, zero, k), jnp.where(low, vs, one))]


def _attn_lat_kernel(sink_ref, q_ref, kl_ref, kc_ref, kr_ref, kx_ref,
                     vl_ref, vc_ref, vr_ref, vx_ref, o_ref, *, nsteps, nblk):
    i = pl.program_id(1)
    nq = ATTN_BLOCK
    rows = lax.broadcasted_iota(jnp.int32, (nq, nq), 0)
    cols = lax.broadcasted_iota(jnp.int32, (nq, nq), 1)
    upper = cols >= rows
    lower = cols <= rows
    parts = [_attn_key_parts(kr[...], vr[...])
             for kr, vr in ((kl_ref, vl_ref), (kc_ref, vc_ref), (kr_ref, vr_ref), (kx_ref, vx_ref))]
    per_head = []
    for kvh in range(N_KV_HEADS):
        left, cen, right, ctx = [p[kvh] for p in parts]
        own = [tuple(t[j * nq:(j + 1) * nq] for t in cen) for j in range(nblk)]
        per_head.append(([left] + own + [right], ctx))
    blocks = []
    for j in range(nblk):
        keys = [tuple(jnp.concatenate([p[t] for p in seq[j:j + 3]] + [ctx[t]], axis=0) for t in range(3))
                for seq, ctx in per_head]
        ok_left = upper & (i >= 1) if j == 0 else upper
        ok_right = lower & (i <= nsteps - 2) if j == nblk - 1 else lower
        blocks.append((j * nq, keys, (ok_left, ok_right)))
    _attn_blocks(sink_ref, q_ref, o_ref, blocks)


def _attn_ctx_kernel(sink_ref, q_ref, kx_ref, vx_ref, o_ref):
    _attn_blocks(sink_ref, q_ref, o_ref, [(0, _attn_key_parts(kx_ref[...], vx_ref[...]), None)])


def _attention_latent(sink, q, kd, vd, kdc, vdc):
    b, l, _ = q.shape
    lc = kdc.shape[1]
    nq = ATTN_BLOCK
    nb = l // nq
    nblk = ATTN_BLOCKS_PER_STEP
    assert nb % nblk == 0
    nsteps = nb // nblk
    smem = pl.BlockSpec(memory_space=pltpu.SMEM)
    edge = lambda f: pl.BlockSpec((None, nq, KV_WIDTH), f)
    left = lambda bb, i: (bb, jnp.maximum(nblk * i - 1, 0), 0)
    cen = lambda bb, i: (bb, i, 0)
    right = lambda bb, i: (bb, jnp.minimum(nblk * i + nblk, nb - 1), 0)
    mid = pl.BlockSpec((None, nblk * nq, KV_WIDTH), cen)
    ctx = pl.BlockSpec((None, lc, KV_WIDTH), lambda bb, i: (bb, 0, 0))
    return pl.pallas_call(
        functools.partial(_attn_lat_kernel, nsteps=nsteps, nblk=nblk),
        grid=(b, nsteps),
        in_specs=[smem, pl.BlockSpec((None, nblk * nq, ATTN_WIDTH), cen),
                  edge(left), mid, edge(right), ctx,
                  edge(left), mid, edge(right), ctx],
        out_specs=pl.BlockSpec((None, nblk * nq, ATTN_WIDTH), cen),
        out_shape=jax.ShapeDtypeStruct((b, l, ATTN_WIDTH), BF16),
        compiler_params=_cparams(("arbitrary", "arbitrary")),
        name="attn_latent",
    )(sink, q, kd, kd, kd, kdc, vd, vd, vd, vdc)


def _attention_ctx(sink, qc, kdc, vdc):
    b, lc, _ = qc.shape
    nq = ATTN_BLOCK
    smem = pl.BlockSpec(memory_space=pltpu.SMEM)
    ctx = pl.BlockSpec((None, lc, KV_WIDTH), lambda bb, i: (bb, 0, 0))
    return pl.pallas_call(
        _attn_ctx_kernel,
        grid=(b, lc // nq),
        in_specs=[smem, pl.BlockSpec((None, nq, ATTN_WIDTH), lambda bb, i: (bb, i, 0)), ctx, ctx],
        out_specs=pl.BlockSpec((None, nq, ATTN_WIDTH), lambda bb, i: (bb, i, 0)),
        out_shape=jax.ShapeDtypeStruct((b, lc, ATTN_WIDTH), BF16),
        compiler_params=_cparams(("arbitrary", "arbitrary")),
        name="attn_ctx",
    )(sink, qc, kdc, vdc)


def _pool_band_matrices():
    r = np.arange(POOL_TILE)[:, None]
    a = np.arange(POOL_TILE + 2 * POOL_HALO)[None, :] - POOL_HALO
    mats = [((a >= r - w // 2) & (a < r + w - w // 2)) for w in POOL_WINDOWS]
    return jnp.asarray(np.concatenate(mats, axis=0).astype(np.float32), dtype=BF16)


def _pool_kernel(u_ref, a_ref, w_ref, sc_ref, o_ref, pad_ref, *, ls):
    halo = POOL_HALO
    zeros = jnp.zeros((halo, POOL_WIDTH), BF16)
    pad_ref[0:halo, :] = zeros
    pad_ref[halo + ls:2 * halo + ls, :] = zeros
    pad_ref[halo:halo + ls, :] = u_ref[...]
    lane = lax.broadcasted_iota(jnp.int32, (1, POOL_WIDTH), 1)
    grp = lane // POOL_GROUP
    wl = jnp.where(grp == 0, POOL_WINDOWS[0],
                   jnp.where(grp == 1, POOL_WINDOWS[1],
                             jnp.where(grp == 2, POOL_WINDOWS[2], POOL_WINDOWS[3])))
    half = wl // 2

    ntile = ls // POOL_TILE
    grp_tiles = min(4, ntile)
    assert ntile % grp_tiles == 0
    nw = len(POOL_WINDOWS)

    def body(gi, carry):
        t0s = [pl.multiple_of((gi * grp_tiles + j) * POOL_TILE, POOL_TILE) for j in range(grp_tiles)]
        sums = [_dot(a_ref[...], pad_ref[pl.ds(t0, POOL_TILE + 2 * halo), :]) for t0 in t0s]
        ds = []
        for t0, sm in zip(t0s, sums):
            acc = sm[0:POOL_TILE]
            for g in range(1, nw):
                acc = jnp.where(grp == g, sm[g * POOL_TILE:(g + 1) * POOL_TILE], acc)
            p = t0 + lax.broadcasted_iota(jnp.int32, (POOL_TILE, 1), 0)
            hi = jnp.minimum(p + (wl - half), ls)
            lo = jnp.maximum(p - half, 0)
            cnt = (hi - lo).astype(F32)
            ut = u_ref[pl.ds(t0, POOL_TILE), :].astype(F32)
            ds.append((acc / cnt - ut).astype(BF16))
        rows = pl.ds(t0s[0], grp_tiles * POOL_TILE)
        o_ref[rows, :] = (_dot(jnp.concatenate(ds, axis=0), w_ref[...]) * sc_ref[...]).astype(BF16)
        return carry

    lax.fori_loop(0, ntile // grp_tiles, body, 0)


def _pool(u, band, wblk, scale):
    b, ls, _ = u.shape
    return pl.pallas_call(
        functools.partial(_pool_kernel, ls=ls),
        grid=(b,),
        in_specs=[
            pl.BlockSpec((None, ls, POOL_WIDTH), lambda bb: (bb, 0, 0)),
            pl.BlockSpec(band.shape, lambda bb: (0, 0)),
            pl.BlockSpec((POOL_WIDTH, POOL_WIDTH), lambda bb: (0, 0)),
            pl.BlockSpec((1, POOL_WIDTH), lambda bb: (0, 0)),
        ],
        out_specs=pl.BlockSpec((None, ls, POOL_WIDTH), lambda bb: (bb, 0, 0)),
        out_shape=jax.ShapeDtypeStruct((b, ls, POOL_WIDTH), BF16),
        scratch_shapes=[pltpu.VMEM((ls + 2 * POOL_HALO, POOL_WIDTH), BF16)],
        compiler_params=_cparams(("arbitrary",)),
        name="pool",
    )(u, band, wblk, scale)


def _ret_kernel(lg_ref, rq_ref, rk_ref, rv_ref, rg_ref, s0f_ref, s0b_ref, gmat_ref,
                o_ref, sf_ref, sb_ref,
                kvf_ref, kvb_ref, sp_ref, dm_ref, tab_ref, *, nc):
    c_len = RET_CHUNK
    kw, vw = RET_KW, RET_WIDTH

    def per_head(idx, d):
        return jnp.where(idx == 0, lg_ref[d, 0],
                         jnp.where(idx == 1, lg_ref[d, 1],
                                   jnp.where(idx == 2, lg_ref[d, 2], lg_ref[d, 3])))

    hk = lax.broadcasted_iota(jnp.int32, (1, kw), 1) // RET_DK
    hv = lax.broadcasted_iota(jnp.int32, (1, vw), 1) // RET_DV
    n_col = lax.broadcasted_iota(jnp.int32, (c_len, 1), 0).astype(F32)
    lgk_f, lgk_b = per_head(hk, 0), per_head(hk, 1)
    tab_ref[0] = jnp.exp(lgk_f * (c_len - 1.0 - n_col))
    tab_ref[1] = jnp.exp(lgk_b * n_col)
    tab_ref[2] = jnp.exp(lgk_f * (n_col + 1.0))
    tab_ref[3] = jnp.exp(lgk_b * (c_len - n_col))
    hs = lax.broadcasted_iota(jnp.int32, (1, RET_HEADS * c_len), 1) // c_len
    m_idx = (lax.broadcasted_iota(jnp.int32, (c_len, RET_HEADS * c_len), 1) & (c_len - 1)).astype(F32)
    n_idx = lax.broadcasted_iota(jnp.int32, (c_len, RET_HEADS * c_len), 0).astype(F32)
    rel = n_idx - m_idx
    dm_ref[0] = jnp.where(rel >= 0, jnp.exp(per_head(hs, 0) * jnp.maximum(rel, 0.0)), 0.0)
    dm_ref[1] = jnp.where(rel <= 0, jnp.exp(per_head(hs, 1) * jnp.maximum(-rel, 0.0)), 0.0)

    bd = (lax.broadcasted_iota(jnp.int32, (kw, vw), 0) // RET_DK) == (lax.broadcasted_iota(jnp.int32, (kw, vw), 1) // RET_DV)
    hk_col = lax.broadcasted_iota(jnp.int32, (kw, 1), 0) // RET_DK
    cd_f = jnp.exp(per_head(hk_col, 0) * float(c_len))
    cd_b = jnp.exp(per_head(hk_col, 1) * float(c_len))

    def rows(c):
        return pl.ds(pl.multiple_of(c * c_len, c_len), c_len)

    grp = min(4, nc)
    assert nc % grp == 0

    def kv_body(gi, carry):
        cs = [gi * grp + j for j in range(grp)]
        ks = [rk_ref[rows(c), :].astype(F32) for c in cs]
        vs = [rv_ref[rows(c), :] for c in cs]
        kf = [(k * tab_ref[0]).astype(BF16) for k in ks]
        kb = [(k * tab_ref[1]).astype(BF16) for k in ks]
        pf = [_dot_tn(a, v) for a, v in zip(kf, vs)]
        pb = [_dot_tn(a, v) for a, v in zip(kb, vs)]
        for j, c in enumerate(cs):
            kvf_ref[c] = jnp.where(bd, pf[j], 0.0)
            kvb_ref[c] = jnp.where(bd, pb[j], 0.0)
        return carry

    lax.fori_loop(0, nc // grp, kv_body, 0)

    def scan_body(j, carry):
        s_f, s_b = carry
        cb = nc - 1 - j
        sp_ref[j, 0:kw, :] = s_f.astype(BF16)
        sp_ref[cb, kw:2 * kw, :] = s_b.astype(BF16)
        return cd_f * s_f + kvf_ref[j], cd_b * s_b + kvb_ref[cb]

    s_f, s_b = lax.fori_loop(0, nc, scan_body, (s0f_ref[...], s0b_ref[...]))
    sf_ref[...] = s_f
    sb_ref[...] = s_b

    def out_body(gi, carry):
        cs = [gi * grp + j for j in range(grp)]
        qs = [rq_ref[rows(c), :] for c in cs]
        ks = [rk_ref[rows(c), :] for c in cs]
        vs = [rv_ref[rows(c), :] for c in cs]
        zk, zv = jnp.zeros_like(ks[0]), jnp.zeros_like(vs[0])
        ksts = [jnp.concatenate([jnp.where(hk == h, k, zk) for h in range(RET_HEADS)], axis=0) for k in ks]
        vsts = [jnp.concatenate([jnp.where(hv == h, v, zv) for h in range(RET_HEADS)], axis=0) for v in vs]
        scs = [_dot_nt(q, kst) for q, kst in zip(qs, ksts)]
        p2s = [jnp.concatenate([(sc * dm_ref[0]).astype(BF16), (sc * dm_ref[1]).astype(BF16)], axis=0)
               for sc in scs]
        q2s = [jnp.concatenate([(q.astype(F32) * tab_ref[2]).astype(BF16),
                                (q.astype(F32) * tab_ref[3]).astype(BF16)], axis=1) for q in qs]
        o2s = [_dot(p2, vst) for p2, vst in zip(p2s, vsts)]
        ocs = [_dot(q2, sp_ref[c]) for q2, c in zip(q2s, cs)]
        o = jnp.concatenate([o2[0:c_len] + o2[c_len:] + oc for o2, oc in zip(o2s, ocs)], axis=0)
        mu = _dot(o.astype(BF16), gmat_ref[...])
        d = o - mu
        var = _dot((d * d).astype(BF16), gmat_ref[...])
        hn = d * lax.rsqrt(var + LN_EPS)
        grows = pl.ds(pl.multiple_of(gi * (grp * c_len), grp * c_len), grp * c_len)
        g = rg_ref[grows, :].astype(F32)
        o_ref[grows, :] = (jax.nn.silu(g) * hn).astype(BF16)
        return carry

    lax.fori_loop(0, nc // grp, out_body, 0)


def _retention(lg, rq, rk, rv, rg, s0f, s0b, gmat):
    b, ls, _ = rq.shape
    nc = ls // RET_CHUNK
    seq = lambda w: pl.BlockSpec((None, ls, w), lambda bb: (bb, 0, 0))
    st = pl.BlockSpec((None, RET_KW, RET_WIDTH), lambda bb: (bb, 0, 0))
    return pl.pallas_call(
        functools.partial(_ret_kernel, nc=nc),
        grid=(b,),
        in_specs=[pl.BlockSpec(memory_space=pltpu.SMEM), seq(RET_KW), seq(RET_KW), seq(RET_WIDTH), seq(RET_WIDTH),
                  st, st, pl.BlockSpec((RET_WIDTH, RET_WIDTH), lambda bb: (0, 0))],
        out_specs=[seq(RET_WIDTH), st, st],
        out_shape=[jax.ShapeDtypeStruct((b, ls, RET_WIDTH), BF16),
                   jax.ShapeDtypeStruct((b, RET_KW, RET_WIDTH), F32),
                   jax.ShapeDtypeStruct((b, RET_KW, RET_WIDTH), F32)],
        scratch_shapes=[
            pltpu.VMEM((nc, RET_KW, RET_WIDTH), F32),
            pltpu.VMEM((nc, RET_KW, RET_WIDTH), F32),
            pltpu.VMEM((nc, 2 * RET_KW, RET_WIDTH), BF16),
            pltpu.VMEM((2, RET_CHUNK, RET_HEADS * RET_CHUNK), F32),
            pltpu.VMEM((4, RET_CHUNK, RET_KW), F32),
        ],
        compiler_params=_cparams(("arbitrary",)),
        name="retention",
    )(lg, rq, rk, rv, rg, s0f, s0b, gmat)


def _mixer_out_ln1(attn_ref, pool_ref, ret_ref, x_ref, m, w_ref, g_ref, b_ref, alpha):
    y = (_dot(attn_ref[...], w_ref[0:ATTN_WIDTH, :])
         + _dot(pool_ref[...], w_ref[ATTN_WIDTH:ATTN_WIDTH + POOL_WIDTH, :])
         + _dot(ret_ref[...], w_ref[ATTN_WIDTH + POOL_WIDTH:, :]))
    x1 = _layer_norm(alpha * x_ref[...] + m[2:3, :] * y, g_ref[...], b_ref[...])
    return x1, x1 * (1.0 + m[4:5, :]) + m[3:4, :]


def _outproj_kernel(attn_ref, pool_ref, ret_ref, x_ref, mod_ref, w_ref, g_ref, b_ref, router_ref,
                    x1_ref, t_ref, lg_ref, *, alpha):
    x1, t = _mixer_out_ln1(attn_ref, pool_ref, ret_ref, x_ref, mod_ref[...], w_ref, g_ref, b_ref, alpha)
    x1_ref[...] = x1
    _store_tile_rows(t_ref, (), t)
    lg_ref[...] = _dot(t.astype(BF16), router_ref[...])


def _mixer_row_specs(tm, d):
    row = lambda w: pl.BlockSpec((None, tm, w), lambda bb, j: (bb, j, 0))
    return [row(ATTN_WIDTH), row(POOL_WIDTH), row(RET_WIDTH), row(d)]


def _outproj_ln_route(attn, pool, ret, x, modtab, mod_row_fn, w_out, ln_g, ln_b, alpha, router):
    b, ls, d = x.shape
    tm = min(512, ls)
    nt = ls // tm
    row = lambda w: pl.BlockSpec((None, tm, w), lambda bb, j: (bb, j, 0))
    const = lambda shape: pl.BlockSpec(shape, lambda bb, j: tuple(0 for _ in shape))
    return pl.pallas_call(
        functools.partial(_outproj_kernel, alpha=alpha),
        grid=(b, nt),
        in_specs=_mixer_row_specs(tm, d) + [
            pl.BlockSpec((None, 6, d), lambda bb, j: (mod_row_fn(bb), 0, 0)),
            const((d, d)), const((1, d)), const((1, d)), const((d, LANES))],
        out_specs=[row(d), pl.BlockSpec((None, tm * SUBLANES, LANES), lambda bb, j: (bb, j, 0)), row(LANES)],
        out_shape=[jax.ShapeDtypeStruct((b, ls, d), F32),
                   jax.ShapeDtypeStruct((b, ls * SUBLANES, LANES), F32),
                   jax.ShapeDtypeStruct((b, ls, LANES), F32)],
        compiler_params=_cparams(("arbitrary", "arbitrary")),
        name="outproj_ln_route",
    )(attn, pool, ret, x, modtab, w_out, ln_g, ln_b, router)


N_FFN_CHUNKS = FFN_DIM // FFN_CHUNK
N_FFN_GAPS = 3 * N_FFN_CHUNKS


def _swiglu_rows(tb, wg_ref, wu_ref, wd_ref, between=None):
    gap = (lambda j: None) if between is None else between
    f = None
    for c in range(N_FFN_CHUNKS):
        sl = slice(c * FFN_CHUNK, (c + 1) * FFN_CHUNK)
        g = _dot(tb, wg_ref[:, sl])
        gap(3 * c)
        u = _dot(tb, wu_ref[:, sl])
        gap(3 * c + 1)
        a = (jax.nn.silu(g) * u).astype(BF16)
        part = _dot(a, wd_ref[sl, :])
        gap(3 * c + 2)
        f = part if f is None else f + part
    return f


def _dense_post_kernel(attn_ref, pool_ref, ret_ref, x_ref, mod_ref, w_ref, g1_ref, b1_ref,
                       wg_ref, wu_ref, wd_ref, g2_ref, b2_ref, o_ref, *, alpha):
    m = mod_ref[...]
    x1, t = _mixer_out_ln1(attn_ref, pool_ref, ret_ref, x_ref, m, w_ref, g1_ref, b1_ref, alpha)
    f = _swiglu_rows(t.astype(BF16), wg_ref, wu_ref, wd_ref)
    o_ref[...] = _layer_norm(alpha * x1 + m[5:6, :] * f, g2_ref[...], b2_ref[...])


def _dense_post(attn, pool, ret, x, modtab, mod_row_fn, w_out, g1, b1, wg, wu, wd, g2, b2, alpha):
    b, ls, d = x.shape
    tm = min(512, ls)
    nt = ls // tm
    const = lambda shape: pl.BlockSpec(shape, lambda bb, j: tuple(0 for _ in shape),
                                       pipeline_mode=pl.Buffered(1))
    return pl.pallas_call(
        functools.partial(_dense_post_kernel, alpha=alpha),
        grid=(b, nt),
        in_specs=_mixer_row_specs(tm, d) + [
            pl.BlockSpec((None, 6, d), lambda bb, j: (mod_row_fn(bb), 0, 0)),
            const((d, d)), const((1, d)), const((1, d)),
            const((d, FFN_DIM)), const((d, FFN_DIM)), const((FFN_DIM, d)),
            const((1, d)), const((1, d))],
        out_specs=pl.BlockSpec((None, tm, d), lambda bb, j: (bb, j, 0)),
        out_shape=jax.ShapeDtypeStruct((b, ls, d), F32),
        compiler_params=_cparams(("arbitrary", "arbitrary")),
        name="dense_post",
    )(attn, pool, ret, x, modtab, w_out, g1, b1, wg, wu, wd, g2, b2)


ROUTE_TILE = 512
MOE_TILE = 512
ROW_TILE = 256


def _route_kernel(lg_ref, tri_ref, o_ref, ot_ref, cnt_ref, carry_ref):
    @pl.when(pl.program_id(0) == 0)
    def _():
        carry_ref[...] = jnp.zeros_like(carry_ref)

    tm = lg_ref.shape[0]
    lane = lax.broadcasted_iota(jnp.int32, (tm, LANES), 1)
    l = jnp.where(lane < N_EXPERTS, lg_ref[...], -jnp.inf)
    m1 = jnp.max(l, axis=1, keepdims=True)
    i1 = jnp.min(jnp.where(l == m1, lane, LANES), axis=1, keepdims=True)
    l2 = jnp.where(lane == i1, -jnp.inf, l)
    m2 = jnp.max(l2, axis=1, keepdims=True)
    i2 = jnp.min(jnp.where(l2 == m2, lane, LANES), axis=1, keepdims=True)
    e = jnp.exp(m2 - m1)
    w1 = 1.0 / (1.0 + e)
    w2 = e / (1.0 + e)
    oh = jnp.where((lane == i1) | (lane == i2), 1.0, 0.0)
    carry = carry_ref[0:1, :]
    cum = _dot(tri_ref[...], oh.astype(BF16)) + carry
    r1 = jnp.sum(jnp.where(lane == i1, cum, 0.0), axis=1, keepdims=True)
    r2 = jnp.sum(jnp.where(lane == i2, cum, 0.0), axis=1, keepdims=True)
    new = carry + jnp.sum(oh, axis=0, keepdims=True)
    carry_ref[...] = jnp.broadcast_to(new, carry_ref.shape)
    cnt_ref[...] = jnp.broadcast_to(new, cnt_ref.shape)
    tab = jnp.where(lane == 0, i1.astype(F32),
                    jnp.where(lane == 1, i2.astype(F32),
                              jnp.where(lane == 2, r1,
                                        jnp.where(lane == 3, r2,
                                                  jnp.where(lane == 4, w1,
                                                            jnp.where(lane == 5, w2, 0.0))))))
    o_ref[...] = tab
    ot_ref[...] = tab.T[0:8, :]


def _route(logits):
    n = logits.shape[0]
    tm = min(ROUTE_TILE, n)
    tri = jnp.asarray(np.tril(np.ones((tm, tm), np.float32), -1), dtype=BF16)
    return pl.pallas_call(
        _route_kernel,
        grid=(n // tm,),
        in_specs=[pl.BlockSpec((tm, LANES), lambda i: (i, 0)), pl.BlockSpec((tm, tm), lambda i: (0, 0))],
        out_specs=[pl.BlockSpec((tm, LANES), lambda i: (i, 0)), pl.BlockSpec((8, tm), lambda i: (0, i)),
                   pl.BlockSpec((8, LANES), lambda i: (0, 0))],
        out_shape=[jax.ShapeDtypeStruct((n, LANES), F32), jax.ShapeDtypeStruct((8, n), F32),
                   jax.ShapeDtypeStruct((8, LANES), F32)],
        scratch_shapes=[pltpu.VMEM((8, LANES), F32)],
        compiler_params=_cparams(("arbitrary",)),
        name="route",
    )(logits, tri)


def _invert_kernel(p1_ref, p2_ref, lo_ref, hi_ref, dst_ref, *, n):
    for e in range(N_EXPERTS + 1):
        def fill(s, carry):
            dst_ref[s] = 2 * n + (s & (MOE_TILE - 1))
            return carry

        lax.fori_loop(lo_ref[e], hi_ref[e], fill, 0)

    def body(t, carry):
        dst_ref[p1_ref[t]] = t
        dst_ref[p2_ref[t]] = n + t
        return carry

    lax.fori_loop(0, n, body, 0, unroll=16)


def _invert(pos1, pos2, fill_lo, fill_hi, nslots):
    n = pos1.shape[0]
    smem = pl.BlockSpec(memory_space=pltpu.SMEM)
    return pl.pallas_call(
        functools.partial(_invert_kernel, n=n),
        in_specs=[smem, smem, smem, smem],
        out_specs=smem,
        out_shape=jax.ShapeDtypeStruct((nslots,), jnp.int32),
        name="invert",
    )(pos1, pos2, fill_lo, fill_hi)


def _slot_source_row(v, n):
    if n & (n - 1) == 0:
        return v & (n - 1)
    return jnp.where(v >= 2 * n, v - 2 * n, jnp.where(v >= n, v - n, v))


def _moe_kernel(te_ref, nt_ref, inv_ref, t_ref, wg_ref, wu_ref, wd_ref, y_ref, xbuf, obuf, gsem, ssem, *, n):
    del te_ref
    i = pl.program_id(0)
    nt = nt_ref[0]
    tm = MOE_TILE
    slot = i & 1
    other = 1 - slot

    def row_tile(r):
        return pl.ds(pl.multiple_of(r * SUBLANES, SUBLANES), SUBLANES)

    def gather_row(tile, r, buf):
        src = _slot_source_row(inv_ref[tile * tm + r], n)
        pltpu.make_async_copy(t_ref.at[row_tile(src)], xbuf.at[buf, row_tile(r)], gsem.at[buf]).start()

    def scatter_row(tile, r, buf, real):
        dst = jnp.where(real, inv_ref[tile * tm + r], 2 * n + r)
        pltpu.make_async_copy(obuf.at[buf, row_tile(r)], y_ref.at[row_tile(dst)], ssem.at[buf]).start()

    def wait_gather(buf):
        pltpu.make_async_copy(t_ref.at[pl.ds(0, tm * SUBLANES)], xbuf.at[buf], gsem.at[buf]).wait()

    def wait_scatter(buf):
        pltpu.make_async_copy(obuf.at[buf], y_ref.at[pl.ds(0, tm * SUBLANES)], ssem.at[buf]).wait()

    @pl.when(i == 0)
    def _():
        obuf[1] = jnp.zeros(obuf.shape[1:], obuf.dtype)

        def first(r, carry):
            gather_row(0, r, 0)
            return carry

        lax.fori_loop(0, tm, first, 0, unroll=8)

    def main(cur, oth):
        wait_gather(cur)
        nxt = jnp.minimum(i + 1, nt - 1)
        prv = jnp.maximum(i - 1, 0)
        real = i >= 1
        n_gather_gaps = N_FFN_GAPS // 2
        per_g = -(-tm // n_gather_gaps)
        per_s = -(-tm // (N_FFN_GAPS - n_gather_gaps))

        def between(j):
            if j < n_gather_gaps:
                for r in range(j * per_g, min((j + 1) * per_g, tm)):
                    gather_row(nxt, r, oth)
            else:
                k = j - n_gather_gaps
                for r in range(k * per_s, min((k + 1) * per_s, tm)):
                    scatter_row(prv, r, oth, real)

        x = _load_tile_rows(xbuf, (cur,), tm).astype(BF16)
        f = _swiglu_rows(x, wg_ref, wu_ref, wd_ref, between)

        @pl.when(i >= 1)
        def _():
            wait_scatter(cur)

        _store_tile_rows(obuf, (cur,), f)

    @pl.when((i < nt) & (slot == 0))
    def _():
        main(0, 1)

    @pl.when((i < nt) & (slot == 1))
    def _():
        main(1, 0)

    @pl.when(i == nt)
    def _():
        last = nt - 1
        buf = last & 1
        wait_scatter(nt & 1)

        def tail(r, carry):
            scatter_row(last, r, buf, True)
            return carry

        lax.fori_loop(0, tm, tail, 0, unroll=8)
        wait_scatter(buf)
        wait_gather(nt & 1)


def _moe_ffn(tile_expert, ntiles, inv, t, wg, wu, wd):
    n = t.shape[0] // SUBLANES
    d = D_MODEL
    tm = MOE_TILE
    assert n >= tm
    nt_max = inv.shape[0] // tm
    wspec = lambda shape: pl.BlockSpec((None,) + shape, lambda i, te, nt, iv: (te[i], 0, 0))
    return pl.pallas_call(
        functools.partial(_moe_kernel, n=n),
        grid_spec=pltpu.PrefetchScalarGridSpec(
            num_scalar_prefetch=3,
            grid=(nt_max,),
            in_specs=[pl.BlockSpec(memory_space=pl.ANY),
                      wspec((d, FFN_DIM)), wspec((d, FFN_DIM)), wspec((FFN_DIM, d))],
            out_specs=pl.BlockSpec(memory_space=pl.ANY),
            scratch_shapes=[pltpu.VMEM((2, tm * SUBLANES, LANES), F32), pltpu.VMEM((2, tm * SUBLANES, LANES), F32),
                            pltpu.SemaphoreType.DMA((2,)), pltpu.SemaphoreType.DMA((2,))],
        ),
        out_shape=jax.ShapeDtypeStruct(((2 * n + tm) * SUBLANES, LANES), F32),
        compiler_params=_cparams(("arbitrary",)),
        name="moe_ffn",
    )(tile_expert, ntiles, inv, t, wg, wu, wd)


def _combine_kernel(y1_ref, y2_ref, x1_ref, rt_ref, mod_ref, g_ref, b_ref, o_ref, *, alpha):
    rt = rt_ref[...]
    tm = x1_ref.shape[0]
    f = rt[:, 4:5] * _load_tile_rows(y1_ref, (), tm) + rt[:, 5:6] * _load_tile_rows(y2_ref, (), tm)
    m = mod_ref[...]
    o_ref[...] = _layer_norm(alpha * x1_ref[...] + m[5:6, :] * f, g_ref[...], b_ref[...])


def _combine_ln(y, x1, rtab, modtab, l, ln_g, ln_b, alpha):
    n, d = x1.shape
    tm = min(ROW_TILE, n)
    per_b = l // tm
    nblk = n // tm
    const = lambda shape: pl.BlockSpec(shape, lambda i: tuple(0 for _ in shape))
    return pl.pallas_call(
        functools.partial(_combine_kernel, alpha=alpha),
        grid=(nblk,),
        in_specs=[pl.BlockSpec((tm * SUBLANES, LANES), lambda i: (i, 0)),
                  pl.BlockSpec((tm * SUBLANES, LANES), lambda i: (nblk + i, 0)),
                  pl.BlockSpec((tm, d), lambda i: (i, 0)),
                  pl.BlockSpec((tm, LANES), lambda i: (i, 0)),
                  pl.BlockSpec((None, 6, d), lambda i: (i // per_b, 0, 0)),
                  const((1, d)), const((1, d))],
        out_specs=pl.BlockSpec((tm, d), lambda i: (i, 0)),
        out_shape=jax.ShapeDtypeStruct((n, d), F32),
        compiler_params=_cparams(("arbitrary",)),
        name="combine_ln",
    )(y, y, x1, rtab, modtab, ln_g, ln_b)


def _moe_layer(t, logits, x1, modtab, l, wg, wu, wd, ln_g, ln_b, alpha):
    n = x1.shape[0]
    rtab, rtab_t, cnt = _route(logits)
    counts = cnt[0, :N_EXPERTS].astype(jnp.int32)
    tiles_e = (counts + MOE_TILE - 1) // MOE_TILE
    tile_end = jnp.cumsum(tiles_e)
    tile_start = tile_end - tiles_e
    ntiles = tile_end[-1:]
    nt_max = (2 * n) // MOE_TILE + N_EXPERTS + 1
    tile_expert = jnp.minimum(
        jnp.sum((jnp.arange(nt_max)[:, None] >= tile_end[None, :]).astype(jnp.int32), axis=1),
        N_EXPERTS - 1).astype(jnp.int32)
    e1 = rtab_t[0].astype(jnp.int32)
    e2 = rtab_t[1].astype(jnp.int32)
    row_start = tile_start * MOE_TILE
    pos1 = row_start[e1] + rtab_t[2].astype(jnp.int32)
    pos2 = row_start[e2] + rtab_t[3].astype(jnp.int32)
    fill_lo = jnp.concatenate([row_start + counts, tile_end[-1:] * MOE_TILE]).astype(jnp.int32)
    fill_hi = jnp.concatenate([tile_end * MOE_TILE, jnp.full((1,), nt_max * MOE_TILE)]).astype(jnp.int32)
    inv = _invert(pos1, pos2, fill_lo, fill_hi, nt_max * MOE_TILE)
    y = _moe_ffn(tile_expert, ntiles.astype(jnp.int32), inv, t, wg, wu, wd)
    return _combine_ln(y, x1, rtab, modtab, l, ln_g, ln_b, alpha)


def _prep_w_in(w):
    q = w[:, 0:ATTN_WIDTH] * (HEAD_DIM ** -0.5 * LOG2E)
    return jnp.concatenate([q, w[:, ATTN_WIDTH:]], axis=1).astype(BF16)


def _block_diag(mats):
    n = len(mats)
    rows = []
    for i, m in enumerate(mats):
        rows.append(jnp.concatenate([m if j == i else jnp.zeros_like(m) for j in range(n)], axis=1))
    return jnp.concatenate(rows, axis=0)


def kernel(x, c, ctx, c_ctx, w_mod, b_mod, w_in, attn_sink, pool_w, pool_scale, ret_log_decay_fwd,
           ret_log_decay_bwd, w_out, ln1_g, ln1_b, ln2_g, ln2_b, ffn_w_gate, ffn_w_up, ffn_w_down,
           moe_router, moe_w_gate, moe_w_up, moe_w_down):
    b, l, d = x.shape
    lc = ctx.shape[1]
    depth = w_in.shape[0]
    alpha = (2.0 * depth) ** 0.25
    assert d == D_MODEL and l % 256 == 0 and lc % 256 == 0 and l % GRID_W == 0

    n_rows = ((b + 1 + 7) // 8) * 8
    c_all = jnp.concatenate([c, c_ctx[None, :], jnp.zeros((n_rows - b - 1, d), F32)], axis=0)
    mod_all = _modulation(c_all, w_mod, b_mod).reshape(depth, n_rows, 6, d)
    lat_row = lambda bb: bb
    ctx_row = lambda bb: b

    rope_tabs = _rope_tables(l)
    band = _pool_band_matrices()
    gmat = jnp.asarray(np.kron(np.eye(RET_HEADS), np.ones((RET_DV, RET_DV))) / RET_DV, dtype=BF16)
    zero_state = jnp.zeros((b, RET_KW, RET_WIDTH), F32)

    xc = ctx
    for i in range(depth):
        last = i == depth - 1
        modtab = mod_all[i]
        w2 = _prep_w_in(w_in[i])
        wo = w_out[i].astype(BF16)
        sink = attn_sink[i].astype(F32)
        pw = _block_diag([pool_w[i, g] for g in range(len(POOL_WINDOWS))]).astype(BF16)
        ps = pool_scale[i].reshape(1, POOL_WIDTH).astype(F32)
        lg = jnp.stack([ret_log_decay_fwd[i], ret_log_decay_bwd[i]]).astype(F32)
        g1, b1 = ln1_g[i].reshape(1, d), ln1_b[i].reshape(1, d)
        g2, b2 = ln2_g[i].reshape(1, d), ln2_b[i].reshape(1, d)

        qc, kdc, vdc, uc, rqc, rkc, rvc, rgc = _inproj(xc, modtab, ctx_row, w2, None)
        ret_c, s_f, s_b = _retention(lg, rqc, rkc, rvc, rgc, zero_state, zero_state, gmat)

        q, kd, vd, u, rq, rk, rv, rg = _inproj(x, modtab, lat_row, w2, rope_tabs)
        attn = _attention_latent(sink, q, kd, vd, kdc, vdc)
        pool = _pool(u, band, pw, ps)
        ret, _, _ = _retention(lg, rq, rk, rv, rg, s_f, s_b, gmat)

        j = i // 2
        if i % 2 == 0:
            wg, wu, wd = ffn_w_gate[j].astype(BF16), ffn_w_up[j].astype(BF16), ffn_w_down[j].astype(BF16)
            x = _dense_post(attn, pool, ret, x, modtab, lat_row, wo, g1, b1, wg, wu, wd, g2, b2, alpha)
        else:
            wg, wu, wd = moe_w_gate[j].astype(BF16), moe_w_up[j].astype(BF16), moe_w_down[j].astype(BF16)
            router = jnp.pad(moe_router[j], ((0, 0), (0, LANES - N_EXPERTS))).astype(BF16)
            x1, t, logits = _outproj_ln_route(attn, pool, ret, x, modtab, lat_row, wo, g1, b1, alpha, router)
            x = _moe_layer(t.reshape(b * l * SUBLANES, LANES), logits.reshape(b * l, LANES), x1.reshape(b * l, d),
                           modtab, l, wg, wu, wd, g2, b2, alpha).reshape(b, l, d)

        if not last:
            attn_c = _attention_ctx(sink, qc, kdc, vdc)
            pool_c = _pool(uc, band, pw, ps)
            if i % 2 == 0:
                xc = _dense_post(attn_c, pool_c, ret_c, xc, modtab, ctx_row, wo, g1, b1, wg, wu, wd, g2, b2,
                                 alpha)
            else:
                x1c, tc, logits_c = _outproj_ln_route(attn_c, pool_c, ret_c, xc, modtab, ctx_row, wo, g1, b1,
                                                      alpha, router)
                ctx_mod = jnp.broadcast_to(modtab[b:b + 1], (b, 6, d))
                xc = _moe_layer(tc.reshape(b * lc * SUBLANES, LANES), logits_c.reshape(b * lc, LANES),
                                x1c.reshape(b * lc, d),
                                ctx_mod, lc, wg, wu, wd, g2, b2, alpha).reshape(b, lc, d)
    return x
```

```python
import functools
import math

import numpy as np
import jax
import jax.numpy as jnp
from jax import lax
from jax.experimental import pallas as pl
from jax.experimental.pallas import tpu as pltpu

F32 = jnp.float32
BF16 = jnp.bfloat16

D_MODEL = 1024
GRID_W = 64
HEAD_DIM = 64
N_HEADS = 8
N_KV_HEADS = 2
ATTN_WIDTH = N_HEADS * HEAD_DIM
ATTN_BLOCK = 128
ATTN_BLOCKS_PER_STEP = 4
ROPE_BASE = 10000.0
POOL_WINDOWS = (2, 4, 8, 16)
POOL_WIDTH = 256
POOL_GROUP = 64
POOL_TILE = 256
POOL_HALO = 16
RET_HEADS = 4
RET_DK = 32
RET_DV = 64
RET_WIDTH = 256
RET_KW = RET_HEADS * RET_DK
RET_CHUNK = 128
FFN_DIM = 2816
FFN_CHUNK = 256
N_EXPERTS = 8
ROUTE_TILE = 512
MOE_TILE = 512
ROW_TILE = 256
LN_EPS = 1e-5
NEG_INF = -1e30
LOG2E = math.log2(math.e)
LANES = 128
IN_WIDTH = 1792

VMEM_LIMIT = 56 * 1024 * 1024


def _cparams(sem):
    return pltpu.CompilerParams(dimension_semantics=sem, vmem_limit_bytes=VMEM_LIMIT)


def _dot(a, b):
    return jnp.dot(a, b, preferred_element_type=F32)


def _dot_nt(a, b):
    return lax.dot_general(a, b, (((1,), (1,)), ((), ())), preferred_element_type=F32)


def _dot_tn(a, b):
    return lax.dot_general(a, b, (((0,), (0,)), ((), ())), preferred_element_type=F32)


def _split_bf16(a):
    hi = a.astype(BF16)
    lo = (a - hi.astype(F32)).astype(BF16)
    return hi, lo


def _dot3(a, b):
    ah, al = _split_bf16(a)
    bh, bl = _split_bf16(b)
    return _dot(ah, bh) + _dot(ah, bl) + _dot(al, bh)


SUBLANES = 8
ROW_TILES = D_MODEL // LANES


def _store_tile_rows(ref, idx, val):
    rows = val.shape[0]
    for c in range(ROW_TILES):
        ref[idx + (pl.ds(c, rows, stride=SUBLANES), slice(None))] = val[:, c * LANES:(c + 1) * LANES]


def _load_tile_rows(ref, idx, rows):
    return jnp.concatenate([ref[idx + (pl.ds(c, rows, stride=SUBLANES), slice(None))]
                            for c in range(ROW_TILES)], axis=1)


def _layer_norm(z, g, b):
    mu = jnp.mean(z, axis=-1, keepdims=True)
    d = z - mu
    var = jnp.mean(d * d, axis=-1, keepdims=True)
    return d * lax.rsqrt(var + LN_EPS) * g + b


def _mod_kernel(c_ref, w_ref, b_ref, o_ref):
    s = jax.nn.silu(c_ref[...])
    o_ref[...] = _dot3(s, w_ref[...]) + b_ref[...]


def _modulation(c_all, w_mod, b_mod):
    depth, d, n6 = w_mod.shape
    r = c_all.shape[0]
    bn = 1536
    return pl.pallas_call(
        _mod_kernel,
        grid=(depth, n6 // bn),
        in_specs=[
            pl.BlockSpec((r, d), lambda i, j: (0, 0)),
            pl.BlockSpec((None, d, bn), lambda i, j: (i, 0, j)),
            pl.BlockSpec((None, 1, bn), lambda i, j: (i, 0, j)),
        ],
        out_specs=pl.BlockSpec((None, r, bn), lambda i, j: (i, 0, j)),
        out_shape=jax.ShapeDtypeStruct((depth, r, n6), F32),
        compiler_params=_cparams(("arbitrary", "arbitrary")),
        name="modulation",
    )(c_all, w_mod, b_mod.reshape(depth, 1, n6))


KV_WIDTH = N_KV_HEADS * HEAD_DIM
_IN_OUT_WIDTHS = (ATTN_WIDTH, KV_WIDTH, KV_WIDTH, POOL_WIDTH, RET_KW, RET_KW, RET_WIDTH, RET_WIDTH)
_IN_COL_STARTS = tuple(int(c) for c in np.cumsum((0,) + _IN_OUT_WIDTHS))


def _rope_groups(a, cos, s_prev, s_next):
    outs = []
    for g in range(a.shape[1] // LANES):
        ag = a[:, g * LANES:(g + 1) * LANES]
        outs.append(ag * cos + pltpu.roll(ag, 16, 1) * s_prev + pltpu.roll(ag, LANES - 16, 1) * s_next)
    return outs[0] if len(outs) == 1 else jnp.concatenate(outs, axis=1)


def _inproj_kernel(*refs, rope):
    if rope:
        x_ref, mod_ref, w_ref, cos_ref, sp_ref, sn_ref = refs[:6]
        outs = refs[6:]
    else:
        x_ref, mod_ref, w_ref = refs[:3]
        outs = refs[3:]
    q_ref, k_ref, v_ref, u_ref, rq_ref, rk_ref, rv_ref, rg_ref = outs
    m = mod_ref[...]
    h = (x_ref[...] * (1.0 + m[1:2, :]) + m[0:1, :]).astype(BF16)

    def mm(j0, j1):
        return _dot(h, w_ref[:, _IN_COL_STARTS[j0]:_IN_COL_STARTS[j1]])

    q = mm(0, 1)
    kv = mm(1, 3)
    k, v = kv[:, :KV_WIDTH], kv[:, KV_WIDTH:]
    if rope:
        cos, sp, sn = cos_ref[...], sp_ref[...], sn_ref[...]
        q = _rope_groups(q, cos, sp, sn)
        k = _rope_groups(k, cos, sp, sn)
    q_ref[...] = q.astype(BF16)
    k_ref[...] = k.astype(BF16)
    v_ref[...] = v.astype(BF16)
    u_ref[...] = mm(3, 4).astype(BF16)
    rqk = mm(4, 6)
    rq_ref[...] = rqk[:, :RET_KW].astype(BF16)
    rk_ref[...] = (rqk[:, RET_KW:] * (RET_DK ** -0.5)).astype(BF16)
    rv_ref[...] = mm(6, 7).astype(BF16)
    rg_ref[...] = mm(7, 8).astype(BF16)


def _inproj(x, modtab, mod_row_fn, w2, rope_tabs):
    b, ls, d = x.shape
    tm = min(1024, ls)
    nt = ls // tm
    rope = rope_tabs is not None
    in_specs = [
        pl.BlockSpec((None, tm, d), lambda j, bb: (bb, j, 0)),
        pl.BlockSpec((None, 6, d), lambda j, bb: (mod_row_fn(bb), 0, 0)),
        pl.BlockSpec((d, IN_WIDTH), lambda j, bb: (0, 0)),
    ]
    args = [x, modtab, w2]
    if rope:
        in_specs += [pl.BlockSpec((tm, LANES), lambda j, bb: (j, 0))] * 3
        args += list(rope_tabs)
    out_specs = [pl.BlockSpec((None, tm, w), lambda j, bb: (bb, j, 0)) for w in _IN_OUT_WIDTHS]
    out_shape = [jax.ShapeDtypeStruct((b, ls, w), BF16) for w in _IN_OUT_WIDTHS]
    return pl.pallas_call(
        functools.partial(_inproj_kernel, rope=rope),
        grid=(nt, b),
        in_specs=in_specs,
        out_specs=out_specs,
        out_shape=out_shape,
        compiler_params=_cparams(("arbitrary", "arbitrary")),
        name="inproj_rope" if rope else "inproj",
    )(*args)


def _rope_tables(l):
    p = jnp.arange(l)
    row = (p // GRID_W).astype(F32)
    col = (p % GRID_W).astype(F32)
    quarter = HEAD_DIM // 4
    inv = ROPE_BASE ** (-jnp.arange(quarter, dtype=F32) / quarter)
    lane = np.arange(LANES)
    j = lane % HEAD_DIM
    use_row = jnp.asarray(j < HEAD_DIM // 2)
    freq = inv[jnp.asarray(j % quarter)]
    pos = jnp.where(use_row[None, :], row[:, None], col[:, None])
    ang = pos * freq[None, :]
    cos, sin = jnp.cos(ang), jnp.sin(ang)
    second = jnp.asarray((lane % (2 * quarter)) >= quarter)[None, :]
    s_prev = jnp.where(second, sin, 0.0)
    s_next = jnp.where(second, 0.0, -sin)
    return cos, s_prev, s_next


def _attn_blocks(sink_ref, q_ref, o_ref, blocks):
    nq = ATTN_BLOCK
    lane = lax.broadcasted_iota(jnp.int32, (1, LANES), 1)
    rid = lax.broadcasted_iota(jnp.int32, (4, 1, 1), 0)
    chains = [(row0, kvh, keys[kvh], ok) for row0, keys, ok in blocks for kvh in range(N_KV_HEADS)]

    scores = []
    for row0, kvh, (k_lo, k_hi, _), _ in chains:
        rows = pl.ds(row0, nq)
        qg = jnp.concatenate([q_ref[rows, 256 * kvh:256 * kvh + LANES],
                              q_ref[rows, 256 * kvh + LANES:256 * kvh + 2 * LANES]], axis=0)
        scores.append(jnp.concatenate([_dot_nt(qg, k_lo), _dot_nt(qg, k_hi)], axis=0).reshape(4, nq, -1))

    probs, sink_terms = [], []
    for (row0, kvh, _, ok), s in zip(chains, scores):
        if ok is not None:
            ok_left, ok_right = ok
            s = jnp.concatenate([jnp.where(ok_left[None], s[:, :, 0:nq], NEG_INF), s[:, :, nq:2 * nq],
                                 jnp.where(ok_right[None], s[:, :, 2 * nq:3 * nq], NEG_INF), s[:, :, 3 * nq:]],
                                axis=2)
        h0 = 4 * kvh
        sk = jnp.where(rid == 0, sink_ref[h0],
                       jnp.where(rid == 1, sink_ref[h0 + 2],
                                 jnp.where(rid == 2, sink_ref[h0 + 1], sink_ref[h0 + 3]))) * LOG2E
        m = jnp.maximum(jnp.max(s, axis=2, keepdims=True), sk)
        probs.append(jnp.exp2((s - m).astype(BF16)).reshape(4 * nq, -1))
        sink_terms.append(jnp.exp2(sk - m).reshape(4 * nq, 1))

    outs = [_dot(e, v_ext) for e, (_, _, (_, _, v_ext), _) in zip(probs, chains)]

    for (row0, kvh, _, _), o, es in zip(chains, outs, sink_terms):
        rows = pl.ds(row0, nq)
        swapped = pltpu.roll(o, HEAD_DIM, 1)
        even = o[0:2 * nq] / (swapped[0:2 * nq] + es[0:2 * nq])
        odd = swapped[2 * nq:] / (o[2 * nq:] + es[2 * nq:])
        g0 = jnp.where(lane < HEAD_DIM, even[0:nq], odd[0:nq])
        g1 = jnp.where(lane < HEAD_DIM, even[nq:], odd[nq:])
        o_ref[rows, 256 * kvh:256 * kvh + LANES] = g0.astype(BF16)
        o_ref[rows, 256 * kvh + LANES:256 * kvh + 2 * LANES] = g1.astype(BF16)


def _attn_key_parts(k, v):
    lane = lax.broadcasted_iota(jnp.int32, (1, LANES), 1)
    low = lane < HEAD_DIM
    swap = lambda a: jnp.concatenate([a[:, HEAD_DIM:], a[:, :HEAD_DIM]], axis=1)
    ks, vs = swap(k), swap(v)
    zero, one = jnp.zeros_like(k), jnp.ones_like(v)
    return [(jnp.where(low, k, zero), jnp.where(low, zero, ks), jnp.where(low, v, one)),
            (jnp.where(low, ks, zero), jnp.where(low, zero, k), jnp.where(low, vs, one))]


def _attn_lat_kernel(sink_ref, q_ref, kl_ref, kc_ref, kr_ref, kx_ref,
                     vl_ref, vc_ref, vr_ref, vx_ref, o_ref, *, nsteps, nblk):
    i = pl.program_id(1)
    nq = ATTN_BLOCK
    rows = lax.broadcasted_iota(jnp.int32, (nq, nq), 0)
    cols = lax.broadcasted_iota(jnp.int32, (nq, nq), 1)
    upper = cols >= rows
    lower = cols <= rows
    parts = [_attn_key_parts(kr[...], vr[...])
             for kr, vr in ((kl_ref, vl_ref), (kc_ref, vc_ref), (kr_ref, vr_ref), (kx_ref, vx_ref))]
    per_head = []
    for kvh in range(N_KV_HEADS):
        left, cen, right, ctx = [p[kvh] for p in parts]
        own = [tuple(t[j * nq:(j + 1) * nq] for t in cen) for j in range(nblk)]
        per_head.append(([left] + own + [right], ctx))
    blocks = []
    for j in range(nblk):
        keys = [tuple(jnp.concatenate([p[t] for p in seq[j:j + 3]] + [ctx[t]], axis=0) for t in range(3))
                for seq, ctx in per_head]
        ok_left = upper & (i >= 1) if j == 0 else upper
        ok_right = lower & (i <= nsteps - 2) if j == nblk - 1 else lower
        blocks.append((j * nq, keys, (ok_left, ok_right)))
    _attn_blocks(sink_ref, q_ref, o_ref, blocks)


def _attn_ctx_kernel(sink_ref, q_ref, kx_ref, vx_ref, o_ref):
    _attn_blocks(sink_ref, q_ref, o_ref, [(0, _attn_key_parts(kx_ref[...], vx_ref[...]), None)])


def _attention_latent(sink, q, kd, vd, kdc, vdc):
    b, l, _ = q.shape
    lc = kdc.shape[1]
    nq = ATTN_BLOCK
    nb = l // nq
    nblk = ATTN_BLOCKS_PER_STEP
    assert nb % nblk == 0
    nsteps = nb // nblk
    smem = pl.BlockSpec(memory_space=pltpu.SMEM)
    edge = lambda f: pl.BlockSpec((None, nq, KV_WIDTH), f)
    left = lambda bb, i: (bb, jnp.maximum(nblk * i - 1, 0), 0)
    cen = lambda bb, i: (bb, i, 0)
    right = lambda bb, i: (bb, jnp.minimum(nblk * i + nblk, nb - 1), 0)
    mid = pl.BlockSpec((None, nblk * nq, KV_WIDTH), cen)
    ctx = pl.BlockSpec((None, lc, KV_WIDTH), lambda bb, i: (bb, 0, 0))
    return pl.pallas_call(
        functools.partial(_attn_lat_kernel, nsteps=nsteps, nblk=nblk),
        grid=(b, nsteps),
        in_specs=[smem, pl.BlockSpec((None, nblk * nq, ATTN_WIDTH), cen),
                  edge(left), mid, edge(right), ctx,
                  edge(left), mid, edge(right), ctx],
        out_specs=pl.BlockSpec((None, nblk * nq, ATTN_WIDTH), cen),
        out_shape=jax.ShapeDtypeStruct((b, l, ATTN_WIDTH), BF16),
        compiler_params=_cparams(("arbitrary", "arbitrary")),
        name="attn_latent",
    )(sink, q, kd, kd, kd, kdc, vd, vd, vd, vdc)


def _attention_ctx(sink, qc, kdc, vdc):
    b, lc, _ = qc.shape
    nq = ATTN_BLOCK
    smem = pl.BlockSpec(memory_space=pltpu.SMEM)
    ctx = pl.BlockSpec((None, lc, KV_WIDTH), lambda bb, i: (bb, 0, 0))
    return pl.pallas_call(
        _attn_ctx_kernel,
        grid=(b, lc // nq),
        in_specs=[smem, pl.BlockSpec((None, nq, ATTN_WIDTH), lambda bb, i: (bb, i, 0)), ctx, ctx],
        out_specs=pl.BlockSpec((None, nq, ATTN_WIDTH), lambda bb, i: (bb, i, 0)),
        out_shape=jax.ShapeDtypeStruct((b, lc, ATTN_WIDTH), BF16),
        compiler_params=_cparams(("arbitrary", "arbitrary")),
        name="attn_ctx",
    )(sink, qc, kdc, vdc)


def _pool_band_matrices():
    r = np.arange(POOL_TILE)[:, None]
    a = np.arange(POOL_TILE + 2 * POOL_HALO)[None, :] - POOL_HALO
    mats = [((a >= r - w // 2) & (a < r + w - w // 2)) for w in POOL_WINDOWS]
    return jnp.asarray(np.concatenate(mats, axis=0).astype(np.float32), dtype=BF16)


def _pool_kernel(u_ref, a_ref, w_ref, sc_ref, o_ref, pad_ref, *, ls):
    halo = POOL_HALO
    zeros = jnp.zeros((halo, POOL_WIDTH), BF16)
    pad_ref[0:halo, :] = zeros
    pad_ref[halo + ls:2 * halo + ls, :] = zeros
    pad_ref[halo:halo + ls, :] = u_ref[...]
    lane = lax.broadcasted_iota(jnp.int32, (1, POOL_WIDTH), 1)
    grp = lane // POOL_GROUP
    wl = jnp.where(grp == 0, POOL_WINDOWS[0],
                   jnp.where(grp == 1, POOL_WINDOWS[1],
                             jnp.where(grp == 2, POOL_WINDOWS[2], POOL_WINDOWS[3])))
    half = wl // 2

    ntile = ls // POOL_TILE
    grp_tiles = min(4, ntile)
    assert ntile % grp_tiles == 0
    nw = len(POOL_WINDOWS)

    def body(gi, carry):
        t0s = [pl.multiple_of((gi * grp_tiles + j) * POOL_TILE, POOL_TILE) for j in range(grp_tiles)]
        sums = [_dot(a_ref[...], pad_ref[pl.ds(t0, POOL_TILE + 2 * halo), :]) for t0 in t0s]
        ds = []
        for t0, sm in zip(t0s, sums):
            acc = sm[0:POOL_TILE]
            for g in range(1, nw):
                acc = jnp.where(grp == g, sm[g * POOL_TILE:(g + 1) * POOL_TILE], acc)
            p = t0 + lax.broadcasted_iota(jnp.int32, (POOL_TILE, 1), 0)
            hi = jnp.minimum(p + (wl - half), ls)
            lo = jnp.maximum(p - half, 0)
            cnt = (hi - lo).astype(F32)
            ut = u_ref[pl.ds(t0, POOL_TILE), :].astype(F32)
            ds.append((acc / cnt - ut).astype(BF16))
        rows = pl.ds(t0s[0], grp_tiles * POOL_TILE)
        o_ref[rows, :] = (_dot(jnp.concatenate(ds, axis=0), w_ref[...]) * sc_ref[...]).astype(BF16)
        return carry

    lax.fori_loop(0, ntile // grp_tiles, body, 0)


def _pool(u, band, wblk, scale):
    b, ls, _ = u.shape
    return pl.pallas_call(
        functools.partial(_pool_kernel, ls=ls),
        grid=(b,),
        in_specs=[
            pl.BlockSpec((None, ls, POOL_WIDTH), lambda bb: (bb, 0, 0)),
            pl.BlockSpec(band.shape, lambda bb: (0, 0)),
            pl.BlockSpec((POOL_WIDTH, POOL_WIDTH), lambda bb: (0, 0)),
            pl.BlockSpec((1, POOL_WIDTH), lambda bb: (0, 0)),
        ],
        out_specs=pl.BlockSpec((None, ls, POOL_WIDTH), lambda bb: (bb, 0, 0)),
        out_shape=jax.ShapeDtypeStruct((b, ls, POOL_WIDTH), BF16),
        scratch_shapes=[pltpu.VMEM((ls + 2 * POOL_HALO, POOL_WIDTH), BF16)],
        compiler_params=_cparams(("arbitrary",)),
        name="pool",
    )(u, band, wblk, scale)


def _ret_kernel(lg_ref, rq_ref, rk_ref, rv_ref, rg_ref, s0f_ref, s0b_ref, gmat_ref,
                o_ref, sf_ref, sb_ref,
                kvf_ref, kvb_ref, sp_ref, dm_ref, tab_ref, *, nc):
    c_len = RET_CHUNK
    kw, vw = RET_KW, RET_WIDTH

    def per_head(idx, d):
        return jnp.where(idx == 0, lg_ref[d, 0],
                         jnp.where(idx == 1, lg_ref[d, 1],
                                   jnp.where(idx == 2, lg_ref[d, 2], lg_ref[d, 3])))

    hk = lax.broadcasted_iota(jnp.int32, (1, kw), 1) // RET_DK
    hv = lax.broadcasted_iota(jnp.int32, (1, vw), 1) // RET_DV

    @pl.when(pl.program_id(0) == 0)
    def _():
        n_col = lax.broadcasted_iota(jnp.int32, (c_len, 1), 0).astype(F32)
        lgk_f, lgk_b = per_head(hk, 0), per_head(hk, 1)
        tab_ref[0] = jnp.exp(lgk_f * (c_len - 1.0 - n_col))
        tab_ref[1] = jnp.exp(lgk_b * n_col)
        tab_ref[2] = jnp.exp(lgk_f * (n_col + 1.0))
        tab_ref[3] = jnp.exp(lgk_b * (c_len - n_col))
        hs = lax.broadcasted_iota(jnp.int32, (1, RET_HEADS * c_len), 1) // c_len
        m_idx = (lax.broadcasted_iota(jnp.int32, (c_len, RET_HEADS * c_len), 1) & (c_len - 1)).astype(F32)
        n_idx = lax.broadcasted_iota(jnp.int32, (c_len, RET_HEADS * c_len), 0).astype(F32)
        rel = n_idx - m_idx
        dm_ref[0] = jnp.where(rel >= 0, jnp.exp(per_head(hs, 0) * jnp.maximum(rel, 0.0)), 0.0)
        dm_ref[1] = jnp.where(rel <= 0, jnp.exp(per_head(hs, 1) * jnp.maximum(-rel, 0.0)), 0.0)

    bd = (lax.broadcasted_iota(jnp.int32, (kw, vw), 0) // RET_DK) == (lax.broadcasted_iota(jnp.int32, (kw, vw), 1) // RET_DV)
    hk_col = lax.broadcasted_iota(jnp.int32, (kw, 1), 0) // RET_DK
    cd_f = jnp.exp(per_head(hk_col, 0) * float(c_len))
    cd_b = jnp.exp(per_head(hk_col, 1) * float(c_len))

    def rows(c):
        return pl.ds(pl.multiple_of(c * c_len, c_len), c_len)

    grp = min(4, nc)
    assert nc % grp == 0

    def kv_body(gi, carry):
        cs = [gi * grp + j for j in range(grp)]
        ks = [rk_ref[rows(c), :].astype(F32) for c in cs]
        vs = [rv_ref[rows(c), :] for c in cs]
        kf = [(k * tab_ref[0]).astype(BF16) for k in ks]
        kb = [(k * tab_ref[1]).astype(BF16) for k in ks]
        pf = [_dot_tn(a, v) for a, v in zip(kf, vs)]
        pb = [_dot_tn(a, v) for a, v in zip(kb, vs)]
        for j, c in enumerate(cs):
            kvf_ref[c] = jnp.where(bd, pf[j], 0.0)
            kvb_ref[c] = jnp.where(bd, pb[j], 0.0)
        return carry

    lax.fori_loop(0, nc // grp, kv_body, 0)

    def scan_body(j, carry):
        s_f, s_b = carry
        cb = nc - 1 - j
        sp_ref[j, 0:kw, :] = s_f.astype(BF16)
        sp_ref[cb, kw:2 * kw, :] = s_b.astype(BF16)
        return cd_f * s_f + kvf_ref[j], cd_b * s_b + kvb_ref[cb]

    s_f, s_b = lax.fori_loop(0, nc, scan_body, (s0f_ref[...], s0b_ref[...]))
    sf_ref[...] = s_f
    sb_ref[...] = s_b

    def out_body(gi, carry):
        cs = [gi * grp + j for j in range(grp)]
        qs = [rq_ref[rows(c), :] for c in cs]
        ks = [rk_ref[rows(c), :] for c in cs]
        vs = [rv_ref[rows(c), :] for c in cs]
        zk, zv = jnp.zeros_like(ks[0]), jnp.zeros_like(vs[0])
        ksts = [jnp.concatenate([jnp.where(hk == h, k, zk) for h in range(RET_HEADS)], axis=0) for k in ks]
        vsts = [jnp.concatenate([jnp.where(hv == h, v, zv) for h in range(RET_HEADS)], axis=0) for v in vs]
        scs = [_dot_nt(q, kst) for q, kst in zip(qs, ksts)]
        p2s = [jnp.concatenate([(sc * dm_ref[0]).astype(BF16), (sc * dm_ref[1]).astype(BF16)], axis=0)
               for sc in scs]
        q2s = [jnp.concatenate([(q.astype(F32) * tab_ref[2]).astype(BF16),
                                (q.astype(F32) * tab_ref[3]).astype(BF16)], axis=1) for q in qs]
        o2s = [_dot(p2, vst) for p2, vst in zip(p2s, vsts)]
        ocs = [_dot(q2, sp_ref[c]) for q2, c in zip(q2s, cs)]
        o = jnp.concatenate([o2[0:c_len] + o2[c_len:] + oc for o2, oc in zip(o2s, ocs)], axis=0)
        mu = _dot(o.astype(BF16), gmat_ref[...])
        d = o - mu
        var = _dot((d * d).astype(BF16), gmat_ref[...])
        hn = d * lax.rsqrt(var + LN_EPS)
        grows = pl.ds(pl.multiple_of(gi * (grp * c_len), grp * c_len), grp * c_len)
        g = rg_ref[grows, :].astype(F32)
        o_ref[grows, :] = (jax.nn.silu(g) * hn).astype(BF16)
        return carry

    lax.fori_loop(0, nc // grp, out_body, 0)


def _retention(lg, rq, rk, rv, rg, s0f, s0b, gmat):
    b, ls, _ = rq.shape
    nc = ls // RET_CHUNK
    seq = lambda w: pl.BlockSpec((None, ls, w), lambda bb: (bb, 0, 0))
    st = pl.BlockSpec((None, RET_KW, RET_WIDTH), lambda bb: (bb, 0, 0))
    return pl.pallas_call(
        functools.partial(_ret_kernel, nc=nc),
        grid=(b,),
        in_specs=[pl.BlockSpec(memory_space=pltpu.SMEM), seq(RET_KW), seq(RET_KW), seq(RET_WIDTH), seq(RET_WIDTH),
                  st, st, pl.BlockSpec((RET_WIDTH, RET_WIDTH), lambda bb: (0, 0))],
        out_specs=[seq(RET_WIDTH), st, st],
        out_shape=[jax.ShapeDtypeStruct((b, ls, RET_WIDTH), BF16),
                   jax.ShapeDtypeStruct((b, RET_KW, RET_WIDTH), F32),
                   jax.ShapeDtypeStruct((b, RET_KW, RET_WIDTH), F32)],
        scratch_shapes=[
            pltpu.VMEM((nc, RET_KW, RET_WIDTH), F32),
            pltpu.VMEM((nc, RET_KW, RET_WIDTH), F32),
            pltpu.VMEM((nc, 2 * RET_KW, RET_WIDTH), BF16),
            pltpu.VMEM((2, RET_CHUNK, RET_HEADS * RET_CHUNK), F32),
            pltpu.VMEM((4, RET_CHUNK, RET_KW), F32),
        ],
        compiler_params=_cparams(("arbitrary",)),
        name="retention",
    )(lg, rq, rk, rv, rg, s0f, s0b, gmat)


def _mixer_out_ln1(attn_ref, pool_ref, ret_ref, x_ref, m, w_ref, g_ref, b_ref, alpha):
    y = (_dot(attn_ref[...], w_ref[0:ATTN_WIDTH, :])
         + _dot(pool_ref[...], w_ref[ATTN_WIDTH:ATTN_WIDTH + POOL_WIDTH, :])
         + _dot(ret_ref[...], w_ref[ATTN_WIDTH + POOL_WIDTH:, :]))
    x1 = _layer_norm(alpha * x_ref[...] + m[2:3, :] * y, g_ref[...], b_ref[...])
    return x1, x1 * (1.0 + m[4:5, :]) + m[3:4, :]


def _route_tile(logits, tri_ref, carry_ref):
    tm = logits.shape[0]
    lane = lax.broadcasted_iota(jnp.int32, (tm, LANES), 1)
    l = jnp.where(lane < N_EXPERTS, logits, -jnp.inf)
    m1 = jnp.max(l, axis=1, keepdims=True)
    i1 = jnp.min(jnp.where(l == m1, lane, LANES), axis=1, keepdims=True)
    l2 = jnp.where(lane == i1, -jnp.inf, l)
    m2 = jnp.max(l2, axis=1, keepdims=True)
    i2 = jnp.min(jnp.where(l2 == m2, lane, LANES), axis=1, keepdims=True)
    e = jnp.exp(m2 - m1)
    w1 = 1.0 / (1.0 + e)
    w2 = e / (1.0 + e)
    oh = jnp.where((lane == i1) | (lane == i2), 1.0, 0.0)
    carry = carry_ref[0:1, :]
    cum = _dot(tri_ref[...], oh.astype(BF16)) + carry
    r1 = jnp.sum(jnp.where(lane == i1, cum, 0.0), axis=1, keepdims=True)
    r2 = jnp.sum(jnp.where(lane == i2, cum, 0.0), axis=1, keepdims=True)
    carry_ref[...] = jnp.broadcast_to(carry + jnp.sum(oh, axis=0, keepdims=True), carry_ref.shape)
    return jnp.where(lane == 0, i1.astype(F32),
                     jnp.where(lane == 1, i2.astype(F32),
                               jnp.where(lane == 2, r1,
                                         jnp.where(lane == 3, r2,
                                                   jnp.where(lane == 4, w1,
                                                             jnp.where(lane == 5, w2, 0.0))))))


def _outproj_kernel(attn_ref, pool_ref, ret_ref, x_ref, mod_ref, w_ref, g_ref, b_ref, router_ref, tri_ref,
                    x1_ref, t_ref, tab_ref, tabt_ref, cnt_ref, carry_ref, *, alpha):
    @pl.when((pl.program_id(0) == 0) & (pl.program_id(1) == 0))
    def _():
        carry_ref[...] = jnp.zeros_like(carry_ref)

    x1, t = _mixer_out_ln1(attn_ref, pool_ref, ret_ref, x_ref, mod_ref[...], w_ref, g_ref, b_ref, alpha)
    x1_ref[...] = x1
    _store_tile_rows(t_ref, (), t)
    tab = _route_tile(_dot(t.astype(BF16), router_ref[...]), tri_ref, carry_ref)
    tab_ref[...] = tab
    tabt_ref[...] = tab.T[0:8, :]
    cnt_ref[...] = carry_ref[...]


def _mixer_row_specs(tm, d):
    row = lambda w: pl.BlockSpec((None, tm, w), lambda bb, j: (bb, j, 0))
    return [row(ATTN_WIDTH), row(POOL_WIDTH), row(RET_WIDTH), row(d)]


def _outproj_ln_route(attn, pool, ret, x, modtab, mod_row_fn, w_out, ln_g, ln_b, alpha, router):
    b, ls, d = x.shape
    tm = min(ROUTE_TILE, ls)
    nt = ls // tm
    tri = jnp.asarray(np.tril(np.ones((tm, tm), np.float32), -1), dtype=BF16)
    row = lambda w: pl.BlockSpec((None, tm, w), lambda bb, j: (bb, j, 0))
    const = lambda shape: pl.BlockSpec(shape, lambda bb, j: tuple(0 for _ in shape))
    return pl.pallas_call(
        functools.partial(_outproj_kernel, alpha=alpha),
        grid=(b, nt),
        in_specs=_mixer_row_specs(tm, d) + [
            pl.BlockSpec((None, 6, d), lambda bb, j: (mod_row_fn(bb), 0, 0)),
            const((d, d)), const((1, d)), const((1, d)), const((d, LANES)), const((tm, tm))],
        out_specs=[row(d), pl.BlockSpec((None, tm * SUBLANES, LANES), lambda bb, j: (bb, j, 0)), row(LANES),
                   pl.BlockSpec((8, tm), lambda bb, j: (0, bb * nt + j)), const((8, LANES))],
        out_shape=[jax.ShapeDtypeStruct((b, ls, d), F32),
                   jax.ShapeDtypeStruct((b, ls * SUBLANES, LANES), F32),
                   jax.ShapeDtypeStruct((b, ls, LANES), F32),
                   jax.ShapeDtypeStruct((8, b * ls), F32),
                   jax.ShapeDtypeStruct((8, LANES), F32)],
        scratch_shapes=[pltpu.VMEM((8, LANES), F32)],
        compiler_params=_cparams(("arbitrary", "arbitrary")),
        name="outproj_ln_route",
    )(attn, pool, ret, x, modtab, w_out, ln_g, ln_b, router, tri)


N_FFN_CHUNKS = FFN_DIM // FFN_CHUNK
N_FFN_GAPS = 3 * N_FFN_CHUNKS


def _swiglu_rows(tb, wg_ref, wu_ref, wd_ref, between=None):
    gap = (lambda j: None) if between is None else between
    f = None
    for c in range(N_FFN_CHUNKS):
        sl = slice(c * FFN_CHUNK, (c + 1) * FFN_CHUNK)
        g = _dot(tb, wg_ref[:, sl])
        gap(3 * c)
        u = _dot(tb, wu_ref[:, sl])
        gap(3 * c + 1)
        a = (jax.nn.silu(g) * u).astype(BF16)
        part = _dot(a, wd_ref[sl, :])
        gap(3 * c + 2)
        f = part if f is None else f + part
    return f


def _dense_post_kernel(attn_ref, pool_ref, ret_ref, x_ref, mod_ref, w_ref, g1_ref, b1_ref,
                       wg_ref, wu_ref, wd_ref, g2_ref, b2_ref, o_ref, *, alpha):
    m = mod_ref[...]
    x1, t = _mixer_out_ln1(attn_ref, pool_ref, ret_ref, x_ref, m, w_ref, g1_ref, b1_ref, alpha)
    f = _swiglu_rows(t.astype(BF16), wg_ref, wu_ref, wd_ref)
    o_ref[...] = _layer_norm(alpha * x1 + m[5:6, :] * f, g2_ref[...], b2_ref[...])


def _dense_post(attn, pool, ret, x, modtab, mod_row_fn, w_out, g1, b1, wg, wu, wd, g2, b2, alpha):
    b, ls, d = x.shape
    tm = min(512, ls)
    nt = ls // tm
    const = lambda shape: pl.BlockSpec(shape, lambda bb, j: tuple(0 for _ in shape),
                                       pipeline_mode=pl.Buffered(1))
    return pl.pallas_call(
        functools.partial(_dense_post_kernel, alpha=alpha),
        grid=(b, nt),
        in_specs=_mixer_row_specs(tm, d) + [
            pl.BlockSpec((None, 6, d), lambda bb, j: (mod_row_fn(bb), 0, 0)),
            const((d, d)), const((1, d)), const((1, d)),
            const((d, FFN_DIM)), const((d, FFN_DIM)), const((FFN_DIM, d)),
            const((1, d)), const((1, d))],
        out_specs=pl.BlockSpec((None, tm, d), lambda bb, j: (bb, j, 0)),
        out_shape=jax.ShapeDtypeStruct((b, ls, d), F32),
        compiler_params=_cparams(("arbitrary", "arbitrary")),
        name="dense_post",
    )(attn, pool, ret, x, modtab, w_out, g1, b1, wg, wu, wd, g2, b2)


def _invert_kernel(p1_ref, p2_ref, lo_ref, hi_ref, dst_ref, *, n):
    for e in range(N_EXPERTS + 1):
        def fill(s, carry):
            dst_ref[s] = 2 * n + (s & (MOE_TILE - 1))
            return carry

        lax.fori_loop(lo_ref[e], hi_ref[e], fill, 0)

    def body(t, carry):
        dst_ref[p1_ref[t]] = t
        dst_ref[p2_ref[t]] = n + t
        return carry

    lax.fori_loop(0, n, body, 0, unroll=16)


def _invert(pos1, pos2, fill_lo, fill_hi, nslots):
    n = pos1.shape[0]
    smem = pl.BlockSpec(memory_space=pltpu.SMEM)
    return pl.pallas_call(
        functools.partial(_invert_kernel, n=n),
        in_specs=[smem, smem, smem, smem],
        out_specs=smem,
        out_shape=jax.ShapeDtypeStruct((nslots,), jnp.int32),
        name="invert",
    )(pos1, pos2, fill_lo, fill_hi)


def _slot_source_row(v, n):
    if n & (n - 1) == 0:
        return v & (n - 1)
    return jnp.where(v >= 2 * n, v - 2 * n, jnp.where(v >= n, v - n, v))


def _moe_kernel(te_ref, nt_ref, inv_ref, t_ref, wg_ref, wu_ref, wd_ref, y_ref, xbuf, obuf, gsem, ssem, *, n):
    del te_ref
    i = pl.program_id(0)
    nt = nt_ref[0]
    tm = MOE_TILE
    slot = i & 1
    other = 1 - slot

    def row_tile(r):
        return pl.ds(pl.multiple_of(r * SUBLANES, SUBLANES), SUBLANES)

    def gather_row(tile, r, buf):
        src = _slot_source_row(inv_ref[tile * tm + r], n)
        pltpu.make_async_copy(t_ref.at[row_tile(src)], xbuf.at[buf, row_tile(r)], gsem.at[buf]).start()

    def scatter_row(tile, r, buf, real):
        dst = jnp.where(real, inv_ref[tile * tm + r], 2 * n + r)
        pltpu.make_async_copy(obuf.at[buf, row_tile(r)], y_ref.at[row_tile(dst)], ssem.at[buf]).start()

    def wait_gather(buf):
        pltpu.make_async_copy(t_ref.at[pl.ds(0, tm * SUBLANES)], xbuf.at[buf], gsem.at[buf]).wait()

    def wait_scatter(buf):
        pltpu.make_async_copy(obuf.at[buf], y_ref.at[pl.ds(0, tm * SUBLANES)], ssem.at[buf]).wait()

    @pl.when(i == 0)
    def _():
        obuf[1] = jnp.zeros(obuf.shape[1:], obuf.dtype)

        def first(r, carry):
            gather_row(0, r, 0)
            return carry

        lax.fori_loop(0, tm, first, 0, unroll=8)

    def main(cur, oth):
        wait_gather(cur)
        nxt = jnp.minimum(i + 1, nt - 1)
        prv = jnp.maximum(i - 1, 0)
        real = i >= 1
        n_gather_gaps = N_FFN_GAPS // 2
        per_g = -(-tm // n_gather_gaps)
        per_s = -(-tm // (N_FFN_GAPS - n_gather_gaps))

        def between(j):
            if j < n_gather_gaps:
                for r in range(j * per_g, min((j + 1) * per_g, tm)):
                    gather_row(nxt, r, oth)
            else:
                k = j - n_gather_gaps
                for r in range(k * per_s, min((k + 1) * per_s, tm)):
                    scatter_row(prv, r, oth, real)

        x = _load_tile_rows(xbuf, (cur,), tm).astype(BF16)
        f = _swiglu_rows(x, wg_ref, wu_ref, wd_ref, between)

        @pl.when(i >= 1)
        def _():
            wait_scatter(cur)

        _store_tile_rows(obuf, (cur,), f)

    @pl.when((i < nt) & (slot == 0))
    def _():
        main(0, 1)

    @pl.when((i < nt) & (slot == 1))
    def _():
        main(1, 0)

    @pl.when(i == nt)
    def _():
        last = nt - 1
        buf = last & 1
        wait_scatter(nt & 1)

        def tail(r, carry):
            scatter_row(last, r, buf, True)
            return carry

        lax.fori_loop(0, tm, tail, 0, unroll=8)
        wait_scatter(buf)
        wait_gather(nt & 1)


def _moe_ffn(tile_expert, ntiles, inv, t, wg, wu, wd):
    n = t.shape[0] // SUBLANES
    d = D_MODEL
    tm = MOE_TILE
    assert n >= tm
    nt_max = inv.shape[0] // tm
    wspec = lambda shape: pl.BlockSpec((None,) + shape, lambda i, te, nt, iv: (te[i], 0, 0))
    return pl.pallas_call(
        functools.partial(_moe_kernel, n=n),
        grid_spec=pltpu.PrefetchScalarGridSpec(
            num_scalar_prefetch=3,
            grid=(nt_max,),
            in_specs=[pl.BlockSpec(memory_space=pl.ANY),
                      wspec((d, FFN_DIM)), wspec((d, FFN_DIM)), wspec((FFN_DIM, d))],
            out_specs=pl.BlockSpec(memory_space=pl.ANY),
            scratch_shapes=[pltpu.VMEM((2, tm * SUBLANES, LANES), F32), pltpu.VMEM((2, tm * SUBLANES, LANES), F32),
                            pltpu.SemaphoreType.DMA((2,)), pltpu.SemaphoreType.DMA((2,))],
        ),
        out_shape=jax.ShapeDtypeStruct(((2 * n + tm) * SUBLANES, LANES), F32),
        compiler_params=_cparams(("arbitrary",)),
        name="moe_ffn",
    )(tile_expert, ntiles, inv, t, wg, wu, wd)


def _combine_kernel(y1_ref, y2_ref, x1_ref, rt_ref, mod_ref, g_ref, b_ref, o_ref, *, alpha):
    rt = rt_ref[...]
    tm = x1_ref.shape[0]
    f = rt[:, 4:5] * _load_tile_rows(y1_ref, (), tm) + rt[:, 5:6] * _load_tile_rows(y2_ref, (), tm)
    m = mod_ref[...]
    o_ref[...] = _layer_norm(alpha * x1_ref[...] + m[5:6, :] * f, g_ref[...], b_ref[...])


def _combine_ln(y, x1, rtab, modtab, l, ln_g, ln_b, alpha):
    n, d = x1.shape
    tm = min(ROW_TILE, n)
    per_b = l // tm
    nblk = n // tm
    const = lambda shape: pl.BlockSpec(shape, lambda i: tuple(0 for _ in shape))
    return pl.pallas_call(
        functools.partial(_combine_kernel, alpha=alpha),
        grid=(nblk,),
        in_specs=[pl.BlockSpec((tm * SUBLANES, LANES), lambda i: (i, 0)),
                  pl.BlockSpec((tm * SUBLANES, LANES), lambda i: (nblk + i, 0)),
                  pl.BlockSpec((tm, d), lambda i: (i, 0)),
                  pl.BlockSpec((tm, LANES), lambda i: (i, 0)),
                  pl.BlockSpec((None, 6, d), lambda i: (i // per_b, 0, 0)),
                  const((1, d)), const((1, d))],
        out_specs=pl.BlockSpec((tm, d), lambda i: (i, 0)),
        out_shape=jax.ShapeDtypeStruct((n, d), F32),
        compiler_params=_cparams(("arbitrary",)),
        name="combine_ln",
    )(y, y, x1, rtab, modtab, ln_g, ln_b)


def _moe_layer(t, rtab, rtab_t, cnt, x1, modtab, l, wg, wu, wd, ln_g, ln_b, alpha):
    n = x1.shape[0]
    counts = cnt[0, :N_EXPERTS].astype(jnp.int32)
    tiles_e = (counts + MOE_TILE - 1) // MOE_TILE
    tile_end = jnp.cumsum(tiles_e)
    tile_start = tile_end - tiles_e
    ntiles = tile_end[-1:]
    nt_max = (2 * n) // MOE_TILE + N_EXPERTS + 1
    tile_expert = jnp.minimum(
        jnp.sum((jnp.arange(nt_max)[:, None] >= tile_end[None, :]).astype(jnp.int32), axis=1),
        N_EXPERTS - 1).astype(jnp.int32)
    e1 = rtab_t[0].astype(jnp.int32)
    e2 = rtab_t[1].astype(jnp.int32)
    row_start = tile_start * MOE_TILE
    pos1 = row_start[e1] + rtab_t[2].astype(jnp.int32)
    pos2 = row_start[e2] + rtab_t[3].astype(jnp.int32)
    fill_lo = jnp.concatenate([row_start + counts, tile_end[-1:] * MOE_TILE]).astype(jnp.int32)
    fill_hi = jnp.concatenate([tile_end * MOE_TILE, jnp.full((1,), nt_max * MOE_TILE)]).astype(jnp.int32)
    inv = _invert(pos1, pos2, fill_lo, fill_hi, nt_max * MOE_TILE)
    y = _moe_ffn(tile_expert, ntiles.astype(jnp.int32), inv, t, wg, wu, wd)
    return _combine_ln(y, x1, rtab, modtab, l, ln_g, ln_b, alpha)


def _prep_w_in(w):
    q = w[:, 0:ATTN_WIDTH] * (HEAD_DIM ** -0.5 * LOG2E)
    return jnp.concatenate([q, w[:, ATTN_WIDTH:]], axis=1).astype(BF16)


def _block_diag(mats):
    n = len(mats)
    rows = []
    for i, m in enumerate(mats):
        rows.append(jnp.concatenate([m if j == i else jnp.zeros_like(m) for j in range(n)], axis=1))
    return jnp.concatenate(rows, axis=0)


def kernel(x, c, ctx, c_ctx, w_mod, b_mod, w_in, attn_sink, pool_w, pool_scale, ret_log_decay_fwd,
           ret_log_decay_bwd, w_out, ln1_g, ln1_b, ln2_g, ln2_b, ffn_w_gate, ffn_w_up, ffn_w_down,
           moe_router, moe_w_gate, moe_w_up, moe_w_down):
    b, l, d = x.shape
    lc = ctx.shape[1]
    depth = w_in.shape[0]
    alpha = (2.0 * depth) ** 0.25
    assert d == D_MODEL and l % 256 == 0 and lc % 256 == 0 and l % GRID_W == 0

    n_rows = ((b + 1 + 7) // 8) * 8
    c_all = jnp.concatenate([c, c_ctx[None, :], jnp.zeros((n_rows - b - 1, d), F32)], axis=0)
    mod_all = _modulation(c_all, w_mod, b_mod).reshape(depth, n_rows, 6, d)
    lat_row = lambda bb: bb
    ctx_row = lambda bb: b

    rope_tabs = _rope_tables(l)
    band = _pool_band_matrices()
    gmat = jnp.asarray(np.kron(np.eye(RET_HEADS), np.ones((RET_DV, RET_DV))) / RET_DV, dtype=BF16)
    zero_state = jnp.zeros((b, RET_KW, RET_WIDTH), F32)

    xc = ctx
    for i in range(depth):
        last = i == depth - 1
        modtab = mod_all[i]
        w2 = _prep_w_in(w_in[i])
        wo = w_out[i].astype(BF16)
        sink = attn_sink[i].astype(F32)
        pw = _block_diag([pool_w[i, g] for g in range(len(POOL_WINDOWS))]).astype(BF16)
        ps = pool_scale[i].reshape(1, POOL_WIDTH).astype(F32)
        lg = jnp.stack([ret_log_decay_fwd[i], ret_log_decay_bwd[i]]).astype(F32)
        g1, b1 = ln1_g[i].reshape(1, d), ln1_b[i].reshape(1, d)
        g2, b2 = ln2_g[i].reshape(1, d), ln2_b[i].reshape(1, d)

        qc, kdc, vdc, uc, rqc, rkc, rvc, rgc = _inproj(xc, modtab, ctx_row, w2, None)
        ret_c, s_f, s_b = _retention(lg, rqc, rkc, rvc, rgc, zero_state, zero_state, gmat)

        q, kd, vd, u, rq, rk, rv, rg = _inproj(x, modtab, lat_row, w2, rope_tabs)
        attn = _attention_latent(sink, q, kd, vd, kdc, vdc)
        pool = _pool(u, band, pw, ps)
        ret, _, _ = _retention(lg, rq, rk, rv, rg, s_f, s_b, gmat)

        j = i // 2
        if i % 2 == 0:
            wg, wu, wd = ffn_w_gate[j].astype(BF16), ffn_w_up[j].astype(BF16), ffn_w_down[j].astype(BF16)
            x = _dense_post(attn, pool, ret, x, modtab, lat_row, wo, g1, b1, wg, wu, wd, g2, b2, alpha)
        else:
            wg, wu, wd = moe_w_gate[j].astype(BF16), moe_w_up[j].astype(BF16), moe_w_down[j].astype(BF16)
            router = jnp.pad(moe_router[j], ((0, 0), (0, LANES - N_EXPERTS))).astype(BF16)
            x1, t, rtab, rtab_t, cnt = _outproj_ln_route(attn, pool, ret, x, modtab, lat_row, wo, g1, b1, alpha,
                                                         router)
            x = _moe_layer(t.reshape(b * l * SUBLANES, LANES), rtab.reshape(b * l, LANES), rtab_t, cnt,
                           x1.reshape(b * l, d), modtab, l, wg, wu, wd, g2, b2, alpha).reshape(b, l, d)

        if not last:
            attn_c = _attention_ctx(sink, qc, kdc, vdc)
            pool_c = _pool(uc, band, pw, ps)
            if i % 2 == 0:
                xc = _dense_post(attn_c, pool_c, ret_c, xc, modtab, ctx_row, wo, g1, b1, wg, wu, wd, g2, b2,
                                 alpha)
            else:
                x1c, tc, rtab_c, rtab_tc, cnt_c = _outproj_ln_route(attn_c, pool_c, ret_c, xc, modtab, ctx_row, wo,
                                                                    g1, b1, alpha, router)
                ctx_mod = jnp.broadcast_to(modtab[b:b + 1], (b, 6, d))
                xc = _moe_layer(tc.reshape(b * lc * SUBLANES, LANES), rtab_c.reshape(b * lc, LANES), rtab_tc,
                                cnt_c, x1c.reshape(b * lc, d), ctx_mod, lc, wg, wu, wd, g2, b2,
                                alpha).reshape(b, lc, d)
    return x
```

```python
import functools
import math

import numpy as np
import jax
import jax.numpy as jnp
from jax import lax
from jax.experimental import pallas as pl
from jax.experimental.pallas import tpu as pltpu

F32 = jnp.float32
BF16 = jnp.bfloat16

D_MODEL = 1024
GRID_W = 64
HEAD_DIM = 64
N_HEADS = 8
N_KV_HEADS = 2
ATTN_WIDTH = N_HEADS * HEAD_DIM
ATTN_BLOCK = 128
ATTN_BLOCKS_PER_STEP = 8
ROPE_BASE = 10000.0
POOL_WINDOWS = (2, 4, 8, 16)
POOL_WIDTH = 256
POOL_GROUP = 64
POOL_TILE = 256
POOL_HALO = 16
RET_HEADS = 4
RET_DK = 32
RET_DV = 64
RET_WIDTH = 256
RET_KW = RET_HEADS * RET_DK
RET_CHUNK = 128
FFN_DIM = 2816
FFN_CHUNK = 256
N_EXPERTS = 8
ROUTER_ROWS = 16
ROUTE_TILE = 512
MOE_TILE = 512
ROW_TILE = 512
LN_EPS = 1e-5
NEG_INF = -1e30
LOG2E = math.log2(math.e)
LANES = 128
IN_WIDTH = 1792

VMEM_LIMIT = 56 * 1024 * 1024


def _cparams(sem):
    return pltpu.CompilerParams(dimension_semantics=sem, vmem_limit_bytes=VMEM_LIMIT)


def _dot(a, b):
    return jnp.dot(a, b, preferred_element_type=F32)


def _dot_nt(a, b):
    return lax.dot_general(a, b, (((1,), (1,)), ((), ())), preferred_element_type=F32)


def _dot_tn(a, b):
    return lax.dot_general(a, b, (((0,), (0,)), ((), ())), preferred_element_type=F32)


def _split_bf16(a):
    hi = a.astype(BF16)
    lo = (a - hi.astype(F32)).astype(BF16)
    return hi, lo


def _dot3(a, b):
    ah, al = _split_bf16(a)
    bh, bl = _split_bf16(b)
    return _dot(ah, bh) + _dot(ah, bl) + _dot(al, bh)


SUBLANES = 8
ROW_TILES = D_MODEL // LANES


def _store_tile_rows(ref, idx, val):
    rows = val.shape[0]
    for c in range(ROW_TILES):
        ref[idx + (pl.ds(c, rows, stride=SUBLANES), slice(None))] = val[:, c * LANES:(c + 1) * LANES]


def _load_tile_rows(ref, idx, rows):
    return jnp.concatenate([ref[idx + (pl.ds(c, rows, stride=SUBLANES), slice(None))]
                            for c in range(ROW_TILES)], axis=1)


def _layer_norm(z, g, b):
    mu = jnp.mean(z, axis=-1, keepdims=True)
    d = z - mu
    var = jnp.mean(d * d, axis=-1, keepdims=True)
    return d * lax.rsqrt(var + LN_EPS) * g + b


def _mod_kernel(c_ref, w_ref, b_ref, o_ref):
    s = jax.nn.silu(c_ref[...])
    o_ref[...] = _dot3(s, w_ref[...]) + b_ref[...]


def _modulation(c_all, w_mod, b_mod):
    depth, d, n6 = w_mod.shape
    r = c_all.shape[0]
    bn = 1536
    return pl.pallas_call(
        _mod_kernel,
        grid=(depth, n6 // bn),
        in_specs=[
            pl.BlockSpec((r, d), lambda i, j: (0, 0)),
            pl.BlockSpec((None, d, bn), lambda i, j: (i, 0, j)),
            pl.BlockSpec((None, 1, bn), lambda i, j: (i, 0, j)),
        ],
        out_specs=pl.BlockSpec((None, r, bn), lambda i, j: (i, 0, j)),
        out_shape=jax.ShapeDtypeStruct((depth, r, n6), F32),
        compiler_params=_cparams(("arbitrary", "arbitrary")),
        name="modulation",
    )(c_all, w_mod, b_mod.reshape(depth, 1, n6))


KV_WIDTH = N_KV_HEADS * HEAD_DIM
_IN_OUT_WIDTHS = (ATTN_WIDTH, KV_WIDTH, KV_WIDTH, POOL_WIDTH, RET_KW, RET_KW, RET_WIDTH, RET_WIDTH)
_IN_COL_STARTS = tuple(int(c) for c in np.cumsum((0,) + _IN_OUT_WIDTHS))


def _rope_groups(a, cos, s_prev, s_next):
    outs = []
    for g in range(a.shape[1] // LANES):
        ag = a[:, g * LANES:(g + 1) * LANES]
        outs.append(ag * cos + pltpu.roll(ag, 16, 1) * s_prev + pltpu.roll(ag, LANES - 16, 1) * s_next)
    return outs[0] if len(outs) == 1 else jnp.concatenate(outs, axis=1)


def _inproj_kernel(*refs, rope):
    if rope:
        x_ref, mod_ref, w_ref, cos_ref, sp_ref, sn_ref = refs[:6]
        outs = refs[6:]
    else:
        x_ref, mod_ref, w_ref = refs[:3]
        outs = refs[3:]
    q_ref, k_ref, v_ref, u_ref, rq_ref, rk_ref, rv_ref, rg_ref = outs
    m = mod_ref[...]
    h = (x_ref[...] * (1.0 + m[1:2, :]) + m[0:1, :]).astype(BF16)

    def mm(j0, j1):
        return _dot(h, w_ref[:, _IN_COL_STARTS[j0]:_IN_COL_STARTS[j1]])

    q = mm(0, 1)
    kv = mm(1, 3)
    k, v = kv[:, :KV_WIDTH], kv[:, KV_WIDTH:]
    if rope:
        cos, sp, sn = cos_ref[...], sp_ref[...], sn_ref[...]
        q = _rope_groups(q, cos, sp, sn)
        k = _rope_groups(k, cos, sp, sn)
    q_ref[...] = q.astype(BF16)
    k_ref[...] = k.astype(BF16)
    v_ref[...] = v.astype(BF16)
    u_ref[...] = mm(3, 4).astype(BF16)
    rqk = mm(4, 6)
    rq_ref[...] = rqk[:, :RET_KW].astype(BF16)
    rk_ref[...] = (rqk[:, RET_KW:] * (RET_DK ** -0.5)).astype(BF16)
    rv_ref[...] = mm(6, 7).astype(BF16)
    rg_ref[...] = mm(7, 8).astype(BF16)


def _inproj(x, modtab, mod_row_fn, w2, rope_tabs):
    b, ls, d = x.shape
    tm = min(1024, ls)
    nt = ls // tm
    rope = rope_tabs is not None
    in_specs = [
        pl.BlockSpec((None, tm, d), lambda j, bb: (bb, j, 0)),
        pl.BlockSpec((None, 6, d), lambda j, bb: (mod_row_fn(bb), 0, 0)),
        pl.BlockSpec((d, IN_WIDTH), lambda j, bb: (0, 0)),
    ]
    args = [x, modtab, w2]
    if rope:
        in_specs += [pl.BlockSpec((tm, LANES), lambda j, bb: (j, 0))] * 3
        args += list(rope_tabs)
    out_specs = [pl.BlockSpec((None, tm, w), lambda j, bb: (bb, j, 0)) for w in _IN_OUT_WIDTHS]
    out_shape = [jax.ShapeDtypeStruct((b, ls, w), BF16) for w in _IN_OUT_WIDTHS]
    return pl.pallas_call(
        functools.partial(_inproj_kernel, rope=rope),
        grid=(nt, b),
        in_specs=in_specs,
        out_specs=out_specs,
        out_shape=out_shape,
        compiler_params=_cparams(("arbitrary", "arbitrary")),
        name="inproj_rope" if rope else "inproj",
    )(*args)


def _rope_tables(l):
    p = jnp.arange(l)
    row = (p // GRID_W).astype(F32)
    col = (p % GRID_W).astype(F32)
    quarter = HEAD_DIM // 4
    inv = ROPE_BASE ** (-jnp.arange(quarter, dtype=F32) / quarter)
    lane = np.arange(LANES)
    j = lane % HEAD_DIM
    use_row = jnp.asarray(j < HEAD_DIM // 2)
    freq = inv[jnp.asarray(j % quarter)]
    pos = jnp.where(use_row[None, :], row[:, None], col[:, None])
    ang = pos * freq[None, :]
    cos, sin = jnp.cos(ang), jnp.sin(ang)
    second = jnp.asarray((lane % (2 * quarter)) >= quarter)[None, :]
    s_prev = jnp.where(second, sin, 0.0)
    s_next = jnp.where(second, 0.0, -sin)
    return cos, s_prev, s_next


def _attn_blocks(sink_ref, q_ref, o_ref, blocks):
    nq = ATTN_BLOCK
    lane = lax.broadcasted_iota(jnp.int32, (1, LANES), 1)
    rid = lax.broadcasted_iota(jnp.int32, (4, 1, 1), 0)
    chains = [(row0, kvh, keys[kvh], ok) for row0, keys, ok in blocks for kvh in range(N_KV_HEADS)]

    scores = []
    for row0, kvh, (k_lo, k_hi, _), _ in chains:
        rows = pl.ds(row0, nq)
        qg = jnp.concatenate([q_ref[rows, 256 * kvh:256 * kvh + LANES],
                              q_ref[rows, 256 * kvh + LANES:256 * kvh + 2 * LANES]], axis=0)
        scores.append(jnp.concatenate([_dot_nt(qg, k_lo), _dot_nt(qg, k_hi)], axis=0).reshape(4, nq, -1))

    probs, sink_terms = [], []
    for (row0, kvh, _, ok), s in zip(chains, scores):
        if ok is not None:
            ok_left, ok_right = ok
            s = jnp.concatenate([jnp.where(ok_left[None], s[:, :, 0:nq], NEG_INF), s[:, :, nq:2 * nq],
                                 jnp.where(ok_right[None], s[:, :, 2 * nq:3 * nq], NEG_INF), s[:, :, 3 * nq:]],
                                axis=2)
        h0 = 4 * kvh
        sk = jnp.where(rid == 0, sink_ref[h0],
                       jnp.where(rid == 1, sink_ref[h0 + 2],
                                 jnp.where(rid == 2, sink_ref[h0 + 1], sink_ref[h0 + 3]))) * LOG2E
        m = jnp.maximum(jnp.max(s, axis=2, keepdims=True), sk)
        probs.append(jnp.exp2((s - m).astype(BF16)).reshape(4 * nq, -1))
        sink_terms.append(jnp.exp2(sk - m).reshape(4 * nq, 1))

    outs = [_dot(e, v_ext) for e, (_, _, (_, _, v_ext), _) in zip(probs, chains)]

    for (row0, kvh, _, _), o, es in zip(chains, outs, sink_terms):
        rows = pl.ds(row0, nq)
        swapped = pltpu.roll(o, HEAD_DIM, 1)
        even = o[0:2 * nq] / (swapped[0:2 * nq] + es[0:2 * nq])
        odd = swapped[2 * nq:] / (o[2 * nq:] + es[2 * nq:])
        g0 = jnp.where(lane < HEAD_DIM, even[0:nq], odd[0:nq])
        g1 = jnp.where(lane < HEAD_DIM, even[nq:], odd[nq:])
        o_ref[rows, 256 * kvh:256 * kvh + LANES] = g0.astype(BF16)
        o_ref[rows, 256 * kvh + LANES:256 * kvh + 2 * LANES] = g1.astype(BF16)


def _attn_key_parts(k, v):
    lane = lax.broadcasted_iota(jnp.int32, (1, LANES), 1)
    low = lane < HEAD_DIM
    swap = lambda a: jnp.concatenate([a[:, HEAD_DIM:], a[:, :HEAD_DIM]], axis=1)
    ks, vs = swap(k), swap(v)
    zero, one = jnp.zeros_like(k), jnp.ones_like(v)
    return [(jnp.where(low, k, zero), jnp.where(low, zero, ks), jnp.where(low, v, one)),
            (jnp.where(low, ks, zero), jnp.where(low, zero, k), jnp.where(low, vs, one))]


def _attn_lat_kernel(sink_ref, q_ref, kl_ref, kc_ref, kr_ref, kx_ref,
                     vl_ref, vc_ref, vr_ref, vx_ref, o_ref, *, nsteps, nblk):
    i = pl.program_id(1)
    nq = ATTN_BLOCK
    rows = lax.broadcasted_iota(jnp.int32, (nq, nq), 0)
    cols = lax.broadcasted_iota(jnp.int32, (nq, nq), 1)
    upper = cols >= rows
    lower = cols <= rows
    parts = [_attn_key_parts(kr[...], vr[...])
             for kr, vr in ((kl_ref, vl_ref), (kc_ref, vc_ref), (kr_ref, vr_ref), (kx_ref, vx_ref))]
    per_head = []
    for kvh in range(N_KV_HEADS):
        left, cen, right, ctx = [p[kvh] for p in parts]
        own = [tuple(t[j * nq:(j + 1) * nq] for t in cen) for j in range(nblk)]
        per_head.append(([left] + own + [right], ctx))
    blocks = []
    for j in range(nblk):
        keys = [tuple(jnp.concatenate([p[t] for p in seq[j:j + 3]] + [ctx[t]], axis=0) for t in range(3))
                for seq, ctx in per_head]
        ok_left = upper & (i >= 1) if j == 0 else upper
        ok_right = lower & (i <= nsteps - 2) if j == nblk - 1 else lower
        blocks.append((j * nq, keys, (ok_left, ok_right)))
    _attn_blocks(sink_ref, q_ref, o_ref, blocks)


def _attn_ctx_kernel(sink_ref, q_ref, kx_ref, vx_ref, o_ref):
    _attn_blocks(sink_ref, q_ref, o_ref, [(0, _attn_key_parts(kx_ref[...], vx_ref[...]), None)])


def _attention_latent(sink, q, kd, vd, kdc, vdc):
    b, l, _ = q.shape
    lc = kdc.shape[1]
    nq = ATTN_BLOCK
    nb = l // nq
    nblk = min(ATTN_BLOCKS_PER_STEP, nb)
    assert nb % nblk == 0
    nsteps = nb // nblk
    smem = pl.BlockSpec(memory_space=pltpu.SMEM)
    edge = lambda f: pl.BlockSpec((None, nq, KV_WIDTH), f)
    left = lambda bb, i: (bb, jnp.maximum(nblk * i - 1, 0), 0)
    cen = lambda bb, i: (bb, i, 0)
    right = lambda bb, i: (bb, jnp.minimum(nblk * i + nblk, nb - 1), 0)
    mid = pl.BlockSpec((None, nblk * nq, KV_WIDTH), cen)
    ctx = pl.BlockSpec((None, lc, KV_WIDTH), lambda bb, i: (bb, 0, 0))
    return pl.pallas_call(
        functools.partial(_attn_lat_kernel, nsteps=nsteps, nblk=nblk),
        grid=(b, nsteps),
        in_specs=[smem, pl.BlockSpec((None, nblk * nq, ATTN_WIDTH), cen),
                  edge(left), mid, edge(right), ctx,
                  edge(left), mid, edge(right), ctx],
        out_specs=pl.BlockSpec((None, nblk * nq, ATTN_WIDTH), cen),
        out_shape=jax.ShapeDtypeStruct((b, l, ATTN_WIDTH), BF16),
        compiler_params=_cparams(("arbitrary", "arbitrary")),
        name="attn_latent",
    )(sink, q, kd, kd, kd, kdc, vd, vd, vd, vdc)


def _attention_ctx(sink, qc, kdc, vdc):
    b, lc, _ = qc.shape
    nq = ATTN_BLOCK
    smem = pl.BlockSpec(memory_space=pltpu.SMEM)
    ctx = pl.BlockSpec((None, lc, KV_WIDTH), lambda bb, i: (bb, 0, 0))
    return pl.pallas_call(
        _attn_ctx_kernel,
        grid=(b, lc // nq),
        in_specs=[smem, pl.BlockSpec((None, nq, ATTN_WIDTH), lambda bb, i: (bb, i, 0)), ctx, ctx],
        out_specs=pl.BlockSpec((None, nq, ATTN_WIDTH), lambda bb, i: (bb, i, 0)),
        out_shape=jax.ShapeDtypeStruct((b, lc, ATTN_WIDTH), BF16),
        compiler_params=_cparams(("arbitrary", "arbitrary")),
        name="attn_ctx",
    )(sink, qc, kdc, vdc)


def _pool_band_matrices():
    r = np.arange(POOL_TILE)[:, None]
    a = np.arange(POOL_TILE + 2 * POOL_HALO)[None, :] - POOL_HALO
    mats = [((a >= r - w // 2) & (a < r + w - w // 2)) for w in POOL_WINDOWS]
    return jnp.asarray(np.concatenate(mats, axis=0).astype(np.float32), dtype=BF16)


def _pool_kernel(u_ref, a_ref, w_ref, sc_ref, o_ref, pad_ref, *, ls):
    halo = POOL_HALO
    zeros = jnp.zeros((halo, POOL_WIDTH), BF16)
    pad_ref[0:halo, :] = zeros
    pad_ref[halo + ls:2 * halo + ls, :] = zeros
    pad_ref[halo:halo + ls, :] = u_ref[...]
    lane = lax.broadcasted_iota(jnp.int32, (1, POOL_WIDTH), 1)
    grp = lane // POOL_GROUP
    wl = jnp.where(grp == 0, POOL_WINDOWS[0],
                   jnp.where(grp == 1, POOL_WINDOWS[1],
                             jnp.where(grp == 2, POOL_WINDOWS[2], POOL_WINDOWS[3])))
    half = wl // 2

    ntile = ls // POOL_TILE
    grp_tiles = min(8, ntile)
    assert ntile % grp_tiles == 0
    nw = len(POOL_WINDOWS)

    def body(gi, carry):
        t0s = [pl.multiple_of((gi * grp_tiles + j) * POOL_TILE, POOL_TILE) for j in range(grp_tiles)]
        sums = [_dot(a_ref[...], pad_ref[pl.ds(t0, POOL_TILE + 2 * halo), :]) for t0 in t0s]
        ds = []
        for t0, sm in zip(t0s, sums):
            acc = sm[0:POOL_TILE]
            for g in range(1, nw):
                acc = jnp.where(grp == g, sm[g * POOL_TILE:(g + 1) * POOL_TILE], acc)
            p = t0 + lax.broadcasted_iota(jnp.int32, (POOL_TILE, 1), 0)
            hi = jnp.minimum(p + (wl - half), ls)
            lo = jnp.maximum(p - half, 0)
            cnt = (hi - lo).astype(F32)
            ut = u_ref[pl.ds(t0, POOL_TILE), :].astype(F32)
            ds.append((acc / cnt - ut).astype(BF16))
        rows = pl.ds(t0s[0], grp_tiles * POOL_TILE)
        o_ref[rows, :] = (_dot(jnp.concatenate(ds, axis=0), w_ref[...]) * sc_ref[...]).astype(BF16)
        return carry

    lax.fori_loop(0, ntile // grp_tiles, body, 0)


def _pool(u, band, wblk, scale):
    b, ls, _ = u.shape
    return pl.pallas_call(
        functools.partial(_pool_kernel, ls=ls),
        grid=(b,),
        in_specs=[
            pl.BlockSpec((None, ls, POOL_WIDTH), lambda bb: (bb, 0, 0)),
            pl.BlockSpec(band.shape, lambda bb: (0, 0)),
            pl.BlockSpec((POOL_WIDTH, POOL_WIDTH), lambda bb: (0, 0)),
            pl.BlockSpec((1, POOL_WIDTH), lambda bb: (0, 0)),
        ],
        out_specs=pl.BlockSpec((None, ls, POOL_WIDTH), lambda bb: (bb, 0, 0)),
        out_shape=jax.ShapeDtypeStruct((b, ls, POOL_WIDTH), BF16),
        scratch_shapes=[pltpu.VMEM((ls + 2 * POOL_HALO, POOL_WIDTH), BF16)],
        compiler_params=_cparams(("arbitrary",)),
        name="pool",
    )(u, band, wblk, scale)


def _ret_kernel(lg_ref, rq_ref, rk_ref, rv_ref, rg_ref, s0f_ref, s0b_ref, gmat_ref,
                o_ref, sf_ref, sb_ref,
                kvf_ref, kvb_ref, sp_ref, dm_ref, tab_ref, *, nc):
    c_len = RET_CHUNK
    kw, vw = RET_KW, RET_WIDTH

    def per_head(idx, d):
        return jnp.where(idx == 0, lg_ref[d, 0],
                         jnp.where(idx == 1, lg_ref[d, 1],
                                   jnp.where(idx == 2, lg_ref[d, 2], lg_ref[d, 3])))

    hk = lax.broadcasted_iota(jnp.int32, (1, kw), 1) // RET_DK
    hv = lax.broadcasted_iota(jnp.int32, (1, vw), 1) // RET_DV

    @pl.when(pl.program_id(0) == 0)
    def _():
        n_col = lax.broadcasted_iota(jnp.int32, (c_len, 1), 0).astype(F32)
        lgk_f, lgk_b = per_head(hk, 0), per_head(hk, 1)
        tab_ref[0] = jnp.exp(lgk_f * (c_len - 1.0 - n_col))
        tab_ref[1] = jnp.exp(lgk_b * n_col)
        tab_ref[2] = jnp.exp(lgk_f * (n_col + 1.0))
        tab_ref[3] = jnp.exp(lgk_b * (c_len - n_col))
        hs = lax.broadcasted_iota(jnp.int32, (1, RET_HEADS * c_len), 1) // c_len
        m_idx = (lax.broadcasted_iota(jnp.int32, (c_len, RET_HEADS * c_len), 1) & (c_len - 1)).astype(F32)
        n_idx = lax.broadcasted_iota(jnp.int32, (c_len, RET_HEADS * c_len), 0).astype(F32)
        rel = n_idx - m_idx
        dm_ref[0] = jnp.where(rel >= 0, jnp.exp(per_head(hs, 0) * jnp.maximum(rel, 0.0)), 0.0)
        dm_ref[1] = jnp.where(rel <= 0, jnp.exp(per_head(hs, 1) * jnp.maximum(-rel, 0.0)), 0.0)

    bd = (lax.broadcasted_iota(jnp.int32, (kw, vw), 0) // RET_DK) == (lax.broadcasted_iota(jnp.int32, (kw, vw), 1) // RET_DV)
    hk_col = lax.broadcasted_iota(jnp.int32, (kw, 1), 0) // RET_DK
    cd_f = jnp.exp(per_head(hk_col, 0) * float(c_len))
    cd_b = jnp.exp(per_head(hk_col, 1) * float(c_len))

    def rows(c):
        return pl.ds(pl.multiple_of(c * c_len, c_len), c_len)

    grp = min(8, nc)
    assert nc % grp == 0

    def kv_body(gi, carry):
        cs = [gi * grp + j for j in range(grp)]
        ks = [rk_ref[rows(c), :].astype(F32) for c in cs]
        vs = [rv_ref[rows(c), :] for c in cs]
        kf = [(k * tab_ref[0]).astype(BF16) for k in ks]
        kb = [(k * tab_ref[1]).astype(BF16) for k in ks]
        pf = [_dot_tn(a, v) for a, v in zip(kf, vs)]
        pb = [_dot_tn(a, v) for a, v in zip(kb, vs)]
        for j, c in enumerate(cs):
            kvf_ref[c] = jnp.where(bd, pf[j], 0.0)
            kvb_ref[c] = jnp.where(bd, pb[j], 0.0)
        return carry

    lax.fori_loop(0, nc // grp, kv_body, 0)

    def scan_body(j, carry):
        s_f, s_b = carry
        cb = nc - 1 - j
        sp_ref[j, 0:kw, :] = s_f.astype(BF16)
        sp_ref[cb, kw:2 * kw, :] = s_b.astype(BF16)
        return cd_f * s_f + kvf_ref[j], cd_b * s_b + kvb_ref[cb]

    s_f, s_b = lax.fori_loop(0, nc, scan_body, (s0f_ref[...], s0b_ref[...]))
    sf_ref[...] = s_f
    sb_ref[...] = s_b

    def out_body(gi, carry):
        cs = [gi * grp + j for j in range(grp)]
        qs = [rq_ref[rows(c), :] for c in cs]
        ks = [rk_ref[rows(c), :] for c in cs]
        vs = [rv_ref[rows(c), :] for c in cs]
        zk, zv = jnp.zeros_like(ks[0]), jnp.zeros_like(vs[0])
        ksts = [jnp.concatenate([jnp.where(hk == h, k, zk) for h in range(RET_HEADS)], axis=0) for k in ks]
        vsts = [jnp.concatenate([jnp.where(hv == h, v, zv) for h in range(RET_HEADS)], axis=0) for v in vs]
        scs = [_dot_nt(q, kst) for q, kst in zip(qs, ksts)]
        p2s = [jnp.concatenate([(sc * dm_ref[0]).astype(BF16), (sc * dm_ref[1]).astype(BF16)], axis=0)
               for sc in scs]
        q2s = [jnp.concatenate([(q.astype(F32) * tab_ref[2]).astype(BF16),
                                (q.astype(F32) * tab_ref[3]).astype(BF16)], axis=1) for q in qs]
        o2s = [_dot(p2, vst) for p2, vst in zip(p2s, vsts)]
        ocs = [_dot(q2, sp_ref[c]) for q2, c in zip(q2s, cs)]
        o = jnp.concatenate([o2[0:c_len] + o2[c_len:] + oc for o2, oc in zip(o2s, ocs)], axis=0)
        mu = _dot(o.astype(BF16), gmat_ref[...])
        d = o - mu
        var = _dot((d * d).astype(BF16), gmat_ref[...])
        hn = d * lax.rsqrt(var + LN_EPS)
        grows = pl.ds(pl.multiple_of(gi * (grp * c_len), grp * c_len), grp * c_len)
        g = rg_ref[grows, :].astype(F32)
        o_ref[grows, :] = (jax.nn.silu(g) * hn).astype(BF16)
        return carry

    lax.fori_loop(0, nc // grp, out_body, 0)


def _retention(lg, rq, rk, rv, rg, s0f, s0b, gmat):
    b, ls, _ = rq.shape
    nc = ls // RET_CHUNK
    seq = lambda w: pl.BlockSpec((None, ls, w), lambda bb: (bb, 0, 0))
    st = pl.BlockSpec((None, RET_KW, RET_WIDTH), lambda bb: (bb, 0, 0))
    return pl.pallas_call(
        functools.partial(_ret_kernel, nc=nc),
        grid=(b,),
        in_specs=[pl.BlockSpec(memory_space=pltpu.SMEM), seq(RET_KW), seq(RET_KW), seq(RET_WIDTH), seq(RET_WIDTH),
                  st, st, pl.BlockSpec((RET_WIDTH, RET_WIDTH), lambda bb: (0, 0))],
        out_specs=[seq(RET_WIDTH), st, st],
        out_shape=[jax.ShapeDtypeStruct((b, ls, RET_WIDTH), BF16),
                   jax.ShapeDtypeStruct((b, RET_KW, RET_WIDTH), F32),
                   jax.ShapeDtypeStruct((b, RET_KW, RET_WIDTH), F32)],
        scratch_shapes=[
            pltpu.VMEM((nc, RET_KW, RET_WIDTH), F32),
            pltpu.VMEM((nc, RET_KW, RET_WIDTH), F32),
            pltpu.VMEM((nc, 2 * RET_KW, RET_WIDTH), BF16),
            pltpu.VMEM((2, RET_CHUNK, RET_HEADS * RET_CHUNK), F32),
            pltpu.VMEM((4, RET_CHUNK, RET_KW), F32),
        ],
        compiler_params=_cparams(("arbitrary",)),
        name="retention",
    )(lg, rq, rk, rv, rg, s0f, s0b, gmat)


def _mixer_out_ln1(attn_ref, pool_ref, ret_ref, x_ref, m, w_ref, g_ref, b_ref, alpha):
    y = (_dot(attn_ref[...], w_ref[0:ATTN_WIDTH, :])
         + _dot(pool_ref[...], w_ref[ATTN_WIDTH:ATTN_WIDTH + POOL_WIDTH, :])
         + _dot(ret_ref[...], w_ref[ATTN_WIDTH + POOL_WIDTH:, :]))
    x1 = _layer_norm(alpha * x_ref[...] + m[2:3, :] * y, g_ref[...], b_ref[...])
    return x1, x1 * (1.0 + m[4:5, :]) + m[3:4, :]


def _route_tile(logits_t, tri_ref, carry_ref):
    ne, tm = logits_t.shape
    row = lax.broadcasted_iota(jnp.int32, (ne, tm), 0)
    l = jnp.where(row < N_EXPERTS, logits_t, -jnp.inf)
    m1 = jnp.max(l, axis=0, keepdims=True)
    i1 = jnp.min(jnp.where(l == m1, row, ne), axis=0, keepdims=True)
    l2 = jnp.where(row == i1, -jnp.inf, l)
    m2 = jnp.max(l2, axis=0, keepdims=True)
    i2 = jnp.min(jnp.where(l2 == m2, row, ne), axis=0, keepdims=True)
    e = jnp.exp(m2 - m1)
    w1 = 1.0 / (1.0 + e)
    w2 = e / (1.0 + e)
    oh = jnp.where((row == i1) | (row == i2), 1.0, 0.0)
    carry = carry_ref[:, 0:1]
    cum = _dot(oh.astype(BF16), tri_ref[...]) + carry
    r1 = jnp.sum(jnp.where(row == i1, cum, 0.0), axis=0, keepdims=True)
    r2 = jnp.sum(jnp.where(row == i2, cum, 0.0), axis=0, keepdims=True)
    carry_ref[...] = jnp.broadcast_to(carry + jnp.sum(oh, axis=1, keepdims=True), carry_ref.shape)
    r8 = lax.broadcasted_iota(jnp.int32, (8, tm), 0)
    return jnp.where(r8 == 0, i1.astype(F32),
                     jnp.where(r8 == 1, i2.astype(F32),
                               jnp.where(r8 == 2, r1,
                                         jnp.where(r8 == 3, r2,
                                                   jnp.where(r8 == 4, w1,
                                                             jnp.where(r8 == 5, w2, 0.0))))))


def _outproj_kernel(attn_ref, pool_ref, ret_ref, x_ref, mod_ref, w_ref, g_ref, b_ref, router_ref, tri_ref,
                    x1_ref, t_ref, tab_ref, tabt_ref, cnt_ref, carry_ref, *, alpha):
    @pl.when((pl.program_id(0) == 0) & (pl.program_id(1) == 0))
    def _():
        carry_ref[...] = jnp.zeros_like(carry_ref)

    x1, t = _mixer_out_ln1(attn_ref, pool_ref, ret_ref, x_ref, mod_ref[...], w_ref, g_ref, b_ref, alpha)
    x1_ref[...] = x1
    _store_tile_rows(t_ref, (), t)
    tab_t = _route_tile(_dot_nt(router_ref[...], t.astype(BF16)), tri_ref, carry_ref)
    tabt_ref[...] = tab_t
    tm = tab_t.shape[1]
    tab_ref[...] = jnp.concatenate([tab_t, jnp.zeros((LANES - 8, tm), F32)], axis=0).T
    cnt_ref[...] = carry_ref[0:8, :]


def _mixer_row_specs(tm, d):
    row = lambda w: pl.BlockSpec((None, tm, w), lambda bb, j: (bb, j, 0))
    return [row(ATTN_WIDTH), row(POOL_WIDTH), row(RET_WIDTH), row(d)]


def _outproj_ln_route(attn, pool, ret, x, modtab, mod_row_fn, w_out, ln_g, ln_b, alpha, router):
    b, ls, d = x.shape
    tm = min(ROUTE_TILE, ls)
    nt = ls // tm
    tri = jnp.asarray(np.triu(np.ones((tm, tm), np.float32), 1), dtype=BF16)
    row = lambda w: pl.BlockSpec((None, tm, w), lambda bb, j: (bb, j, 0))
    const = lambda shape: pl.BlockSpec(shape, lambda bb, j: tuple(0 for _ in shape))
    return pl.pallas_call(
        functools.partial(_outproj_kernel, alpha=alpha),
        grid=(b, nt),
        in_specs=_mixer_row_specs(tm, d) + [
            pl.BlockSpec((None, 6, d), lambda bb, j: (mod_row_fn(bb), 0, 0)),
            const((d, d)), const((1, d)), const((1, d)), const((ROUTER_ROWS, d)), const((tm, tm))],
        out_specs=[row(d), pl.BlockSpec((None, tm * SUBLANES, LANES), lambda bb, j: (bb, j, 0)), row(LANES),
                   pl.BlockSpec((8, tm), lambda bb, j: (0, bb * nt + j)), const((8, LANES))],
        out_shape=[jax.ShapeDtypeStruct((b, ls, d), F32),
                   jax.ShapeDtypeStruct((b, ls * SUBLANES, LANES), F32),
                   jax.ShapeDtypeStruct((b, ls, LANES), F32),
                   jax.ShapeDtypeStruct((8, b * ls), F32),
                   jax.ShapeDtypeStruct((8, LANES), F32)],
        scratch_shapes=[pltpu.VMEM((ROUTER_ROWS, LANES), F32)],
        compiler_params=_cparams(("arbitrary", "arbitrary")),
        name="outproj_ln_route",
    )(attn, pool, ret, x, modtab, w_out, ln_g, ln_b, router, tri)


N_FFN_CHUNKS = FFN_DIM // FFN_CHUNK
N_FFN_GAPS = 3 * N_FFN_CHUNKS


def _swiglu_rows(tb, wg_ref, wu_ref, wd_ref, between=None):
    gap = (lambda j: None) if between is None else between
    f = None
    for c in range(N_FFN_CHUNKS):
        sl = slice(c * FFN_CHUNK, (c + 1) * FFN_CHUNK)
        g = _dot(tb, wg_ref[:, sl])
        gap(3 * c)
        u = _dot(tb, wu_ref[:, sl])
        gap(3 * c + 1)
        a = (jax.nn.silu(g) * u).astype(BF16)
        part = _dot(a, wd_ref[sl, :])
        gap(3 * c + 2)
        f = part if f is None else f + part
    return f


def _dense_post_kernel(attn_ref, pool_ref, ret_ref, x_ref, mod_ref, w_ref, g1_ref, b1_ref,
                       wg_ref, wu_ref, wd_ref, g2_ref, b2_ref, o_ref, *, alpha):
    m = mod_ref[...]
    x1, t = _mixer_out_ln1(attn_ref, pool_ref, ret_ref, x_ref, m, w_ref, g1_ref, b1_ref, alpha)
    f = _swiglu_rows(t.astype(BF16), wg_ref, wu_ref, wd_ref)
    o_ref[...] = _layer_norm(alpha * x1 + m[5:6, :] * f, g2_ref[...], b2_ref[...])


def _dense_post(attn, pool, ret, x, modtab, mod_row_fn, w_out, g1, b1, wg, wu, wd, g2, b2, alpha):
    b, ls, d = x.shape
    tm = min(512, ls)
    nt = ls // tm
    const = lambda shape: pl.BlockSpec(shape, lambda bb, j: tuple(0 for _ in shape),
                                       pipeline_mode=pl.Buffered(1))
    return pl.pallas_call(
        functools.partial(_dense_post_kernel, alpha=alpha),
        grid=(b, nt),
        in_specs=_mixer_row_specs(tm, d) + [
            pl.BlockSpec((None, 6, d), lambda bb, j: (mod_row_fn(bb), 0, 0)),
            const((d, d)), const((1, d)), const((1, d)),
            const((d, FFN_DIM)), const((d, FFN_DIM)), const((FFN_DIM, d)),
            const((1, d)), const((1, d))],
        out_specs=pl.BlockSpec((None, tm, d), lambda bb, j: (bb, j, 0)),
        out_shape=jax.ShapeDtypeStruct((b, ls, d), F32),
        compiler_params=_cparams(("arbitrary", "arbitrary")),
        name="dense_post",
    )(attn, pool, ret, x, modtab, w_out, g1, b1, wg, wu, wd, g2, b2)


def _invert_kernel(p1_ref, p2_ref, lo_ref, hi_ref, dst_ref, *, n):
    for e in range(N_EXPERTS + 1):
        def fill(s, carry):
            dst_ref[s] = 2 * n + (s & (MOE_TILE - 1))
            return carry

        lax.fori_loop(lo_ref[e], hi_ref[e], fill, 0)

    def body(t, carry):
        dst_ref[p1_ref[t]] = t
        dst_ref[p2_ref[t]] = n + t
        return carry

    lax.fori_loop(0, n, body, 0, unroll=16)


def _invert(pos1, pos2, fill_lo, fill_hi, nslots):
    n = pos1.shape[0]
    smem = pl.BlockSpec(memory_space=pltpu.SMEM)
    return pl.pallas_call(
        functools.partial(_invert_kernel, n=n),
        in_specs=[smem, smem, smem, smem],
        out_specs=smem,
        out_shape=jax.ShapeDtypeStruct((nslots,), jnp.int32),
        name="invert",
    )(pos1, pos2, fill_lo, fill_hi)


def _slot_source_row(v, n):
    if n & (n - 1) == 0:
        return v & (n - 1)
    return jnp.where(v >= 2 * n, v - 2 * n, jnp.where(v >= n, v - n, v))


def _moe_kernel(te_ref, nt_ref, inv_ref, t_ref, wg_ref, wu_ref, wd_ref, y_ref, xbuf, obuf, gsem, ssem, *, n):
    del te_ref
    i = pl.program_id(0)
    nt = nt_ref[0]
    tm = MOE_TILE
    slot = i & 1
    other = 1 - slot

    def row_tile(r):
        return pl.ds(pl.multiple_of(r * SUBLANES, SUBLANES), SUBLANES)

    def gather_row(tile, r, buf):
        src = _slot_source_row(inv_ref[tile * tm + r], n)
        pltpu.make_async_copy(t_ref.at[row_tile(src)], xbuf.at[buf, row_tile(r)], gsem.at[buf]).start()

    def scatter_row(tile, r, buf, real):
        dst = jnp.where(real, inv_ref[tile * tm + r], 2 * n + r)
        pltpu.make_async_copy(obuf.at[buf, row_tile(r)], y_ref.at[row_tile(dst)], ssem.at[buf]).start()

    def wait_gather(buf):
        pltpu.make_async_copy(t_ref.at[pl.ds(0, tm * SUBLANES)], xbuf.at[buf], gsem.at[buf]).wait()

    def wait_scatter(buf):
        pltpu.make_async_copy(obuf.at[buf], y_ref.at[pl.ds(0, tm * SUBLANES)], ssem.at[buf]).wait()

    @pl.when(i == 0)
    def _():
        obuf[1] = jnp.zeros(obuf.shape[1:], obuf.dtype)

        def first(r, carry):
            gather_row(0, r, 0)
            return carry

        lax.fori_loop(0, tm, first, 0, unroll=8)

    def main(cur, oth):
        wait_gather(cur)
        nxt = jnp.minimum(i + 1, nt - 1)
        prv = jnp.maximum(i - 1, 0)
        real = i >= 1
        n_gather_gaps = N_FFN_GAPS // 2
        per_g = -(-tm // n_gather_gaps)
        per_s = -(-tm // (N_FFN_GAPS - n_gather_gaps))

        def between(j):
            if j < n_gather_gaps:
                for r in range(j * per_g, min((j + 1) * per_g, tm)):
                    gather_row(nxt, r, oth)
            else:
                k = j - n_gather_gaps
                for r in range(k * per_s, min((k + 1) * per_s, tm)):
                    scatter_row(prv, r, oth, real)

        x = _load_tile_rows(xbuf, (cur,), tm).astype(BF16)
        f = _swiglu_rows(x, wg_ref, wu_ref, wd_ref, between)

        @pl.when(i >= 1)
        def _():
            wait_scatter(cur)

        _store_tile_rows(obuf, (cur,), f)

    @pl.when((i < nt) & (slot == 0))
    def _():
        main(0, 1)

    @pl.when((i < nt) & (slot == 1))
    def _():
        main(1, 0)

    @pl.when(i == nt)
    def _():
        last = nt - 1
        buf = last & 1
        wait_scatter(nt & 1)

        def tail(r, carry):
            scatter_row(last, r, buf, True)
            return carry

        lax.fori_loop(0, tm, tail, 0, unroll=8)
        wait_scatter(buf)
        wait_gather(nt & 1)


def _moe_ffn(tile_expert, ntiles, inv, t, wg, wu, wd):
    n = t.shape[0] // SUBLANES
    d = D_MODEL
    tm = MOE_TILE
    assert n >= tm
    nt_max = inv.shape[0] // tm
    wspec = lambda shape: pl.BlockSpec((None,) + shape, lambda i, te, nt, iv: (te[i], 0, 0))
    return pl.pallas_call(
        functools.partial(_moe_kernel, n=n),
        grid_spec=pltpu.PrefetchScalarGridSpec(
            num_scalar_prefetch=3,
            grid=(nt_max,),
            in_specs=[pl.BlockSpec(memory_space=pl.ANY),
                      wspec((d, FFN_DIM)), wspec((d, FFN_DIM)), wspec((FFN_DIM, d))],
            out_specs=pl.BlockSpec(memory_space=pl.ANY),
            scratch_shapes=[pltpu.VMEM((2, tm * SUBLANES, LANES), F32), pltpu.VMEM((2, tm * SUBLANES, LANES), F32),
                            pltpu.SemaphoreType.DMA((2,)), pltpu.SemaphoreType.DMA((2,))],
        ),
        out_shape=jax.ShapeDtypeStruct(((2 * n + tm) * SUBLANES, LANES), F32),
        compiler_params=_cparams(("arbitrary",)),
        name="moe_ffn",
    )(tile_expert, ntiles, inv, t, wg, wu, wd)


def _combine_kernel(y1_ref, y2_ref, x1_ref, rt_ref, mod_ref, g_ref, b_ref, o_ref, *, alpha):
    rt = rt_ref[...]
    tm = x1_ref.shape[0]
    f = rt[:, 4:5] * _load_tile_rows(y1_ref, (), tm) + rt[:, 5:6] * _load_tile_rows(y2_ref, (), tm)
    m = mod_ref[...]
    o_ref[...] = _layer_norm(alpha * x1_ref[...] + m[5:6, :] * f, g_ref[...], b_ref[...])


def _combine_ln(y, x1, rtab, modtab, l, ln_g, ln_b, alpha):
    n, d = x1.shape
    tm = min(ROW_TILE, n)
    per_b = l // tm
    nblk = n // tm
    const = lambda shape: pl.BlockSpec(shape, lambda i: tuple(0 for _ in shape))
    return pl.pallas_call(
        functools.partial(_combine_kernel, alpha=alpha),
        grid=(nblk,),
        in_specs=[pl.BlockSpec((tm * SUBLANES, LANES), lambda i: (i, 0)),
                  pl.BlockSpec((tm * SUBLANES, LANES), lambda i: (nblk + i, 0)),
                  pl.BlockSpec((tm, d), lambda i: (i, 0)),
                  pl.BlockSpec((tm, LANES), lambda i: (i, 0)),
                  pl.BlockSpec((None, 6, d), lambda i: (i // per_b, 0, 0)),
                  const((1, d)), const((1, d))],
        out_specs=pl.BlockSpec((tm, d), lambda i: (i, 0)),
        out_shape=jax.ShapeDtypeStruct((n, d), F32),
        compiler_params=_cparams(("arbitrary",)),
        name="combine_ln",
    )(y, y, x1, rtab, modtab, ln_g, ln_b)


def _moe_layer(t, rtab, rtab_t, cnt, x1, modtab, l, wg, wu, wd, ln_g, ln_b, alpha):
    n = x1.shape[0]
    counts = cnt[:N_EXPERTS, 0].astype(jnp.int32)
    tiles_e = (counts + MOE_TILE - 1) // MOE_TILE
    tile_end = jnp.cumsum(tiles_e)
    tile_start = tile_end - tiles_e
    ntiles = tile_end[-1:]
    nt_max = (2 * n) // MOE_TILE + N_EXPERTS + 1
    tile_expert = jnp.minimum(
        jnp.sum((jnp.arange(nt_max)[:, None] >= tile_end[None, :]).astype(jnp.int32), axis=1),
        N_EXPERTS - 1).astype(jnp.int32)
    e1 = rtab_t[0].astype(jnp.int32)
    e2 = rtab_t[1].astype(jnp.int32)
    row_start = tile_start * MOE_TILE
    pos1 = row_start[e1] + rtab_t[2].astype(jnp.int32)
    pos2 = row_start[e2] + rtab_t[3].astype(jnp.int32)
    fill_lo = jnp.concatenate([row_start + counts, tile_end[-1:] * MOE_TILE]).astype(jnp.int32)
    fill_hi = jnp.concatenate([tile_end * MOE_TILE, jnp.full((1,), nt_max * MOE_TILE)]).astype(jnp.int32)
    inv = _invert(pos1, pos2, fill_lo, fill_hi, nt_max * MOE_TILE)
    y = _moe_ffn(tile_expert, ntiles.astype(jnp.int32), inv, t, wg, wu, wd)
    return _combine_ln(y, x1, rtab, modtab, l, ln_g, ln_b, alpha)


def _prep_w_in(w):
    q = w[:, 0:ATTN_WIDTH] * (HEAD_DIM ** -0.5 * LOG2E)
    return jnp.concatenate([q, w[:, ATTN_WIDTH:]], axis=1).astype(BF16)


def _block_diag(mats):
    n = len(mats)
    rows = []
    for i, m in enumerate(mats):
        rows.append(jnp.concatenate([m if j == i else jnp.zeros_like(m) for j in range(n)], axis=1))
    return jnp.concatenate(rows, axis=0)


def kernel(x, c, ctx, c_ctx, w_mod, b_mod, w_in, attn_sink, pool_w, pool_scale, ret_log_decay_fwd,
           ret_log_decay_bwd, w_out, ln1_g, ln1_b, ln2_g, ln2_b, ffn_w_gate, ffn_w_up, ffn_w_down,
           moe_router, moe_w_gate, moe_w_up, moe_w_down):
    b, l, d = x.shape
    lc = ctx.shape[1]
    depth = w_in.shape[0]
    alpha = (2.0 * depth) ** 0.25
    assert d == D_MODEL and l % 256 == 0 and lc % 256 == 0 and l % GRID_W == 0

    n_rows = ((b + 1 + 7) // 8) * 8
    c_all = jnp.concatenate([c, c_ctx[None, :], jnp.zeros((n_rows - b - 1, d), F32)], axis=0)
    mod_all = _modulation(c_all, w_mod, b_mod).reshape(depth, n_rows, 6, d)
    lat_row = lambda bb: bb
    ctx_row = lambda bb: b

    rope_tabs = _rope_tables(l)
    band = _pool_band_matrices()
    gmat = jnp.asarray(np.kron(np.eye(RET_HEADS), np.ones((RET_DV, RET_DV))) / RET_DV, dtype=BF16)
    zero_state = jnp.zeros((b, RET_KW, RET_WIDTH), F32)

    xc = ctx
    for i in range(depth):
        last = i == depth - 1
        modtab = mod_all[i]
        w2 = _prep_w_in(w_in[i])
        wo = w_out[i].astype(BF16)
        sink = attn_sink[i].astype(F32)
        pw = _block_diag([pool_w[i, g] for g in range(len(POOL_WINDOWS))]).astype(BF16)
        ps = pool_scale[i].reshape(1, POOL_WIDTH).astype(F32)
        lg = jnp.stack([ret_log_decay_fwd[i], ret_log_decay_bwd[i]]).astype(F32)
        g1, b1 = ln1_g[i].reshape(1, d), ln1_b[i].reshape(1, d)
        g2, b2 = ln2_g[i].reshape(1, d), ln2_b[i].reshape(1, d)

        qc, kdc, vdc, uc, rqc, rkc, rvc, rgc = _inproj(xc, modtab, ctx_row, w2, None)
        ret_c, s_f, s_b = _retention(lg, rqc, rkc, rvc, rgc, zero_state, zero_state, gmat)

        q, kd, vd, u, rq, rk, rv, rg = _inproj(x, modtab, lat_row, w2, rope_tabs)
        attn = _attention_latent(sink, q, kd, vd, kdc, vdc)
        pool = _pool(u, band, pw, ps)
        ret, _, _ = _retention(lg, rq, rk, rv, rg, s_f, s_b, gmat)

        j = i // 2
        if i % 2 == 0:
            wg, wu, wd = ffn_w_gate[j].astype(BF16), ffn_w_up[j].astype(BF16), ffn_w_down[j].astype(BF16)
            x = _dense_post(attn, pool, ret, x, modtab, lat_row, wo, g1, b1, wg, wu, wd, g2, b2, alpha)
        else:
            wg, wu, wd = moe_w_gate[j].astype(BF16), moe_w_up[j].astype(BF16), moe_w_down[j].astype(BF16)
            router = jnp.pad(moe_router[j].T, ((0, ROUTER_ROWS - N_EXPERTS), (0, 0))).astype(BF16)
            x1, t, rtab, rtab_t, cnt = _outproj_ln_route(attn, pool, ret, x, modtab, lat_row, wo, g1, b1, alpha,
                                                         router)
            x = _moe_layer(t.reshape(b * l * SUBLANES, LANES), rtab.reshape(b * l, LANES), rtab_t, cnt,
                           x1.reshape(b * l, d), modtab, l, wg, wu, wd, g2, b2, alpha).reshape(b, l, d)

        if not last:
            attn_c = _attention_ctx(sink, qc, kdc, vdc)
            pool_c = _pool(uc, band, pw, ps)
            if i % 2 == 0:
                xc = _dense_post(attn_c, pool_c, ret_c, xc, modtab, ctx_row, wo, g1, b1, wg, wu, wd, g2, b2,
                                 alpha)
            else:
                x1c, tc, rtab_c, rtab_tc, cnt_c = _outproj_ln_route(attn_c, pool_c, ret_c, xc, modtab, ctx_row, wo,
                                                                    g1, b1, alpha, router)
                ctx_mod = jnp.broadcast_to(modtab[b:b + 1], (b, 6, d))
                xc = _moe_layer(tc.reshape(b * lc * SUBLANES, LANES), rtab_c.reshape(b * lc, LANES), rtab_tc,
                                cnt_c, x1c.reshape(b * lc, d), ctx_mod, lc, wg, wu, wd, g2, b2,
                                alpha).reshape(b, lc, d)
    return x
```

```python
import functools
import math

import numpy as np
import jax
import jax.numpy as jnp
from jax import lax
from jax.experimental import pallas as pl
from jax.experimental.pallas import tpu as pltpu

F32 = jnp.float32
BF16 = jnp.bfloat16

D_MODEL = 1024
GRID_W = 64
HEAD_DIM = 64
N_HEADS = 8
N_KV_HEADS = 2
ATTN_WIDTH = N_HEADS * HEAD_DIM
ATTN_BLOCK = 128
ATTN_BLOCKS_PER_STEP = 8
ROPE_BASE = 10000.0
POOL_WINDOWS = (2, 4, 8, 16)
POOL_WIDTH = 256
POOL_GROUP = 64
POOL_TILE = 256
POOL_HALO = 16
RET_HEADS = 4
RET_DK = 32
RET_DV = 64
RET_WIDTH = 256
RET_KW = RET_HEADS * RET_DK
RET_CHUNK = 128
FFN_DIM = 2816
FFN_CHUNK = 256
N_EXPERTS = 8
ROUTER_ROWS = 16
ROUTE_TILE = 512
MOE_TILE = 512
ROW_TILE = 512
LN_EPS = 1e-5
NEG_INF = -1e30
LOG2E = math.log2(math.e)
LANES = 128
IN_WIDTH = 1792

VMEM_LIMIT = 56 * 1024 * 1024


def _cparams(sem):
    return pltpu.CompilerParams(dimension_semantics=sem, vmem_limit_bytes=VMEM_LIMIT)


def _dot(a, b):
    return jnp.dot(a, b, preferred_element_type=F32)


def _dot_nt(a, b):
    return lax.dot_general(a, b, (((1,), (1,)), ((), ())), preferred_element_type=F32)


def _dot_tn(a, b):
    return lax.dot_general(a, b, (((0,), (0,)), ((), ())), preferred_element_type=F32)


def _split_bf16(a):
    hi = a.astype(BF16)
    lo = (a - hi.astype(F32)).astype(BF16)
    return hi, lo


def _dot3(a, b):
    ah, al = _split_bf16(a)
    bh, bl = _split_bf16(b)
    return _dot(ah, bh) + _dot(ah, bl) + _dot(al, bh)


SUBLANES = 8
ROW_TILES = D_MODEL // LANES


def _store_tile_rows(ref, idx, val):
    rows = val.shape[0]
    for c in range(ROW_TILES):
        ref[idx + (pl.ds(c, rows, stride=SUBLANES), slice(None))] = val[:, c * LANES:(c + 1) * LANES]


def _load_tile_rows(ref, idx, rows):
    return jnp.concatenate([ref[idx + (pl.ds(c, rows, stride=SUBLANES), slice(None))]
                            for c in range(ROW_TILES)], axis=1)


def _layer_norm(z, g, b):
    mu = jnp.mean(z, axis=-1, keepdims=True)
    d = z - mu
    var = jnp.mean(d * d, axis=-1, keepdims=True)
    return d * lax.rsqrt(var + LN_EPS) * g + b


def _mod_kernel(c_ref, w_ref, b_ref, o_ref):
    s = jax.nn.silu(c_ref[...])
    o_ref[...] = _dot3(s, w_ref[...]) + b_ref[...]


def _modulation(c_all, w_mod, b_mod):
    depth, d, n6 = w_mod.shape
    r = c_all.shape[0]
    bn = 1536
    return pl.pallas_call(
        _mod_kernel,
        grid=(depth, n6 // bn),
        in_specs=[
            pl.BlockSpec((r, d), lambda i, j: (0, 0)),
            pl.BlockSpec((None, d, bn), lambda i, j: (i, 0, j)),
            pl.BlockSpec((None, 1, bn), lambda i, j: (i, 0, j)),
        ],
        out_specs=pl.BlockSpec((None, r, bn), lambda i, j: (i, 0, j)),
        out_shape=jax.ShapeDtypeStruct((depth, r, n6), F32),
        compiler_params=_cparams(("arbitrary", "arbitrary")),
        name="modulation",
    )(c_all, w_mod, b_mod.reshape(depth, 1, n6))


KV_WIDTH = N_KV_HEADS * HEAD_DIM
_IN_OUT_WIDTHS = (ATTN_WIDTH, KV_WIDTH, KV_WIDTH, POOL_WIDTH, RET_KW, RET_KW, RET_WIDTH, RET_WIDTH)
_IN_COL_STARTS = tuple(int(c) for c in np.cumsum((0,) + _IN_OUT_WIDTHS))


def _rope_groups(a, cos, s_prev, s_next):
    outs = []
    for g in range(a.shape[1] // LANES):
        ag = a[:, g * LANES:(g + 1) * LANES]
        outs.append(ag * cos + pltpu.roll(ag, 16, 1) * s_prev + pltpu.roll(ag, LANES - 16, 1) * s_next)
    return outs[0] if len(outs) == 1 else jnp.concatenate(outs, axis=1)


def _inproj_kernel(*refs, rope):
    if rope:
        x_ref, mod_ref, w_ref, cos_ref, sp_ref, sn_ref = refs[:6]
        outs = refs[6:]
    else:
        x_ref, mod_ref, w_ref = refs[:3]
        outs = refs[3:]
    q_ref, k_ref, v_ref, u_ref, rq_ref, rk_ref, rv_ref, rg_ref = outs
    m = mod_ref[...]
    h = (x_ref[...] * (1.0 + m[1:2, :]) + m[0:1, :]).astype(BF16)

    def mm(j0, j1):
        return _dot(h, w_ref[:, _IN_COL_STARTS[j0]:_IN_COL_STARTS[j1]])

    q = mm(0, 1)
    kv = mm(1, 3)
    k, v = kv[:, :KV_WIDTH], kv[:, KV_WIDTH:]
    if rope:
        cos, sp, sn = cos_ref[...], sp_ref[...], sn_ref[...]
        q = _rope_groups(q, cos, sp, sn)
        k = _rope_groups(k, cos, sp, sn)
    q_ref[...] = q.astype(BF16)
    k_ref[...] = k.astype(BF16)
    v_ref[...] = v.astype(BF16)
    u_ref[...] = mm(3, 4).astype(BF16)
    rqk = mm(4, 6)
    rq_ref[...] = rqk[:, :RET_KW].astype(BF16)
    rk_ref[...] = (rqk[:, RET_KW:] * (RET_DK ** -0.5)).astype(BF16)
    rv_ref[...] = mm(6, 7).astype(BF16)
    rg_ref[...] = mm(7, 8).astype(BF16)


def _inproj(x, modtab, mod_row_fn, w2, rope_tabs):
    b, ls, d = x.shape
    tm = min(1024, ls)
    nt = ls // tm
    rope = rope_tabs is not None
    in_specs = [
        pl.BlockSpec((None, tm, d), lambda j, bb: (bb, j, 0)),
        pl.BlockSpec((None, 6, d), lambda j, bb: (mod_row_fn(bb), 0, 0)),
        pl.BlockSpec((d, IN_WIDTH), lambda j, bb: (0, 0)),
    ]
    args = [x, modtab, w2]
    if rope:
        in_specs += [pl.BlockSpec((tm, LANES), lambda j, bb: (j, 0))] * 3
        args += list(rope_tabs)
    out_specs = [pl.BlockSpec((None, tm, w), lambda j, bb: (bb, j, 0)) for w in _IN_OUT_WIDTHS]
    out_shape = [jax.ShapeDtypeStruct((b, ls, w), BF16) for w in _IN_OUT_WIDTHS]
    return pl.pallas_call(
        functools.partial(_inproj_kernel, rope=rope),
        grid=(nt, b),
        in_specs=in_specs,
        out_specs=out_specs,
        out_shape=out_shape,
        compiler_params=_cparams(("arbitrary", "arbitrary")),
        name="inproj_rope" if rope else "inproj",
    )(*args)


def _rope_tables(l):
    p = np.arange(l)
    row = (p // GRID_W).astype(np.float64)
    col = (p % GRID_W).astype(np.float64)
    quarter = HEAD_DIM // 4
    inv = ROPE_BASE ** (-np.arange(quarter, dtype=np.float64) / quarter)
    lane = np.arange(LANES)
    j = lane % HEAD_DIM
    pos = np.where((j < HEAD_DIM // 2)[None, :], row[:, None], col[:, None])
    ang = pos * inv[j % quarter][None, :]
    cos, sin = np.cos(ang), np.sin(ang)
    second = ((lane % (2 * quarter)) >= quarter)[None, :]
    s_prev = np.where(second, sin, 0.0)
    s_next = np.where(second, 0.0, -sin)
    return tuple(jnp.asarray(a.astype(np.float32)) for a in (cos, s_prev, s_next))


def _attn_blocks(sink_ref, q_ref, o_ref, blocks):
    nq = ATTN_BLOCK
    lane = lax.broadcasted_iota(jnp.int32, (1, LANES), 1)
    rid = lax.broadcasted_iota(jnp.int32, (4, 1, 1), 0)
    chains = [(row0, kvh, keys[kvh], ok) for row0, keys, ok in blocks for kvh in range(N_KV_HEADS)]

    scores = []
    for row0, kvh, (k_lo, k_hi, _), _ in chains:
        rows = pl.ds(row0, nq)
        qg = jnp.concatenate([q_ref[rows, 256 * kvh:256 * kvh + LANES],
                              q_ref[rows, 256 * kvh + LANES:256 * kvh + 2 * LANES]], axis=0)
        scores.append(jnp.concatenate([_dot_nt(qg, k_lo), _dot_nt(qg, k_hi)], axis=0).reshape(4, nq, -1))

    probs, sink_terms = [], []
    for (row0, kvh, _, ok), s in zip(chains, scores):
        if ok is not None:
            ok_left, ok_right = ok
            s = jnp.concatenate([jnp.where(ok_left[None], s[:, :, 0:nq], NEG_INF), s[:, :, nq:2 * nq],
                                 jnp.where(ok_right[None], s[:, :, 2 * nq:3 * nq], NEG_INF), s[:, :, 3 * nq:]],
                                axis=2)
        h0 = 4 * kvh
        sk = jnp.where(rid == 0, sink_ref[h0],
                       jnp.where(rid == 1, sink_ref[h0 + 2],
                                 jnp.where(rid == 2, sink_ref[h0 + 1], sink_ref[h0 + 3]))) * LOG2E
        m = jnp.maximum(jnp.max(s, axis=2, keepdims=True), sk)
        probs.append(jnp.exp2((s - m).astype(BF16)).reshape(4 * nq, -1))
        sink_terms.append(jnp.exp2(sk - m).reshape(4 * nq, 1))

    outs = [_dot(e, v_ext) for e, (_, _, (_, _, v_ext), _) in zip(probs, chains)]

    for (row0, kvh, _, _), o, es in zip(chains, outs, sink_terms):
        rows = pl.ds(row0, nq)
        swapped = pltpu.roll(o, HEAD_DIM, 1)
        even = o[0:2 * nq] / (swapped[0:2 * nq] + es[0:2 * nq])
        odd = swapped[2 * nq:] / (o[2 * nq:] + es[2 * nq:])
        g0 = jnp.where(lane < HEAD_DIM, even[0:nq], odd[0:nq])
        g1 = jnp.where(lane < HEAD_DIM, even[nq:], odd[nq:])
        o_ref[rows, 256 * kvh:256 * kvh + LANES] = g0.astype(BF16)
        o_ref[rows, 256 * kvh + LANES:256 * kvh + 2 * LANES] = g1.astype(BF16)


def _attn_key_parts(k, v):
    lane = lax.broadcasted_iota(jnp.int32, (1, LANES), 1)
    low = lane < HEAD_DIM
    swap = lambda a: jnp.concatenate([a[:, HEAD_DIM:], a[:, :HEAD_DIM]], axis=1)
    ks, vs = swap(k), swap(v)
    zero, one = jnp.zeros_like(k), jnp.ones_like(v)
    return [(jnp.where(low, k, zero), jnp.where(low, zero, ks), jnp.where(low, v, one)),
            (jnp.where(low, ks, zero), jnp.where(low, zero, k), jnp.where(low, vs, one))]


def _attn_lat_kernel(sink_ref, q_ref, kl_ref, kc_ref, kr_ref, kx_ref,
                     vl_ref, vc_ref, vr_ref, vx_ref, o_ref, *, nsteps, nblk):
    i = pl.program_id(1)
    nq = ATTN_BLOCK
    rows = lax.broadcasted_iota(jnp.int32, (nq, nq), 0)
    cols = lax.broadcasted_iota(jnp.int32, (nq, nq), 1)
    upper = cols >= rows
    lower = cols <= rows
    parts = [_attn_key_parts(kr[...], vr[...])
             for kr, vr in ((kl_ref, vl_ref), (kc_ref, vc_ref), (kr_ref, vr_ref), (kx_ref, vx_ref))]
    per_head = []
    for kvh in range(N_KV_HEADS):
        left, cen, right, ctx = [p[kvh] for p in parts]
        own = [tuple(t[j * nq:(j + 1) * nq] for t in cen) for j in range(nblk)]
        per_head.append(([left] + own + [right], ctx))
    blocks = []
    for j in range(nblk):
        keys = [tuple(jnp.concatenate([p[t] for p in seq[j:j + 3]] + [ctx[t]], axis=0) for t in range(3))
                for seq, ctx in per_head]
        ok_left = upper & (i >= 1) if j == 0 else upper
        ok_right = lower & (i <= nsteps - 2) if j == nblk - 1 else lower
        blocks.append((j * nq, keys, (ok_left, ok_right)))
    _attn_blocks(sink_ref, q_ref, o_ref, blocks)


def _attn_ctx_kernel(sink_ref, q_ref, kx_ref, vx_ref, o_ref):
    _attn_blocks(sink_ref, q_ref, o_ref, [(0, _attn_key_parts(kx_ref[...], vx_ref[...]), None)])


def _attention_latent(sink, q, kd, vd, kdc, vdc):
    b, l, _ = q.shape
    lc = kdc.shape[1]
    nq = ATTN_BLOCK
    nb = l // nq
    nblk = min(ATTN_BLOCKS_PER_STEP, nb)
    assert nb % nblk == 0
    nsteps = nb // nblk
    smem = pl.BlockSpec(memory_space=pltpu.SMEM)
    edge = lambda f: pl.BlockSpec((None, nq, KV_WIDTH), f)
    left = lambda bb, i: (bb, jnp.maximum(nblk * i - 1, 0), 0)
    cen = lambda bb, i: (bb, i, 0)
    right = lambda bb, i: (bb, jnp.minimum(nblk * i + nblk, nb - 1), 0)
    mid = pl.BlockSpec((None, nblk * nq, KV_WIDTH), cen)
    ctx = pl.BlockSpec((None, lc, KV_WIDTH), lambda bb, i: (bb, 0, 0))
    return pl.pallas_call(
        functools.partial(_attn_lat_kernel, nsteps=nsteps, nblk=nblk),
        grid=(b, nsteps),
        in_specs=[smem, pl.BlockSpec((None, nblk * nq, ATTN_WIDTH), cen),
                  edge(left), mid, edge(right), ctx,
                  edge(left), mid, edge(right), ctx],
        out_specs=pl.BlockSpec((None, nblk * nq, ATTN_WIDTH), cen),
        out_shape=jax.ShapeDtypeStruct((b, l, ATTN_WIDTH), BF16),
        compiler_params=_cparams(("arbitrary", "arbitrary")),
        name="attn_latent",
    )(sink, q, kd, kd, kd, kdc, vd, vd, vd, vdc)


def _attention_ctx(sink, qc, kdc, vdc):
    b, lc, _ = qc.shape
    nq = ATTN_BLOCK
    smem = pl.BlockSpec(memory_space=pltpu.SMEM)
    ctx = pl.BlockSpec((None, lc, KV_WIDTH), lambda bb, i: (bb, 0, 0))
    return pl.pallas_call(
        _attn_ctx_kernel,
        grid=(b, lc // nq),
        in_specs=[smem, pl.BlockSpec((None, nq, ATTN_WIDTH), lambda bb, i: (bb, i, 0)), ctx, ctx],
        out_specs=pl.BlockSpec((None, nq, ATTN_WIDTH), lambda bb, i: (bb, i, 0)),
        out_shape=jax.ShapeDtypeStruct((b, lc, ATTN_WIDTH), BF16),
        compiler_params=_cparams(("arbitrary", "arbitrary")),
        name="attn_ctx",
    )(sink, qc, kdc, vdc)


def _pool_band_matrices():
    r = np.arange(POOL_TILE)[:, None]
    a = np.arange(POOL_TILE + 2 * POOL_HALO)[None, :] - POOL_HALO
    mats = [((a >= r - w // 2) & (a < r + w - w // 2)) for w in POOL_WINDOWS]
    return jnp.asarray(np.concatenate(mats, axis=0).astype(np.float32), dtype=BF16)


def _pool_kernel(u_ref, a_ref, w_ref, sc_ref, o_ref, pad_ref, *, ls):
    halo = POOL_HALO
    zeros = jnp.zeros((halo, POOL_WIDTH), BF16)
    pad_ref[0:halo, :] = zeros
    pad_ref[halo + ls:2 * halo + ls, :] = zeros
    pad_ref[halo:halo + ls, :] = u_ref[...]
    lane = lax.broadcasted_iota(jnp.int32, (1, POOL_WIDTH), 1)
    grp = lane // POOL_GROUP
    wl = jnp.where(grp == 0, POOL_WINDOWS[0],
                   jnp.where(grp == 1, POOL_WINDOWS[1],
                             jnp.where(grp == 2, POOL_WINDOWS[2], POOL_WINDOWS[3])))
    half = wl // 2

    ntile = ls // POOL_TILE
    grp_tiles = min(8, ntile)
    assert ntile % grp_tiles == 0
    nw = len(POOL_WINDOWS)

    def body(gi, carry):
        t0s = [pl.multiple_of((gi * grp_tiles + j) * POOL_TILE, POOL_TILE) for j in range(grp_tiles)]
        sums = [_dot(a_ref[...], pad_ref[pl.ds(t0, POOL_TILE + 2 * halo), :]) for t0 in t0s]
        ds = []
        for t0, sm in zip(t0s, sums):
            acc = sm[0:POOL_TILE]
            for g in range(1, nw):
                acc = jnp.where(grp == g, sm[g * POOL_TILE:(g + 1) * POOL_TILE], acc)
            p = t0 + lax.broadcasted_iota(jnp.int32, (POOL_TILE, 1), 0)
            hi = jnp.minimum(p + (wl - half), ls)
            lo = jnp.maximum(p - half, 0)
            cnt = (hi - lo).astype(F32)
            ut = u_ref[pl.ds(t0, POOL_TILE), :].astype(F32)
            ds.append((acc / cnt - ut).astype(BF16))
        rows = pl.ds(t0s[0], grp_tiles * POOL_TILE)
        o_ref[rows, :] = (_dot(jnp.concatenate(ds, axis=0), w_ref[...]) * sc_ref[...]).astype(BF16)
        return carry

    lax.fori_loop(0, ntile // grp_tiles, body, 0)


def _pool(u, band, wblk, scale):
    b, ls, _ = u.shape
    return pl.pallas_call(
        functools.partial(_pool_kernel, ls=ls),
        grid=(b,),
        in_specs=[
            pl.BlockSpec((None, ls, POOL_WIDTH), lambda bb: (bb, 0, 0)),
            pl.BlockSpec(band.shape, lambda bb: (0, 0)),
            pl.BlockSpec((POOL_WIDTH, POOL_WIDTH), lambda bb: (0, 0)),
            pl.BlockSpec((1, POOL_WIDTH), lambda bb: (0, 0)),
        ],
        out_specs=pl.BlockSpec((None, ls, POOL_WIDTH), lambda bb: (bb, 0, 0)),
        out_shape=jax.ShapeDtypeStruct((b, ls, POOL_WIDTH), BF16),
        scratch_shapes=[pltpu.VMEM((ls + 2 * POOL_HALO, POOL_WIDTH), BF16)],
        compiler_params=_cparams(("arbitrary",)),
        name="pool",
    )(u, band, wblk, scale)


def _ret_kernel(lg_ref, rq_ref, rk_ref, rv_ref, rg_ref, s0f_ref, s0b_ref, gmat_ref,
                o_ref, sf_ref, sb_ref,
                kvf_ref, kvb_ref, sp_ref, dm_ref, tab_ref, *, nc):
    c_len = RET_CHUNK
    kw, vw = RET_KW, RET_WIDTH

    def per_head(idx, d):
        return jnp.where(idx == 0, lg_ref[d, 0],
                         jnp.where(idx == 1, lg_ref[d, 1],
                                   jnp.where(idx == 2, lg_ref[d, 2], lg_ref[d, 3])))

    hk = lax.broadcasted_iota(jnp.int32, (1, kw), 1) // RET_DK
    hv = lax.broadcasted_iota(jnp.int32, (1, vw), 1) // RET_DV

    @pl.when(pl.program_id(0) == 0)
    def _():
        n_col = lax.broadcasted_iota(jnp.int32, (c_len, 1), 0).astype(F32)
        lgk_f, lgk_b = per_head(hk, 0), per_head(hk, 1)
        tab_ref[0] = jnp.exp(lgk_f * (c_len - 1.0 - n_col))
        tab_ref[1] = jnp.exp(lgk_b * n_col)
        tab_ref[2] = jnp.exp(lgk_f * (n_col + 1.0))
        tab_ref[3] = jnp.exp(lgk_b * (c_len - n_col))
        hs = lax.broadcasted_iota(jnp.int32, (1, RET_HEADS * c_len), 1) // c_len
        m_idx = (lax.broadcasted_iota(jnp.int32, (c_len, RET_HEADS * c_len), 1) & (c_len - 1)).astype(F32)
        n_idx = lax.broadcasted_iota(jnp.int32, (c_len, RET_HEADS * c_len), 0).astype(F32)
        rel = n_idx - m_idx
        dm_ref[0] = jnp.where(rel >= 0, jnp.exp(per_head(hs, 0) * jnp.maximum(rel, 0.0)), 0.0)
        dm_ref[1] = jnp.where(rel <= 0, jnp.exp(per_head(hs, 1) * jnp.maximum(-rel, 0.0)), 0.0)

    bd = (lax.broadcasted_iota(jnp.int32, (kw, vw), 0) // RET_DK) == (lax.broadcasted_iota(jnp.int32, (kw, vw), 1) // RET_DV)
    hk_col = lax.broadcasted_iota(jnp.int32, (kw, 1), 0) // RET_DK
    cd_f = jnp.exp(per_head(hk_col, 0) * float(c_len))
    cd_b = jnp.exp(per_head(hk_col, 1) * float(c_len))

    def rows(c):
        return pl.ds(pl.multiple_of(c * c_len, c_len), c_len)

    grp = min(8, nc)
    assert nc % grp == 0

    def kv_body(gi, carry):
        cs = [gi * grp + j for j in range(grp)]
        ks = [rk_ref[rows(c), :].astype(F32) for c in cs]
        vs = [rv_ref[rows(c), :] for c in cs]
        kf = [(k * tab_ref[0]).astype(BF16) for k in ks]
        kb = [(k * tab_ref[1]).astype(BF16) for k in ks]
        pf = [_dot_tn(a, v) for a, v in zip(kf, vs)]
        pb = [_dot_tn(a, v) for a, v in zip(kb, vs)]
        for j, c in enumerate(cs):
            kvf_ref[c] = jnp.where(bd, pf[j], 0.0)
            kvb_ref[c] = jnp.where(bd, pb[j], 0.0)
        return carry

    lax.fori_loop(0, nc // grp, kv_body, 0)

    def scan_body(j, carry):
        s_f, s_b = carry
        cb = nc - 1 - j
        sp_ref[j, 0:kw, :] = s_f.astype(BF16)
        sp_ref[cb, kw:2 * kw, :] = s_b.astype(BF16)
        return cd_f * s_f + kvf_ref[j], cd_b * s_b + kvb_ref[cb]

    s_f, s_b = lax.fori_loop(0, nc, scan_body, (s0f_ref[...], s0b_ref[...]))
    sf_ref[...] = s_f
    sb_ref[...] = s_b

    def out_body(gi, carry):
        cs = [gi * grp + j for j in range(grp)]
        qs = [rq_ref[rows(c), :] for c in cs]
        ks = [rk_ref[rows(c), :] for c in cs]
        vs = [rv_ref[rows(c), :] for c in cs]
        zk, zv = jnp.zeros_like(ks[0]), jnp.zeros_like(vs[0])
        ksts = [jnp.concatenate([jnp.where(hk == h, k, zk) for h in range(RET_HEADS)], axis=0) for k in ks]
        vsts = [jnp.concatenate([jnp.where(hv == h, v, zv) for h in range(RET_HEADS)], axis=0) for v in vs]
        scs = [_dot_nt(q, kst) for q, kst in zip(qs, ksts)]
        p2s = [jnp.concatenate([(sc * dm_ref[0]).astype(BF16), (sc * dm_ref[1]).astype(BF16)], axis=0)
               for sc in scs]
        q2s = [jnp.concatenate([(q.astype(F32) * tab_ref[2]).astype(BF16),
                                (q.astype(F32) * tab_ref[3]).astype(BF16)], axis=1) for q in qs]
        o2s = [_dot(p2, vst) for p2, vst in zip(p2s, vsts)]
        ocs = [_dot(q2, sp_ref[c]) for q2, c in zip(q2s, cs)]
        o = jnp.concatenate([o2[0:c_len] + o2[c_len:] + oc for o2, oc in zip(o2s, ocs)], axis=0)
        mu = _dot(o.astype(BF16), gmat_ref[...])
        d = o - mu
        var = _dot((d * d).astype(BF16), gmat_ref[...])
        hn = d * lax.rsqrt(var + LN_EPS)
        grows = pl.ds(pl.multiple_of(gi * (grp * c_len), grp * c_len), grp * c_len)
        g = rg_ref[grows, :].astype(F32)
        o_ref[grows, :] = (jax.nn.silu(g) * hn).astype(BF16)
        return carry

    lax.fori_loop(0, nc // grp, out_body, 0)


def _retention(lg, rq, rk, rv, rg, s0f, s0b, gmat):
    b, ls, _ = rq.shape
    nc = ls // RET_CHUNK
    seq = lambda w: pl.BlockSpec((None, ls, w), lambda bb: (bb, 0, 0))
    st = pl.BlockSpec((None, RET_KW, RET_WIDTH), lambda bb: (bb, 0, 0))
    return pl.pallas_call(
        functools.partial(_ret_kernel, nc=nc),
        grid=(b,),
        in_specs=[pl.BlockSpec(memory_space=pltpu.SMEM), seq(RET_KW), seq(RET_KW), seq(RET_WIDTH), seq(RET_WIDTH),
                  st, st, pl.BlockSpec((RET_WIDTH, RET_WIDTH), lambda bb: (0, 0))],
        out_specs=[seq(RET_WIDTH), st, st],
        out_shape=[jax.ShapeDtypeStruct((b, ls, RET_WIDTH), BF16),
                   jax.ShapeDtypeStruct((b, RET_KW, RET_WIDTH), F32),
                   jax.ShapeDtypeStruct((b, RET_KW, RET_WIDTH), F32)],
        scratch_shapes=[
            pltpu.VMEM((nc, RET_KW, RET_WIDTH), F32),
            pltpu.VMEM((nc, RET_KW, RET_WIDTH), F32),
            pltpu.VMEM((nc, 2 * RET_KW, RET_WIDTH), BF16),
            pltpu.VMEM((2, RET_CHUNK, RET_HEADS * RET_CHUNK), F32),
            pltpu.VMEM((4, RET_CHUNK, RET_KW), F32),
        ],
        compiler_params=_cparams(("arbitrary",)),
        name="retention",
    )(lg, rq, rk, rv, rg, s0f, s0b, gmat)


def _mixer_out_ln1(attn_ref, pool_ref, ret_ref, x_ref, m, w_ref, g_ref, b_ref, alpha):
    y = (_dot(attn_ref[...], w_ref[0:ATTN_WIDTH, :])
         + _dot(pool_ref[...], w_ref[ATTN_WIDTH:ATTN_WIDTH + POOL_WIDTH, :])
         + _dot(ret_ref[...], w_ref[ATTN_WIDTH + POOL_WIDTH:, :]))
    x1 = _layer_norm(alpha * x_ref[...] + m[2:3, :] * y, g_ref[...], b_ref[...])
    return x1, x1 * (1.0 + m[4:5, :]) + m[3:4, :]


def _route_tile(logits_t, tri_ref, carry_ref):
    ne, tm = logits_t.shape
    row = lax.broadcasted_iota(jnp.int32, (ne, tm), 0)
    l = jnp.where(row < N_EXPERTS, logits_t, -jnp.inf)
    m1 = jnp.max(l, axis=0, keepdims=True)
    i1 = jnp.min(jnp.where(l == m1, row, ne), axis=0, keepdims=True)
    l2 = jnp.where(row == i1, -jnp.inf, l)
    m2 = jnp.max(l2, axis=0, keepdims=True)
    i2 = jnp.min(jnp.where(l2 == m2, row, ne), axis=0, keepdims=True)
    e = jnp.exp(m2 - m1)
    w1 = 1.0 / (1.0 + e)
    w2 = e / (1.0 + e)
    oh = jnp.where((row == i1) | (row == i2), 1.0, 0.0)
    carry = carry_ref[:, 0:1]
    cum = _dot(oh.astype(BF16), tri_ref[...]) + carry
    r1 = jnp.sum(jnp.where(row == i1, cum, 0.0), axis=0, keepdims=True)
    r2 = jnp.sum(jnp.where(row == i2, cum, 0.0), axis=0, keepdims=True)
    carry_ref[...] = jnp.broadcast_to(carry + jnp.sum(oh, axis=1, keepdims=True), carry_ref.shape)
    r8 = lax.broadcasted_iota(jnp.int32, (8, tm), 0)
    return jnp.where(r8 == 0, i1.astype(F32),
                     jnp.where(r8 == 1, i2.astype(F32),
                               jnp.where(r8 == 2, r1,
                                         jnp.where(r8 == 3, r2,
                                                   jnp.where(r8 == 4, w1,
                                                             jnp.where(r8 == 5, w2, 0.0))))))


def _outproj_kernel(attn_ref, pool_ref, ret_ref, x_ref, mod_ref, w_ref, g_ref, b_ref, router_ref, tri_ref,
                    x1_ref, t_ref, tab_ref, tabt_ref, cnt_ref, carry_ref, *, alpha):
    @pl.when((pl.program_id(0) == 0) & (pl.program_id(1) == 0))
    def _():
        carry_ref[...] = jnp.zeros_like(carry_ref)

    x1, t = _mixer_out_ln1(attn_ref, pool_ref, ret_ref, x_ref, mod_ref[...], w_ref, g_ref, b_ref, alpha)
    x1_ref[...] = x1
    _store_tile_rows(t_ref, (), t)
    tab_t = _route_tile(_dot_nt(router_ref[...], t.astype(BF16)), tri_ref, carry_ref)
    tabt_ref[...] = tab_t
    tm = tab_t.shape[1]
    tab_ref[...] = jnp.concatenate([tab_t, jnp.zeros((LANES - 8, tm), F32)], axis=0).T
    cnt_ref[...] = carry_ref[0:8, :]


def _mixer_row_specs(tm, d):
    row = lambda w: pl.BlockSpec((None, tm, w), lambda bb, j: (bb, j, 0))
    return [row(ATTN_WIDTH), row(POOL_WIDTH), row(RET_WIDTH), row(d)]


def _outproj_ln_route(attn, pool, ret, x, modtab, mod_row_fn, w_out, ln_g, ln_b, alpha, router):
    b, ls, d = x.shape
    tm = min(ROUTE_TILE, ls)
    nt = ls // tm
    tri = jnp.asarray(np.triu(np.ones((tm, tm), np.float32), 1), dtype=BF16)
    row = lambda w: pl.BlockSpec((None, tm, w), lambda bb, j: (bb, j, 0))
    const = lambda shape: pl.BlockSpec(shape, lambda bb, j: tuple(0 for _ in shape))
    return pl.pallas_call(
        functools.partial(_outproj_kernel, alpha=alpha),
        grid=(b, nt),
        in_specs=_mixer_row_specs(tm, d) + [
            pl.BlockSpec((None, 6, d), lambda bb, j: (mod_row_fn(bb), 0, 0)),
            const((d, d)), const((1, d)), const((1, d)), const((ROUTER_ROWS, d)), const((tm, tm))],
        out_specs=[row(d), pl.BlockSpec((None, tm * SUBLANES, LANES), lambda bb, j: (bb, j, 0)), row(LANES),
                   pl.BlockSpec((8, tm), lambda bb, j: (0, bb * nt + j)), const((8, LANES))],
        out_shape=[jax.ShapeDtypeStruct((b, ls, d), F32),
                   jax.ShapeDtypeStruct((b, ls * SUBLANES, LANES), F32),
                   jax.ShapeDtypeStruct((b, ls, LANES), F32),
                   jax.ShapeDtypeStruct((8, b * ls), F32),
                   jax.ShapeDtypeStruct((8, LANES), F32)],
        scratch_shapes=[pltpu.VMEM((ROUTER_ROWS, LANES), F32)],
        compiler_params=_cparams(("arbitrary", "arbitrary")),
        name="outproj_ln_route",
    )(attn, pool, ret, x, modtab, w_out, ln_g, ln_b, router, tri)


N_FFN_CHUNKS = FFN_DIM // FFN_CHUNK
N_FFN_GAPS = 3 * N_FFN_CHUNKS


def _swiglu_rows(tb, wg_ref, wu_ref, wd_ref, between=None):
    gap = (lambda j: None) if between is None else between
    acts = []
    for c in range(N_FFN_CHUNKS):
        sl = slice(c * FFN_CHUNK, (c + 1) * FFN_CHUNK)
        g = _dot(tb, wg_ref[:, sl])
        gap(3 * c)
        u = _dot(tb, wu_ref[:, sl])
        gap(3 * c + 1)
        acts.append((jax.nn.silu(g) * u).astype(BF16))
        gap(3 * c + 2)
    return _dot(jnp.concatenate(acts, axis=1), wd_ref[...])


def _dense_post_kernel(attn_ref, pool_ref, ret_ref, x_ref, mod_ref, w_ref, g1_ref, b1_ref,
                       wg_ref, wu_ref, wd_ref, g2_ref, b2_ref, o_ref, *, alpha):
    m = mod_ref[...]
    x1, t = _mixer_out_ln1(attn_ref, pool_ref, ret_ref, x_ref, m, w_ref, g1_ref, b1_ref, alpha)
    f = _swiglu_rows(t.astype(BF16), wg_ref, wu_ref, wd_ref)
    o_ref[...] = _layer_norm(alpha * x1 + m[5:6, :] * f, g2_ref[...], b2_ref[...])


def _dense_post(attn, pool, ret, x, modtab, mod_row_fn, w_out, g1, b1, wg, wu, wd, g2, b2, alpha):
    b, ls, d = x.shape
    tm = min(512, ls)
    nt = ls // tm
    const = lambda shape: pl.BlockSpec(shape, lambda bb, j: tuple(0 for _ in shape),
                                       pipeline_mode=pl.Buffered(1))
    return pl.pallas_call(
        functools.partial(_dense_post_kernel, alpha=alpha),
        grid=(b, nt),
        in_specs=_mixer_row_specs(tm, d) + [
            pl.BlockSpec((None, 6, d), lambda bb, j: (mod_row_fn(bb), 0, 0)),
            const((d, d)), const((1, d)), const((1, d)),
            const((d, FFN_DIM)), const((d, FFN_DIM)), const((FFN_DIM, d)),
            const((1, d)), const((1, d))],
        out_specs=pl.BlockSpec((None, tm, d), lambda bb, j: (bb, j, 0)),
        out_shape=jax.ShapeDtypeStruct((b, ls, d), F32),
        compiler_params=_cparams(("arbitrary", "arbitrary")),
        name="dense_post",
    )(attn, pool, ret, x, modtab, w_out, g1, b1, wg, wu, wd, g2, b2)


def _invert_kernel(p1_ref, p2_ref, lo_ref, hi_ref, dst_ref, *, n):
    for e in range(N_EXPERTS + 1):
        def fill(s, carry):
            dst_ref[s] = 2 * n + (s & (MOE_TILE - 1))
            return carry

        lax.fori_loop(lo_ref[e], hi_ref[e], fill, 0)

    def body(t, carry):
        dst_ref[p1_ref[t]] = t
        dst_ref[p2_ref[t]] = n + t
        return carry

    lax.fori_loop(0, n, body, 0, unroll=16)


def _invert(pos1, pos2, fill_lo, fill_hi, nslots):
    n = pos1.shape[0]
    smem = pl.BlockSpec(memory_space=pltpu.SMEM)
    return pl.pallas_call(
        functools.partial(_invert_kernel, n=n),
        in_specs=[smem, smem, smem, smem],
        out_specs=smem,
        out_shape=jax.ShapeDtypeStruct((nslots,), jnp.int32),
        name="invert",
    )(pos1, pos2, fill_lo, fill_hi)


def _slot_source_row(v, n):
    if n & (n - 1) == 0:
        return v & (n - 1)
    return jnp.where(v >= 2 * n, v - 2 * n, jnp.where(v >= n, v - n, v))


def _moe_kernel(te_ref, nt_ref, inv_ref, t_ref, wg_ref, wu_ref, wd_ref, y_ref, xbuf, obuf, gsem, ssem, *, n):
    del te_ref
    i = pl.program_id(0)
    nt = nt_ref[0]
    tm = MOE_TILE
    slot = i & 1
    other = 1 - slot

    def row_tile(r):
        return pl.ds(pl.multiple_of(r * SUBLANES, SUBLANES), SUBLANES)

    def gather_row(tile, r, buf):
        src = _slot_source_row(inv_ref[tile * tm + r], n)
        pltpu.make_async_copy(t_ref.at[row_tile(src)], xbuf.at[buf, row_tile(r)], gsem.at[buf]).start()

    def scatter_row(tile, r, buf, real):
        dst = jnp.where(real, inv_ref[tile * tm + r], 2 * n + r)
        pltpu.make_async_copy(obuf.at[buf, row_tile(r)], y_ref.at[row_tile(dst)], ssem.at[buf]).start()

    def wait_gather(buf):
        pltpu.make_async_copy(t_ref.at[pl.ds(0, tm * SUBLANES)], xbuf.at[buf], gsem.at[buf]).wait()

    def wait_scatter(buf):
        pltpu.make_async_copy(obuf.at[buf], y_ref.at[pl.ds(0, tm * SUBLANES)], ssem.at[buf]).wait()

    @pl.when(i == 0)
    def _():
        obuf[1] = jnp.zeros(obuf.shape[1:], obuf.dtype)

        def first(r, carry):
            gather_row(0, r, 0)
            return carry

        lax.fori_loop(0, tm, first, 0, unroll=8)

    def main(cur, oth):
        wait_gather(cur)
        nxt = jnp.minimum(i + 1, nt - 1)
        prv = jnp.maximum(i - 1, 0)
        real = i >= 1
        n_gather_gaps = N_FFN_GAPS // 2
        per_g = -(-tm // n_gather_gaps)
        per_s = -(-tm // (N_FFN_GAPS - n_gather_gaps))

        def between(j):
            if j < n_gather_gaps:
                for r in range(j * per_g, min((j + 1) * per_g, tm)):
                    gather_row(nxt, r, oth)
            else:
                k = j - n_gather_gaps
                for r in range(k * per_s, min((k + 1) * per_s, tm)):
                    scatter_row(prv, r, oth, real)

        x = _load_tile_rows(xbuf, (cur,), tm).astype(BF16)
        f = _swiglu_rows(x, wg_ref, wu_ref, wd_ref, between)

        @pl.when(i >= 1)
        def _():
            wait_scatter(cur)

        _store_tile_rows(obuf, (cur,), f)

    @pl.when((i < nt) & (slot == 0))
    def _():
        main(0, 1)

    @pl.when((i < nt) & (slot == 1))
    def _():
        main(1, 0)

    @pl.when(i == nt)
    def _():
        last = nt - 1
        buf = last & 1
        wait_scatter(nt & 1)

        def tail(r, carry):
            scatter_row(last, r, buf, True)
            return carry

        lax.fori_loop(0, tm, tail, 0, unroll=8)
        wait_scatter(buf)
        wait_gather(nt & 1)


def _moe_ffn(tile_expert, ntiles, inv, t, wg, wu, wd):
    n = t.shape[0] // SUBLANES
    d = D_MODEL
    tm = MOE_TILE
    assert n >= tm
    nt_max = inv.shape[0] // tm
    wspec = lambda shape: pl.BlockSpec((None,) + shape, lambda i, te, nt, iv: (te[i], 0, 0))
    return pl.pallas_call(
        functools.partial(_moe_kernel, n=n),
        grid_spec=pltpu.PrefetchScalarGridSpec(
            num_scalar_prefetch=3,
            grid=(nt_max,),
            in_specs=[pl.BlockSpec(memory_space=pl.ANY),
                      wspec((d, FFN_DIM)), wspec((d, FFN_DIM)), wspec((FFN_DIM, d))],
            out_specs=pl.BlockSpec(memory_space=pl.ANY),
            scratch_shapes=[pltpu.VMEM((2, tm * SUBLANES, LANES), F32), pltpu.VMEM((2, tm * SUBLANES, LANES), F32),
                            pltpu.SemaphoreType.DMA((2,)), pltpu.SemaphoreType.DMA((2,))],
        ),
        out_shape=jax.ShapeDtypeStruct(((2 * n + tm) * SUBLANES, LANES), F32),
        compiler_params=_cparams(("arbitrary",)),
        name="moe_ffn",
    )(tile_expert, ntiles, inv, t, wg, wu, wd)


def _combine_kernel(y1_ref, y2_ref, x1_ref, rt_ref, mod_ref, g_ref, b_ref, o_ref, *, alpha):
    rt = rt_ref[...]
    tm = x1_ref.shape[0]
    f = rt[:, 4:5] * _load_tile_rows(y1_ref, (), tm) + rt[:, 5:6] * _load_tile_rows(y2_ref, (), tm)
    m = mod_ref[...]
    o_ref[...] = _layer_norm(alpha * x1_ref[...] + m[5:6, :] * f, g_ref[...], b_ref[...])


def _combine_ln(y, x1, rtab, modtab, l, ln_g, ln_b, alpha):
    n, d = x1.shape
    tm = min(ROW_TILE, n)
    per_b = l // tm
    nblk = n // tm
    const = lambda shape: pl.BlockSpec(shape, lambda i: tuple(0 for _ in shape))
    return pl.pallas_call(
        functools.partial(_combine_kernel, alpha=alpha),
        grid=(nblk,),
        in_specs=[pl.BlockSpec((tm * SUBLANES, LANES), lambda i: (i, 0)),
                  pl.BlockSpec((tm * SUBLANES, LANES), lambda i: (nblk + i, 0)),
                  pl.BlockSpec((tm, d), lambda i: (i, 0)),
                  pl.BlockSpec((tm, LANES), lambda i: (i, 0)),
                  pl.BlockSpec((None, 6, d), lambda i: (i // per_b, 0, 0)),
                  const((1, d)), const((1, d))],
        out_specs=pl.BlockSpec((tm, d), lambda i: (i, 0)),
        out_shape=jax.ShapeDtypeStruct((n, d), F32),
        compiler_params=_cparams(("arbitrary",)),
        name="combine_ln",
    )(y, y, x1, rtab, modtab, ln_g, ln_b)


def _moe_layer(t, rtab, rtab_t, cnt, x1, modtab, l, wg, wu, wd, ln_g, ln_b, alpha):
    n = x1.shape[0]
    counts = cnt[:N_EXPERTS, 0].astype(jnp.int32)
    tiles_e = (counts + MOE_TILE - 1) // MOE_TILE
    tile_end = jnp.cumsum(tiles_e)
    tile_start = tile_end - tiles_e
    ntiles = tile_end[-1:]
    nt_max = (2 * n) // MOE_TILE + N_EXPERTS + 1
    tile_expert = jnp.minimum(
        jnp.sum((jnp.arange(nt_max)[:, None] >= tile_end[None, :]).astype(jnp.int32), axis=1),
        N_EXPERTS - 1).astype(jnp.int32)
    e1 = rtab_t[0].astype(jnp.int32)
    e2 = rtab_t[1].astype(jnp.int32)
    row_start = tile_start * MOE_TILE
    pos1 = row_start[e1] + rtab_t[2].astype(jnp.int32)
    pos2 = row_start[e2] + rtab_t[3].astype(jnp.int32)
    fill_lo = jnp.concatenate([row_start + counts, tile_end[-1:] * MOE_TILE]).astype(jnp.int32)
    fill_hi = jnp.concatenate([tile_end * MOE_TILE, jnp.full((1,), nt_max * MOE_TILE)]).astype(jnp.int32)
    inv = _invert(pos1, pos2, fill_lo, fill_hi, nt_max * MOE_TILE)
    y = _moe_ffn(tile_expert, ntiles.astype(jnp.int32), inv, t, wg, wu, wd)
    return _combine_ln(y, x1, rtab, modtab, l, ln_g, ln_b, alpha)


def _prep_w_in(w):
    q = w[:, 0:ATTN_WIDTH] * (HEAD_DIM ** -0.5 * LOG2E)
    return jnp.concatenate([q, w[:, ATTN_WIDTH:]], axis=1).astype(BF16)


def _block_diag(mats):
    n = len(mats)
    rows = []
    for i, m in enumerate(mats):
        rows.append(jnp.concatenate([m if j == i else jnp.zeros_like(m) for j in range(n)], axis=1))
    return jnp.concatenate(rows, axis=0)


def kernel(x, c, ctx, c_ctx, w_mod, b_mod, w_in, attn_sink, pool_w, pool_scale, ret_log_decay_fwd,
           ret_log_decay_bwd, w_out, ln1_g, ln1_b, ln2_g, ln2_b, ffn_w_gate, ffn_w_up, ffn_w_down,
           moe_router, moe_w_gate, moe_w_up, moe_w_down):
    b, l, d = x.shape
    lc = ctx.shape[1]
    depth = w_in.shape[0]
    alpha = (2.0 * depth) ** 0.25
    assert d == D_MODEL and l % 256 == 0 and lc % 256 == 0 and l % GRID_W == 0

    n_rows = ((b + 1 + 7) // 8) * 8
    c_all = jnp.concatenate([c, c_ctx[None, :], jnp.zeros((n_rows - b - 1, d), F32)], axis=0)
    mod_all = _modulation(c_all, w_mod, b_mod).reshape(depth, n_rows, 6, d)
    lat_row = lambda bb: bb
    ctx_row = lambda bb: b

    rope_tabs = _rope_tables(l)
    band = _pool_band_matrices()
    gmat = jnp.asarray(np.kron(np.eye(RET_HEADS), np.ones((RET_DV, RET_DV))) / RET_DV, dtype=BF16)
    zero_state = jnp.zeros((b, RET_KW, RET_WIDTH), F32)

    xc = ctx
    for i in range(depth):
        last = i == depth - 1
        modtab = mod_all[i]
        w2 = _prep_w_in(w_in[i])
        wo = w_out[i].astype(BF16)
        sink = attn_sink[i].astype(F32)
        pw = _block_diag([pool_w[i, g] for g in range(len(POOL_WINDOWS))]).astype(BF16)
        ps = pool_scale[i].reshape(1, POOL_WIDTH).astype(F32)
        lg = jnp.stack([ret_log_decay_fwd[i], ret_log_decay_bwd[i]]).astype(F32)
        g1, b1 = ln1_g[i].reshape(1, d), ln1_b[i].reshape(1, d)
        g2, b2 = ln2_g[i].reshape(1, d), ln2_b[i].reshape(1, d)

        qc, kdc, vdc, uc, rqc, rkc, rvc, rgc = _inproj(xc, modtab, ctx_row, w2, None)
        ret_c, s_f, s_b = _retention(lg, rqc, rkc, rvc, rgc, zero_state, zero_state, gmat)

        q, kd, vd, u, rq, rk, rv, rg = _inproj(x, modtab, lat_row, w2, rope_tabs)
        attn = _attention_latent(sink, q, kd, vd, kdc, vdc)
        pool = _pool(u, band, pw, ps)
        ret, _, _ = _retention(lg, rq, rk, rv, rg, s_f, s_b, gmat)

        j = i // 2
        if i % 2 == 0:
            wg, wu, wd = ffn_w_gate[j].astype(BF16), ffn_w_up[j].astype(BF16), ffn_w_down[j].astype(BF16)
            x = _dense_post(attn, pool, ret, x, modtab, lat_row, wo, g1, b1, wg, wu, wd, g2, b2, alpha)
        else:
            wg, wu, wd = moe_w_gate[j].astype(BF16), moe_w_up[j].astype(BF16), moe_w_down[j].astype(BF16)
            router = jnp.pad(moe_router[j].T, ((0, ROUTER_ROWS - N_EXPERTS), (0, 0))).astype(BF16)
            x1, t, rtab, rtab_t, cnt = _outproj_ln_route(attn, pool, ret, x, modtab, lat_row, wo, g1, b1, alpha,
                                                         router)
            x = _moe_layer(t.reshape(b * l * SUBLANES, LANES), rtab.reshape(b * l, LANES), rtab_t, cnt,
                           x1.reshape(b * l, d), modtab, l, wg, wu, wd, g2, b2, alpha).reshape(b, l, d)

        if not last:
            attn_c = _attention_ctx(sink, qc, kdc, vdc)
            pool_c = _pool(uc, band, pw, ps)
            if i % 2 == 0:
                xc = _dense_post(attn_c, pool_c, ret_c, xc, modtab, ctx_row, wo, g1, b1, wg, wu, wd, g2, b2,
                                 alpha)
            else:
                x1c, tc, rtab_c, rtab_tc, cnt_c = _outproj_ln_route(attn_c, pool_c, ret_c, xc, modtab, ctx_row, wo,
                                                                    g1, b1, alpha, router)
                ctx_mod = jnp.broadcast_to(modtab[b:b + 1], (b, 6, d))
                xc = _moe_layer(tc.reshape(b * lc * SUBLANES, LANES), rtab_c.reshape(b * lc, LANES), rtab_tc,
                                cnt_c, x1c.reshape(b * lc, d), ctx_mod, lc, wg, wu, wd, g2, b2,
                                alpha).reshape(b, lc, d)
    return x
```

```python
import functools
import math

import numpy as np
import jax
import jax.numpy as jnp
from jax import lax
from jax.experimental import pallas as pl
from jax.experimental.pallas import tpu as pltpu

F32 = jnp.float32
BF16 = jnp.bfloat16

D_MODEL = 1024
GRID_W = 64
HEAD_DIM = 64
N_HEADS = 8
N_KV_HEADS = 2
ATTN_WIDTH = N_HEADS * HEAD_DIM
ATTN_BLOCK = 128
ATTN_BLOCKS_PER_STEP = 8
ROPE_BASE = 10000.0
POOL_WINDOWS = (2, 4, 8, 16)
POOL_WIDTH = 256
POOL_GROUP = 64
POOL_TILE = 256
POOL_HALO = 16
RET_HEADS = 4
RET_DK = 32
RET_DV = 64
RET_WIDTH = 256
RET_KW = RET_HEADS * RET_DK
RET_CHUNK = 128
FFN_DIM = 2816
FFN_CHUNK = 256
N_EXPERTS = 8
ROUTER_ROWS = 16
ROUTE_TILE = 512
MOE_TILE = 512
ROW_TILE = 512
LN_EPS = 1e-5
NEG_INF = -1e30
LOG2E = math.log2(math.e)
LANES = 128
IN_WIDTH = 1792

VMEM_LIMIT = 56 * 1024 * 1024


def _cparams(sem):
    return pltpu.CompilerParams(dimension_semantics=sem, vmem_limit_bytes=VMEM_LIMIT)


def _dot(a, b):
    return jnp.dot(a, b, preferred_element_type=F32)


def _dot_nt(a, b):
    return lax.dot_general(a, b, (((1,), (1,)), ((), ())), preferred_element_type=F32)


def _dot_tn(a, b):
    return lax.dot_general(a, b, (((0,), (0,)), ((), ())), preferred_element_type=F32)


def _split_bf16(a):
    hi = a.astype(BF16)
    lo = (a - hi.astype(F32)).astype(BF16)
    return hi, lo


def _dot3(a, b):
    ah, al = _split_bf16(a)
    bh, bl = _split_bf16(b)
    return _dot(ah, bh) + _dot(ah, bl) + _dot(al, bh)


SUBLANES = 8
ROW_TILES = D_MODEL // LANES


def _store_tile_rows(ref, idx, val):
    rows = val.shape[0]
    for c in range(ROW_TILES):
        ref[idx + (pl.ds(c, rows, stride=SUBLANES), slice(None))] = val[:, c * LANES:(c + 1) * LANES]


def _load_tile_rows(ref, idx, rows):
    return jnp.concatenate([ref[idx + (pl.ds(c, rows, stride=SUBLANES), slice(None))]
                            for c in range(ROW_TILES)], axis=1)


def _layer_norm(z, g, b):
    mu = jnp.mean(z, axis=-1, keepdims=True)
    d = z - mu
    var = jnp.mean(d * d, axis=-1, keepdims=True)
    return d * lax.rsqrt(var + LN_EPS) * g + b


def _mod_kernel(c_ref, w_ref, b_ref, o_ref):
    s = jax.nn.silu(c_ref[...])
    o_ref[...] = _dot3(s, w_ref[...]) + b_ref[...]


def _modulation(c_all, w_mod, b_mod):
    depth, d, n6 = w_mod.shape
    r = c_all.shape[0]
    bn = 1536
    return pl.pallas_call(
        _mod_kernel,
        grid=(depth, n6 // bn),
        in_specs=[
            pl.BlockSpec((r, d), lambda i, j: (0, 0)),
            pl.BlockSpec((None, d, bn), lambda i, j: (i, 0, j)),
            pl.BlockSpec((None, 1, bn), lambda i, j: (i, 0, j)),
        ],
        out_specs=pl.BlockSpec((None, r, bn), lambda i, j: (i, 0, j)),
        out_shape=jax.ShapeDtypeStruct((depth, r, n6), F32),
        compiler_params=_cparams(("arbitrary", "arbitrary")),
        name="modulation",
    )(c_all, w_mod, b_mod.reshape(depth, 1, n6))


KV_WIDTH = N_KV_HEADS * HEAD_DIM
_IN_OUT_WIDTHS = (ATTN_WIDTH, KV_WIDTH, KV_WIDTH, POOL_WIDTH, RET_KW, RET_KW, RET_WIDTH, RET_WIDTH)
_IN_COL_STARTS = tuple(int(c) for c in np.cumsum((0,) + _IN_OUT_WIDTHS))


def _rope_groups(a, cos, s_prev, s_next):
    outs = []
    for g in range(a.shape[1] // LANES):
        ag = a[:, g * LANES:(g + 1) * LANES]
        outs.append(ag * cos + pltpu.roll(ag, 16, 1) * s_prev + pltpu.roll(ag, LANES - 16, 1) * s_next)
    return outs[0] if len(outs) == 1 else jnp.concatenate(outs, axis=1)


def _inproj_kernel(*refs, rope):
    if rope:
        x_ref, mod_ref, w_ref, cos_ref, sp_ref, sn_ref = refs[:6]
        outs = refs[6:]
    else:
        x_ref, mod_ref, w_ref = refs[:3]
        outs = refs[3:]
    q_ref, k_ref, v_ref, u_ref, rq_ref, rk_ref, rv_ref, rg_ref = outs
    m = mod_ref[...]
    h = (x_ref[...] * (1.0 + m[1:2, :]) + m[0:1, :]).astype(BF16)

    def mm(j0, j1):
        return _dot(h, w_ref[:, _IN_COL_STARTS[j0]:_IN_COL_STARTS[j1]])

    q = mm(0, 1)
    kv = mm(1, 3)
    k, v = kv[:, :KV_WIDTH], kv[:, KV_WIDTH:]
    if rope:
        cos, sp, sn = cos_ref[...], sp_ref[...], sn_ref[...]
        q = _rope_groups(q, cos, sp, sn)
        k = _rope_groups(k, cos, sp, sn)
    q_ref[...] = q.astype(BF16)
    k_ref[...] = k.astype(BF16)
    v_ref[...] = v.astype(BF16)
    u_ref[...] = mm(3, 4).astype(BF16)
    rqk = mm(4, 6)
    rq_ref[...] = rqk[:, :RET_KW].astype(BF16)
    rk_ref[...] = (rqk[:, RET_KW:] * (RET_DK ** -0.5)).astype(BF16)
    rv_ref[...] = mm(6, 7).astype(BF16)
    rg_ref[...] = mm(7, 8).astype(BF16)


def _inproj(x, modtab, mod_row_fn, w2, rope_tabs):
    b, ls, d = x.shape
    tm = min(1024, ls)
    nt = ls // tm
    rope = rope_tabs is not None
    in_specs = [
        pl.BlockSpec((None, tm, d), lambda j, bb: (bb, j, 0)),
        pl.BlockSpec((None, 6, d), lambda j, bb: (mod_row_fn(bb), 0, 0)),
        pl.BlockSpec((d, IN_WIDTH), lambda j, bb: (0, 0)),
    ]
    args = [x, modtab, w2]
    if rope:
        in_specs += [pl.BlockSpec((tm, LANES), lambda j, bb: (j, 0))] * 3
        args += list(rope_tabs)
    out_specs = [pl.BlockSpec((None, tm, w), lambda j, bb: (bb, j, 0)) for w in _IN_OUT_WIDTHS]
    out_shape = [jax.ShapeDtypeStruct((b, ls, w), BF16) for w in _IN_OUT_WIDTHS]
    return pl.pallas_call(
        functools.partial(_inproj_kernel, rope=rope),
        grid=(nt, b),
        in_specs=in_specs,
        out_specs=out_specs,
        out_shape=out_shape,
        compiler_params=_cparams(("arbitrary", "arbitrary")),
        name="inproj_rope" if rope else "inproj",
    )(*args)


def _rope_tables(l):
    p = np.arange(l)
    row = (p // GRID_W).astype(np.float64)
    col = (p % GRID_W).astype(np.float64)
    quarter = HEAD_DIM // 4
    inv = ROPE_BASE ** (-np.arange(quarter, dtype=np.float64) / quarter)
    lane = np.arange(LANES)
    j = lane % HEAD_DIM
    pos = np.where((j < HEAD_DIM // 2)[None, :], row[:, None], col[:, None])
    ang = pos * inv[j % quarter][None, :]
    cos, sin = np.cos(ang), np.sin(ang)
    second = ((lane % (2 * quarter)) >= quarter)[None, :]
    s_prev = np.where(second, sin, 0.0)
    s_next = np.where(second, 0.0, -sin)
    return tuple(jnp.asarray(a.astype(np.float32)) for a in (cos, s_prev, s_next))


def _attn_blocks(sink_ref, q_ref, o_ref, blocks):
    nq = ATTN_BLOCK
    lane = lax.broadcasted_iota(jnp.int32, (1, LANES), 1)
    rid = lax.broadcasted_iota(jnp.int32, (4, 1, 1), 0)
    chains = [(row0, kvh, keys[kvh], ok) for row0, keys, ok in blocks for kvh in range(N_KV_HEADS)]

    scores = []
    for row0, kvh, (k_lo, k_hi, _), _ in chains:
        rows = pl.ds(row0, nq)
        qg = jnp.concatenate([q_ref[rows, 256 * kvh:256 * kvh + LANES],
                              q_ref[rows, 256 * kvh + LANES:256 * kvh + 2 * LANES]], axis=0)
        scores.append(jnp.concatenate([_dot_nt(qg, k_lo), _dot_nt(qg, k_hi)], axis=0).reshape(4, nq, -1))

    probs, sink_terms = [], []
    for (row0, kvh, _, ok), s in zip(chains, scores):
        if ok is not None:
            ok_left, ok_right = ok
            s = jnp.concatenate([jnp.where(ok_left[None], s[:, :, 0:nq], NEG_INF), s[:, :, nq:2 * nq],
                                 jnp.where(ok_right[None], s[:, :, 2 * nq:3 * nq], NEG_INF), s[:, :, 3 * nq:]],
                                axis=2)
        h0 = 4 * kvh
        sk = jnp.where(rid == 0, sink_ref[h0],
                       jnp.where(rid == 1, sink_ref[h0 + 2],
                                 jnp.where(rid == 2, sink_ref[h0 + 1], sink_ref[h0 + 3]))) * LOG2E
        m = jnp.maximum(jnp.max(s, axis=2, keepdims=True), sk)
        probs.append(jnp.exp2((s - m).astype(BF16)).reshape(4 * nq, -1))
        sink_terms.append(jnp.exp2(sk - m).reshape(4 * nq, 1))

    outs = [_dot(e, v_ext) for e, (_, _, (_, _, v_ext), _) in zip(probs, chains)]

    for (row0, kvh, _, _), o, es in zip(chains, outs, sink_terms):
        rows = pl.ds(row0, nq)
        swapped = pltpu.roll(o, HEAD_DIM, 1)
        even = o[0:2 * nq] / (swapped[0:2 * nq] + es[0:2 * nq])
        odd = swapped[2 * nq:] / (o[2 * nq:] + es[2 * nq:])
        g0 = jnp.where(lane < HEAD_DIM, even[0:nq], odd[0:nq])
        g1 = jnp.where(lane < HEAD_DIM, even[nq:], odd[nq:])
        o_ref[rows, 256 * kvh:256 * kvh + LANES] = g0.astype(BF16)
        o_ref[rows, 256 * kvh + LANES:256 * kvh + 2 * LANES] = g1.astype(BF16)


def _attn_key_parts(k, v):
    lane = lax.broadcasted_iota(jnp.int32, (1, LANES), 1)
    low = lane < HEAD_DIM
    swap = lambda a: jnp.concatenate([a[:, HEAD_DIM:], a[:, :HEAD_DIM]], axis=1)
    ks, vs = swap(k), swap(v)
    zero, one = jnp.zeros_like(k), jnp.ones_like(v)
    return [(jnp.where(low, k, zero), jnp.where(low, zero, ks), jnp.where(low, v, one)),
            (jnp.where(low, ks, zero), jnp.where(low, zero, k), jnp.where(low, vs, one))]


def _attn_lat_kernel(sink_ref, q_ref, kl_ref, kc_ref, kr_ref, kx_ref,
                     vl_ref, vc_ref, vr_ref, vx_ref, o_ref, *, nsteps, nblk):
    i = pl.program_id(1)
    nq = ATTN_BLOCK
    rows = lax.broadcasted_iota(jnp.int32, (nq, nq), 0)
    cols = lax.broadcasted_iota(jnp.int32, (nq, nq), 1)
    upper = cols >= rows
    lower = cols <= rows
    parts = [_attn_key_parts(kr[...], vr[...])
             for kr, vr in ((kl_ref, vl_ref), (kc_ref, vc_ref), (kr_ref, vr_ref), (kx_ref, vx_ref))]
    per_head = []
    for kvh in range(N_KV_HEADS):
        left, cen, right, ctx = [p[kvh] for p in parts]
        own = [tuple(t[j * nq:(j + 1) * nq] for t in cen) for j in range(nblk)]
        per_head.append(([left] + own + [right], ctx))
    blocks = []
    for j in range(nblk):
        keys = [tuple(jnp.concatenate([p[t] for p in seq[j:j + 3]] + [ctx[t]], axis=0) for t in range(3))
                for seq, ctx in per_head]
        ok_left = upper & (i >= 1) if j == 0 else upper
        ok_right = lower & (i <= nsteps - 2) if j == nblk - 1 else lower
        blocks.append((j * nq, keys, (ok_left, ok_right)))
    _attn_blocks(sink_ref, q_ref, o_ref, blocks)


def _attn_ctx_kernel(sink_ref, q_ref, kx_ref, vx_ref, o_ref):
    _attn_blocks(sink_ref, q_ref, o_ref, [(0, _attn_key_parts(kx_ref[...], vx_ref[...]), None)])


def _attention_latent(sink, q, kd, vd, kdc, vdc):
    b, l, _ = q.shape
    lc = kdc.shape[1]
    nq = ATTN_BLOCK
    nb = l // nq
    nblk = min(ATTN_BLOCKS_PER_STEP, nb)
    assert nb % nblk == 0
    nsteps = nb // nblk
    smem = pl.BlockSpec(memory_space=pltpu.SMEM)
    edge = lambda f: pl.BlockSpec((None, nq, KV_WIDTH), f)
    left = lambda bb, i: (bb, jnp.maximum(nblk * i - 1, 0), 0)
    cen = lambda bb, i: (bb, i, 0)
    right = lambda bb, i: (bb, jnp.minimum(nblk * i + nblk, nb - 1), 0)
    mid = pl.BlockSpec((None, nblk * nq, KV_WIDTH), cen)
    ctx = pl.BlockSpec((None, lc, KV_WIDTH), lambda bb, i: (bb, 0, 0))
    return pl.pallas_call(
        functools.partial(_attn_lat_kernel, nsteps=nsteps, nblk=nblk),
        grid=(b, nsteps),
        in_specs=[smem, pl.BlockSpec((None, nblk * nq, ATTN_WIDTH), cen),
                  edge(left), mid, edge(right), ctx,
                  edge(left), mid, edge(right), ctx],
        out_specs=pl.BlockSpec((None, nblk * nq, ATTN_WIDTH), cen),
        out_shape=jax.ShapeDtypeStruct((b, l, ATTN_WIDTH), BF16),
        compiler_params=_cparams(("arbitrary", "arbitrary")),
        name="attn_latent",
    )(sink, q, kd, kd, kd, kdc, vd, vd, vd, vdc)


def _attention_ctx(sink, qc, kdc, vdc):
    b, lc, _ = qc.shape
    nq = ATTN_BLOCK
    smem = pl.BlockSpec(memory_space=pltpu.SMEM)
    ctx = pl.BlockSpec((None, lc, KV_WIDTH), lambda bb, i: (bb, 0, 0))
    return pl.pallas_call(
        _attn_ctx_kernel,
        grid=(b, lc // nq),
        in_specs=[smem, pl.BlockSpec((None, nq, ATTN_WIDTH), lambda bb, i: (bb, i, 0)), ctx, ctx],
        out_specs=pl.BlockSpec((None, nq, ATTN_WIDTH), lambda bb, i: (bb, i, 0)),
        out_shape=jax.ShapeDtypeStruct((b, lc, ATTN_WIDTH), BF16),
        compiler_params=_cparams(("arbitrary", "arbitrary")),
        name="attn_ctx",
    )(sink, qc, kdc, vdc)


def _pool_band_matrices():
    r = np.arange(POOL_TILE)[:, None]
    a = np.arange(POOL_TILE + 2 * POOL_HALO)[None, :] - POOL_HALO
    mats = [((a >= r - w // 2) & (a < r + w - w // 2)) for w in POOL_WINDOWS]
    return jnp.asarray(np.concatenate(mats, axis=0).astype(np.float32), dtype=BF16)


def _pool_kernel(u_ref, a_ref, w_ref, sc_ref, o_ref, pad_ref, *, ls):
    halo = POOL_HALO
    zeros = jnp.zeros((halo, POOL_WIDTH), BF16)
    pad_ref[0:halo, :] = zeros
    pad_ref[halo + ls:2 * halo + ls, :] = zeros
    pad_ref[halo:halo + ls, :] = u_ref[...]
    lane = lax.broadcasted_iota(jnp.int32, (1, POOL_WIDTH), 1)
    grp = lane // POOL_GROUP
    wl = jnp.where(grp == 0, POOL_WINDOWS[0],
                   jnp.where(grp == 1, POOL_WINDOWS[1],
                             jnp.where(grp == 2, POOL_WINDOWS[2], POOL_WINDOWS[3])))
    half = wl // 2

    ntile = ls // POOL_TILE
    grp_tiles = min(8, ntile)
    assert ntile % grp_tiles == 0
    nw = len(POOL_WINDOWS)

    def body(gi, carry):
        t0s = [pl.multiple_of((gi * grp_tiles + j) * POOL_TILE, POOL_TILE) for j in range(grp_tiles)]
        sums = [_dot(a_ref[...], pad_ref[pl.ds(t0, POOL_TILE + 2 * halo), :]) for t0 in t0s]
        ds = []
        for t0, sm in zip(t0s, sums):
            acc = sm[0:POOL_TILE]
            for g in range(1, nw):
                acc = jnp.where(grp == g, sm[g * POOL_TILE:(g + 1) * POOL_TILE], acc)
            p = t0 + lax.broadcasted_iota(jnp.int32, (POOL_TILE, 1), 0)
            hi = jnp.minimum(p + (wl - half), ls)
            lo = jnp.maximum(p - half, 0)
            cnt = (hi - lo).astype(F32)
            ut = u_ref[pl.ds(t0, POOL_TILE), :].astype(F32)
            ds.append((acc / cnt - ut).astype(BF16))
        rows = pl.ds(t0s[0], grp_tiles * POOL_TILE)
        o_ref[rows, :] = (_dot(jnp.concatenate(ds, axis=0), w_ref[...]) * sc_ref[...]).astype(BF16)
        return carry

    lax.fori_loop(0, ntile // grp_tiles, body, 0)


def _pool(u, band, wblk, scale):
    b, ls, _ = u.shape
    return pl.pallas_call(
        functools.partial(_pool_kernel, ls=ls),
        grid=(b,),
        in_specs=[
            pl.BlockSpec((None, ls, POOL_WIDTH), lambda bb: (bb, 0, 0)),
            pl.BlockSpec(band.shape, lambda bb: (0, 0)),
            pl.BlockSpec((POOL_WIDTH, POOL_WIDTH), lambda bb: (0, 0)),
            pl.BlockSpec((1, POOL_WIDTH), lambda bb: (0, 0)),
        ],
        out_specs=pl.BlockSpec((None, ls, POOL_WIDTH), lambda bb: (bb, 0, 0)),
        out_shape=jax.ShapeDtypeStruct((b, ls, POOL_WIDTH), BF16),
        scratch_shapes=[pltpu.VMEM((ls + 2 * POOL_HALO, POOL_WIDTH), BF16)],
        compiler_params=_cparams(("arbitrary",)),
        name="pool",
    )(u, band, wblk, scale)


def _ret_kernel(lg_ref, rq_ref, rk_ref, rv_ref, rg_ref, s0f_ref, s0b_ref, gmat_ref,
                o_ref, sf_ref, sb_ref,
                kvf_ref, kvb_ref, sp_ref, dm_ref, tab_ref, *, nc):
    c_len = RET_CHUNK
    kw, vw = RET_KW, RET_WIDTH

    def per_head(idx, d):
        return jnp.where(idx == 0, lg_ref[d, 0],
                         jnp.where(idx == 1, lg_ref[d, 1],
                                   jnp.where(idx == 2, lg_ref[d, 2], lg_ref[d, 3])))

    hk = lax.broadcasted_iota(jnp.int32, (1, kw), 1) // RET_DK
    hv = lax.broadcasted_iota(jnp.int32, (1, vw), 1) // RET_DV

    @pl.when(pl.program_id(0) == 0)
    def _():
        n_col = lax.broadcasted_iota(jnp.int32, (c_len, 1), 0).astype(F32)
        lgk_f, lgk_b = per_head(hk, 0), per_head(hk, 1)
        tab_ref[0] = jnp.exp(lgk_f * (c_len - 1.0 - n_col))
        tab_ref[1] = jnp.exp(lgk_b * n_col)
        tab_ref[2] = jnp.exp(lgk_f * (n_col + 1.0))
        tab_ref[3] = jnp.exp(lgk_b * (c_len - n_col))
        hs = lax.broadcasted_iota(jnp.int32, (1, RET_HEADS * c_len), 1) // c_len
        m_idx = (lax.broadcasted_iota(jnp.int32, (c_len, RET_HEADS * c_len), 1) & (c_len - 1)).astype(F32)
        n_idx = lax.broadcasted_iota(jnp.int32, (c_len, RET_HEADS * c_len), 0).astype(F32)
        rel = n_idx - m_idx
        dm_ref[0] = jnp.where(rel >= 0, jnp.exp(per_head(hs, 0) * jnp.maximum(rel, 0.0)), 0.0)
        dm_ref[1] = jnp.where(rel <= 0, jnp.exp(per_head(hs, 1) * jnp.maximum(-rel, 0.0)), 0.0)

    bd = (lax.broadcasted_iota(jnp.int32, (kw, vw), 0) // RET_DK) == (lax.broadcasted_iota(jnp.int32, (kw, vw), 1) // RET_DV)
    hk_col = lax.broadcasted_iota(jnp.int32, (kw, 1), 0) // RET_DK
    cd_f = jnp.exp(per_head(hk_col, 0) * float(c_len))
    cd_b = jnp.exp(per_head(hk_col, 1) * float(c_len))

    def rows(c):
        return pl.ds(pl.multiple_of(c * c_len, c_len), c_len)

    grp = min(8, nc)
    assert nc % grp == 0

    def kv_body(gi, carry):
        cs = [gi * grp + j for j in range(grp)]
        ks = [rk_ref[rows(c), :].astype(F32) for c in cs]
        vs = [rv_ref[rows(c), :] for c in cs]
        kf = [(k * tab_ref[0]).astype(BF16) for k in ks]
        kb = [(k * tab_ref[1]).astype(BF16) for k in ks]
        pf = [_dot_tn(a, v) for a, v in zip(kf, vs)]
        pb = [_dot_tn(a, v) for a, v in zip(kb, vs)]
        for j, c in enumerate(cs):
            kvf_ref[c] = jnp.where(bd, pf[j], 0.0)
            kvb_ref[c] = jnp.where(bd, pb[j], 0.0)
        return carry

    lax.fori_loop(0, nc // grp, kv_body, 0)

    def scan_body(j, carry):
        s_f, s_b = carry
        cb = nc - 1 - j
        sp_ref[j, 0:kw, :] = s_f.astype(BF16)
        sp_ref[cb, kw:2 * kw, :] = s_b.astype(BF16)
        return cd_f * s_f + kvf_ref[j], cd_b * s_b + kvb_ref[cb]

    s_f, s_b = lax.fori_loop(0, nc, scan_body, (s0f_ref[...], s0b_ref[...]))
    sf_ref[...] = s_f
    sb_ref[...] = s_b

    def out_body(gi, carry):
        cs = [gi * grp + j for j in range(grp)]
        qs = [rq_ref[rows(c), :] for c in cs]
        ks = [rk_ref[rows(c), :] for c in cs]
        vs = [rv_ref[rows(c), :] for c in cs]
        zk, zv = jnp.zeros_like(ks[0]), jnp.zeros_like(vs[0])
        ksts = [jnp.concatenate([jnp.where(hk == h, k, zk) for h in range(RET_HEADS)], axis=0) for k in ks]
        vsts = [jnp.concatenate([jnp.where(hv == h, v, zv) for h in range(RET_HEADS)], axis=0) for v in vs]
        scs = [_dot_nt(q, kst) for q, kst in zip(qs, ksts)]
        p2s = [jnp.concatenate([(sc * dm_ref[0]).astype(BF16), (sc * dm_ref[1]).astype(BF16)], axis=0)
               for sc in scs]
        q2s = [jnp.concatenate([(q.astype(F32) * tab_ref[2]).astype(BF16),
                                (q.astype(F32) * tab_ref[3]).astype(BF16)], axis=1) for q in qs]
        o2s = [_dot(p2, vst) for p2, vst in zip(p2s, vsts)]
        ocs = [_dot(q2, sp_ref[c]) for q2, c in zip(q2s, cs)]
        o = jnp.concatenate([o2[0:c_len] + o2[c_len:] + oc for o2, oc in zip(o2s, ocs)], axis=0)
        mu = _dot(o.astype(BF16), gmat_ref[...])
        d = o - mu
        var = _dot((d * d).astype(BF16), gmat_ref[...])
        hn = d * lax.rsqrt(var + LN_EPS)
        grows = pl.ds(pl.multiple_of(gi * (grp * c_len), grp * c_len), grp * c_len)
        g = rg_ref[grows, :].astype(F32)
        o_ref[grows, :] = (jax.nn.silu(g) * hn).astype(BF16)
        return carry

    lax.fori_loop(0, nc // grp, out_body, 0)


def _retention(lg, rq, rk, rv, rg, s0f, s0b, gmat):
    b, ls, _ = rq.shape
    nc = ls // RET_CHUNK
    seq = lambda w: pl.BlockSpec((None, ls, w), lambda bb: (bb, 0, 0))
    st = pl.BlockSpec((None, RET_KW, RET_WIDTH), lambda bb: (bb, 0, 0))
    return pl.pallas_call(
        functools.partial(_ret_kernel, nc=nc),
        grid=(b,),
        in_specs=[pl.BlockSpec(memory_space=pltpu.SMEM), seq(RET_KW), seq(RET_KW), seq(RET_WIDTH), seq(RET_WIDTH),
                  st, st, pl.BlockSpec((RET_WIDTH, RET_WIDTH), lambda bb: (0, 0))],
        out_specs=[seq(RET_WIDTH), st, st],
        out_shape=[jax.ShapeDtypeStruct((b, ls, RET_WIDTH), BF16),
                   jax.ShapeDtypeStruct((b, RET_KW, RET_WIDTH), F32),
                   jax.ShapeDtypeStruct((b, RET_KW, RET_WIDTH), F32)],
        scratch_shapes=[
            pltpu.VMEM((nc, RET_KW, RET_WIDTH), F32),
            pltpu.VMEM((nc, RET_KW, RET_WIDTH), F32),
            pltpu.VMEM((nc, 2 * RET_KW, RET_WIDTH), BF16),
            pltpu.VMEM((2, RET_CHUNK, RET_HEADS * RET_CHUNK), F32),
            pltpu.VMEM((4, RET_CHUNK, RET_KW), F32),
        ],
        compiler_params=_cparams(("arbitrary",)),
        name="retention",
    )(lg, rq, rk, rv, rg, s0f, s0b, gmat)


def _mixer_out_ln1(attn_ref, pool_ref, ret_ref, x_ref, m, w_ref, g_ref, b_ref, alpha):
    y = (_dot(attn_ref[...], w_ref[0:ATTN_WIDTH, :])
         + _dot(pool_ref[...], w_ref[ATTN_WIDTH:ATTN_WIDTH + POOL_WIDTH, :])
         + _dot(ret_ref[...], w_ref[ATTN_WIDTH + POOL_WIDTH:, :]))
    x1 = _layer_norm(alpha * x_ref[...] + m[2:3, :] * y, g_ref[...], b_ref[...])
    return x1, x1 * (1.0 + m[4:5, :]) + m[3:4, :]


def _route_tile(logits_t, tri_ref, carry_ref):
    ne, tm = logits_t.shape
    row = lax.broadcasted_iota(jnp.int32, (ne, tm), 0)
    l = jnp.where(row < N_EXPERTS, logits_t, -jnp.inf)
    m1 = jnp.max(l, axis=0, keepdims=True)
    i1 = jnp.min(jnp.where(l == m1, row, ne), axis=0, keepdims=True)
    l2 = jnp.where(row == i1, -jnp.inf, l)
    m2 = jnp.max(l2, axis=0, keepdims=True)
    i2 = jnp.min(jnp.where(l2 == m2, row, ne), axis=0, keepdims=True)
    e = jnp.exp(m2 - m1)
    w1 = 1.0 / (1.0 + e)
    w2 = e / (1.0 + e)
    oh = jnp.where((row == i1) | (row == i2), 1.0, 0.0)
    carry = carry_ref[:, 0:1]
    cum = _dot(oh.astype(BF16), tri_ref[...]) + carry
    r1 = jnp.sum(jnp.where(row == i1, cum, 0.0), axis=0, keepdims=True)
    r2 = jnp.sum(jnp.where(row == i2, cum, 0.0), axis=0, keepdims=True)
    carry_ref[...] = jnp.broadcast_to(carry + jnp.sum(oh, axis=1, keepdims=True), carry_ref.shape)
    r8 = lax.broadcasted_iota(jnp.int32, (8, tm), 0)
    return jnp.where(r8 == 0, i1.astype(F32),
                     jnp.where(r8 == 1, i2.astype(F32),
                               jnp.where(r8 == 2, r1,
                                         jnp.where(r8 == 3, r2,
                                                   jnp.where(r8 == 4, w1,
                                                             jnp.where(r8 == 5, w2, 0.0))))))


def _outproj_kernel(attn_ref, pool_ref, ret_ref, x_ref, mod_ref, w_ref, g_ref, b_ref, router_ref, tri_ref,
                    x1_ref, t_ref, tab_ref, tabt_ref, cnt_ref, carry_ref, *, alpha):
    @pl.when((pl.program_id(0) == 0) & (pl.program_id(1) == 0))
    def _():
        carry_ref[...] = jnp.zeros_like(carry_ref)

    x1, t = _mixer_out_ln1(attn_ref, pool_ref, ret_ref, x_ref, mod_ref[...], w_ref, g_ref, b_ref, alpha)
    x1_ref[...] = x1
    _store_tile_rows(t_ref, (), t)
    tab_t = _route_tile(_dot_nt(router_ref[...], t.astype(BF16)), tri_ref, carry_ref)
    tabt_ref[...] = tab_t
    tm = tab_t.shape[1]
    tab_ref[...] = jnp.concatenate([tab_t, jnp.zeros((LANES - 8, tm), F32)], axis=0).T
    cnt_ref[...] = carry_ref[0:8, :]


def _mixer_row_specs(tm, d):
    row = lambda w: pl.BlockSpec((None, tm, w), lambda bb, j: (bb, j, 0))
    return [row(ATTN_WIDTH), row(POOL_WIDTH), row(RET_WIDTH), row(d)]


def _outproj_ln_route(attn, pool, ret, x, modtab, mod_row_fn, w_out, ln_g, ln_b, alpha, router):
    b, ls, d = x.shape
    tm = min(ROUTE_TILE, ls)
    nt = ls // tm
    tri = jnp.asarray(np.triu(np.ones((tm, tm), np.float32), 1), dtype=BF16)
    row = lambda w: pl.BlockSpec((None, tm, w), lambda bb, j: (bb, j, 0))
    const = lambda shape: pl.BlockSpec(shape, lambda bb, j: tuple(0 for _ in shape))
    return pl.pallas_call(
        functools.partial(_outproj_kernel, alpha=alpha),
        grid=(b, nt),
        in_specs=_mixer_row_specs(tm, d) + [
            pl.BlockSpec((None, 6, d), lambda bb, j: (mod_row_fn(bb), 0, 0)),
            const((d, d)), const((1, d)), const((1, d)), const((ROUTER_ROWS, d)), const((tm, tm))],
        out_specs=[row(d), pl.BlockSpec((None, tm * SUBLANES, LANES), lambda bb, j: (bb, j, 0)), row(LANES),
                   pl.BlockSpec((8, tm), lambda bb, j: (0, bb * nt + j)), const((8, LANES))],
        out_shape=[jax.ShapeDtypeStruct((b, ls, d), F32),
                   jax.ShapeDtypeStruct((b, ls * SUBLANES, LANES), F32),
                   jax.ShapeDtypeStruct((b, ls, LANES), F32),
                   jax.ShapeDtypeStruct((8, b * ls), F32),
                   jax.ShapeDtypeStruct((8, LANES), F32)],
        scratch_shapes=[pltpu.VMEM((ROUTER_ROWS, LANES), F32)],
        compiler_params=_cparams(("arbitrary", "arbitrary")),
        name="outproj_ln_route",
    )(attn, pool, ret, x, modtab, w_out, ln_g, ln_b, router, tri)


N_FFN_CHUNKS = FFN_DIM // FFN_CHUNK
N_FFN_GAPS = 3 * N_FFN_CHUNKS


def _swiglu_rows(tb, wg_ref, wu_ref, wd_ref, between=None):
    gap = (lambda j: None) if between is None else between
    acts = []
    for c in range(N_FFN_CHUNKS):
        sl = slice(c * FFN_CHUNK, (c + 1) * FFN_CHUNK)
        g = _dot(tb, wg_ref[:, sl])
        gap(3 * c)
        u = _dot(tb, wu_ref[:, sl])
        gap(3 * c + 1)
        acts.append((jax.nn.silu(g) * u).astype(BF16))
        gap(3 * c + 2)
    return _dot(jnp.concatenate(acts, axis=1), wd_ref[...])


def _dense_post_kernel(attn_ref, pool_ref, ret_ref, x_ref, mod_ref, w_ref, g1_ref, b1_ref,
                       wg_ref, wu_ref, wd_ref, g2_ref, b2_ref, o_ref, *, alpha):
    m = mod_ref[...]
    x1, t = _mixer_out_ln1(attn_ref, pool_ref, ret_ref, x_ref, m, w_ref, g1_ref, b1_ref, alpha)
    f = _swiglu_rows(t.astype(BF16), wg_ref, wu_ref, wd_ref)
    o_ref[...] = _layer_norm(alpha * x1 + m[5:6, :] * f, g2_ref[...], b2_ref[...])


def _dense_post(attn, pool, ret, x, modtab, mod_row_fn, w_out, g1, b1, wg, wu, wd, g2, b2, alpha):
    b, ls, d = x.shape
    tm = min(512, ls)
    nt = ls // tm
    const = lambda shape: pl.BlockSpec(shape, lambda bb, j: tuple(0 for _ in shape),
                                       pipeline_mode=pl.Buffered(1))
    return pl.pallas_call(
        functools.partial(_dense_post_kernel, alpha=alpha),
        grid=(b, nt),
        in_specs=_mixer_row_specs(tm, d) + [
            pl.BlockSpec((None, 6, d), lambda bb, j: (mod_row_fn(bb), 0, 0)),
            const((d, d)), const((1, d)), const((1, d)),
            const((d, FFN_DIM)), const((d, FFN_DIM)), const((FFN_DIM, d)),
            const((1, d)), const((1, d))],
        out_specs=pl.BlockSpec((None, tm, d), lambda bb, j: (bb, j, 0)),
        out_shape=jax.ShapeDtypeStruct((b, ls, d), F32),
        compiler_params=_cparams(("arbitrary", "arbitrary")),
        name="dense_post",
    )(attn, pool, ret, x, modtab, w_out, g1, b1, wg, wu, wd, g2, b2)


def _invert_kernel(p1_ref, p2_ref, lo_ref, hi_ref, dst_ref, *, n):
    for e in range(N_EXPERTS + 1):
        def fill(s, carry):
            dst_ref[s] = 2 * n + (s & (MOE_TILE - 1))
            return carry

        lax.fori_loop(lo_ref[e], hi_ref[e], fill, 0)

    def body(t, carry):
        dst_ref[p1_ref[t]] = t
        dst_ref[p2_ref[t]] = n + t
        return carry

    lax.fori_loop(0, n, body, 0, unroll=16)


def _invert(pos1, pos2, fill_lo, fill_hi, nslots):
    n = pos1.shape[0]
    smem = pl.BlockSpec(memory_space=pltpu.SMEM)
    return pl.pallas_call(
        functools.partial(_invert_kernel, n=n),
        in_specs=[smem, smem, smem, smem],
        out_specs=smem,
        out_shape=jax.ShapeDtypeStruct((nslots,), jnp.int32),
        name="invert",
    )(pos1, pos2, fill_lo, fill_hi)


def _slot_source_row(v, n):
    if n & (n - 1) == 0:
        return v & (n - 1)
    return jnp.where(v >= 2 * n, v - 2 * n, jnp.where(v >= n, v - n, v))


def _moe_kernel(te_ref, nt_ref, inv_ref, t_ref, wg_ref, wu_ref, wd_ref, y_ref, xbuf, obuf, gsem, ssem, *, n):
    del te_ref
    i = pl.program_id(0)
    nt = nt_ref[0]
    tm = MOE_TILE
    slot = i & 1
    other = 1 - slot

    def row_tile(r):
        return pl.ds(pl.multiple_of(r * SUBLANES, SUBLANES), SUBLANES)

    def gather_row(tile, r, buf):
        src = _slot_source_row(inv_ref[tile * tm + r], n)
        pltpu.make_async_copy(t_ref.at[row_tile(src)], xbuf.at[buf, row_tile(r)], gsem.at[buf]).start()

    def scatter_row(tile, r, buf, real):
        dst = jnp.where(real, inv_ref[tile * tm + r], 2 * n + r)
        pltpu.make_async_copy(obuf.at[buf, row_tile(r)], y_ref.at[row_tile(dst)], ssem.at[buf]).start()

    def wait_gather(buf):
        pltpu.make_async_copy(t_ref.at[pl.ds(0, tm * SUBLANES)], xbuf.at[buf], gsem.at[buf]).wait()

    def wait_scatter(buf):
        pltpu.make_async_copy(obuf.at[buf], y_ref.at[pl.ds(0, tm * SUBLANES)], ssem.at[buf]).wait()

    @pl.when(i == 0)
    def _():
        obuf[1] = jnp.zeros(obuf.shape[1:], obuf.dtype)

        def first(r, carry):
            gather_row(0, r, 0)
            return carry

        lax.fori_loop(0, tm, first, 0, unroll=8)

    def main(cur, oth):
        wait_gather(cur)

        @pl.when(i >= 1)
        def _():
            wait_scatter(cur)

        nxt = jnp.minimum(i + 1, nt - 1)
        prv = jnp.maximum(i - 1, 0)
        real = i >= 1
        n_issue_gaps = (4 * N_FFN_GAPS) // 5
        n_gather_gaps = n_issue_gaps // 2
        per_g = -(-tm // n_gather_gaps)
        per_s = -(-tm // (n_issue_gaps - n_gather_gaps))

        def between(j):
            if j < n_gather_gaps:
                for r in range(j * per_g, min((j + 1) * per_g, tm)):
                    gather_row(nxt, r, oth)
            elif j < n_issue_gaps:
                k = j - n_gather_gaps
                for r in range(k * per_s, min((k + 1) * per_s, tm)):
                    scatter_row(prv, r, oth, real)

        x = _load_tile_rows(xbuf, (cur,), tm).astype(BF16)
        _store_tile_rows(obuf, (cur,), _swiglu_rows(x, wg_ref, wu_ref, wd_ref, between))

    @pl.when((i < nt) & (slot == 0))
    def _():
        main(0, 1)

    @pl.when((i < nt) & (slot == 1))
    def _():
        main(1, 0)

    @pl.when(i == nt)
    def _():
        last = nt - 1
        buf = last & 1
        wait_scatter(nt & 1)

        def tail(r, carry):
            scatter_row(last, r, buf, True)
            return carry

        lax.fori_loop(0, tm, tail, 0, unroll=8)
        wait_scatter(buf)
        wait_gather(nt & 1)


def _moe_ffn(tile_expert, ntiles, inv, t, wg, wu, wd):
    n = t.shape[0] // SUBLANES
    d = D_MODEL
    tm = MOE_TILE
    assert n >= tm
    nt_max = inv.shape[0] // tm
    wspec = lambda shape: pl.BlockSpec((None,) + shape, lambda i, te, nt, iv: (te[i], 0, 0))
    return pl.pallas_call(
        functools.partial(_moe_kernel, n=n),
        grid_spec=pltpu.PrefetchScalarGridSpec(
            num_scalar_prefetch=3,
            grid=(nt_max,),
            in_specs=[pl.BlockSpec(memory_space=pl.ANY),
                      wspec((d, FFN_DIM)), wspec((d, FFN_DIM)), wspec((FFN_DIM, d))],
            out_specs=pl.BlockSpec(memory_space=pl.ANY),
            scratch_shapes=[pltpu.VMEM((2, tm * SUBLANES, LANES), F32), pltpu.VMEM((2, tm * SUBLANES, LANES), F32),
                            pltpu.SemaphoreType.DMA((2,)), pltpu.SemaphoreType.DMA((2,))],
        ),
        out_shape=jax.ShapeDtypeStruct(((2 * n + tm) * SUBLANES, LANES), F32),
        compiler_params=_cparams(("arbitrary",)),
        name="moe_ffn",
    )(tile_expert, ntiles, inv, t, wg, wu, wd)


def _combine_kernel(y1_ref, y2_ref, x1_ref, rt_ref, mod_ref, g_ref, b_ref, o_ref, *, alpha):
    rt = rt_ref[...]
    tm = x1_ref.shape[0]
    f = rt[:, 4:5] * _load_tile_rows(y1_ref, (), tm) + rt[:, 5:6] * _load_tile_rows(y2_ref, (), tm)
    m = mod_ref[...]
    o_ref[...] = _layer_norm(alpha * x1_ref[...] + m[5:6, :] * f, g_ref[...], b_ref[...])


def _combine_ln(y, x1, rtab, modtab, l, ln_g, ln_b, alpha):
    n, d = x1.shape
    tm = min(ROW_TILE, n)
    per_b = l // tm
    nblk = n // tm
    const = lambda shape: pl.BlockSpec(shape, lambda i: tuple(0 for _ in shape))
    return pl.pallas_call(
        functools.partial(_combine_kernel, alpha=alpha),
        grid=(nblk,),
        in_specs=[pl.BlockSpec((tm * SUBLANES, LANES), lambda i: (i, 0)),
                  pl.BlockSpec((tm * SUBLANES, LANES), lambda i: (nblk + i, 0)),
                  pl.BlockSpec((tm, d), lambda i: (i, 0)),
                  pl.BlockSpec((tm, LANES), lambda i: (i, 0)),
                  pl.BlockSpec((None, 6, d), lambda i: (i // per_b, 0, 0)),
                  const((1, d)), const((1, d))],
        out_specs=pl.BlockSpec((tm, d), lambda i: (i, 0)),
        out_shape=jax.ShapeDtypeStruct((n, d), F32),
        compiler_params=_cparams(("arbitrary",)),
        name="combine_ln",
    )(y, y, x1, rtab, modtab, ln_g, ln_b)


def _moe_layer(t, rtab, rtab_t, cnt, x1, modtab, l, wg, wu, wd, ln_g, ln_b, alpha):
    n = x1.shape[0]
    counts = cnt[:N_EXPERTS, 0].astype(jnp.int32)
    tiles_e = (counts + MOE_TILE - 1) // MOE_TILE
    tile_end = jnp.cumsum(tiles_e)
    tile_start = tile_end - tiles_e
    ntiles = tile_end[-1:]
    nt_max = (2 * n) // MOE_TILE + N_EXPERTS + 1
    tile_expert = jnp.minimum(
        jnp.sum((jnp.arange(nt_max)[:, None] >= tile_end[None, :]).astype(jnp.int32), axis=1),
        N_EXPERTS - 1).astype(jnp.int32)
    e1 = rtab_t[0].astype(jnp.int32)
    e2 = rtab_t[1].astype(jnp.int32)
    row_start = tile_start * MOE_TILE
    pos1 = row_start[e1] + rtab_t[2].astype(jnp.int32)
    pos2 = row_start[e2] + rtab_t[3].astype(jnp.int32)
    fill_lo = jnp.concatenate([row_start + counts, tile_end[-1:] * MOE_TILE]).astype(jnp.int32)
    fill_hi = jnp.concatenate([tile_end * MOE_TILE, jnp.full((1,), nt_max * MOE_TILE)]).astype(jnp.int32)
    inv = _invert(pos1, pos2, fill_lo, fill_hi, nt_max * MOE_TILE)
    y = _moe_ffn(tile_expert, ntiles.astype(jnp.int32), inv, t, wg, wu, wd)
    return _combine_ln(y, x1, rtab, modtab, l, ln_g, ln_b, alpha)


def _prep_w_in(w):
    q = w[:, 0:ATTN_WIDTH] * (HEAD_DIM ** -0.5 * LOG2E)
    return jnp.concatenate([q, w[:, ATTN_WIDTH:]], axis=1).astype(BF16)


def _block_diag(mats):
    n = len(mats)
    rows = []
    for i, m in enumerate(mats):
        rows.append(jnp.concatenate([m if j == i else jnp.zeros_like(m) for j in range(n)], axis=1))
    return jnp.concatenate(rows, axis=0)


def kernel(x, c, ctx, c_ctx, w_mod, b_mod, w_in, attn_sink, pool_w, pool_scale, ret_log_decay_fwd,
           ret_log_decay_bwd, w_out, ln1_g, ln1_b, ln2_g, ln2_b, ffn_w_gate, ffn_w_up, ffn_w_down,
           moe_router, moe_w_gate, moe_w_up, moe_w_down):
    b, l, d = x.shape
    lc = ctx.shape[1]
    depth = w_in.shape[0]
    alpha = (2.0 * depth) ** 0.25
    assert d == D_MODEL and l % 256 == 0 and lc % 256 == 0 and l % GRID_W == 0

    n_rows = ((b + 1 + 7) // 8) * 8
    c_all = jnp.concatenate([c, c_ctx[None, :], jnp.zeros((n_rows - b - 1, d), F32)], axis=0)
    mod_all = _modulation(c_all, w_mod, b_mod).reshape(depth, n_rows, 6, d)
    lat_row = lambda bb: bb
    ctx_row = lambda bb: b

    rope_tabs = _rope_tables(l)
    band = _pool_band_matrices()
    gmat = jnp.asarray(np.kron(np.eye(RET_HEADS), np.ones((RET_DV, RET_DV))) / RET_DV, dtype=BF16)
    zero_state = jnp.zeros((b, RET_KW, RET_WIDTH), F32)

    xc = ctx
    for i in range(depth):
        last = i == depth - 1
        modtab = mod_all[i]
        w2 = _prep_w_in(w_in[i])
        wo = w_out[i].astype(BF16)
        sink = attn_sink[i].astype(F32)
        pw = _block_diag([pool_w[i, g] for g in range(len(POOL_WINDOWS))]).astype(BF16)
        ps = pool_scale[i].reshape(1, POOL_WIDTH).astype(F32)
        lg = jnp.stack([ret_log_decay_fwd[i], ret_log_decay_bwd[i]]).astype(F32)
        g1, b1 = ln1_g[i].reshape(1, d), ln1_b[i].reshape(1, d)
        g2, b2 = ln2_g[i].reshape(1, d), ln2_b[i].reshape(1, d)

        qc, kdc, vdc, uc, rqc, rkc, rvc, rgc = _inproj(xc, modtab, ctx_row, w2, None)
        ret_c, s_f, s_b = _retention(lg, rqc, rkc, rvc, rgc, zero_state, zero_state, gmat)

        q, kd, vd, u, rq, rk, rv, rg = _inproj(x, modtab, lat_row, w2, rope_tabs)
        attn = _attention_latent(sink, q, kd, vd, kdc, vdc)
        pool = _pool(u, band, pw, ps)
        ret, _, _ = _retention(lg, rq, rk, rv, rg, s_f, s_b, gmat)

        j = i // 2
        if i % 2 == 0:
            wg, wu, wd = ffn_w_gate[j].astype(BF16), ffn_w_up[j].astype(BF16), ffn_w_down[j].astype(BF16)
            x = _dense_post(attn, pool, ret, x, modtab, lat_row, wo, g1, b1, wg, wu, wd, g2, b2, alpha)
        else:
            wg, wu, wd = moe_w_gate[j].astype(BF16), moe_w_up[j].astype(BF16), moe_w_down[j].astype(BF16)
            router = jnp.pad(moe_router[j].T, ((0, ROUTER_ROWS - N_EXPERTS), (0, 0))).astype(BF16)
            x1, t, rtab, rtab_t, cnt = _outproj_ln_route(attn, pool, ret, x, modtab, lat_row, wo, g1, b1, alpha,
                                                         router)
            x = _moe_layer(t.reshape(b * l * SUBLANES, LANES), rtab.reshape(b * l, LANES), rtab_t, cnt,
                           x1.reshape(b * l, d), modtab, l, wg, wu, wd, g2, b2, alpha).reshape(b, l, d)

        if not last:
            attn_c = _attention_ctx(sink, qc, kdc, vdc)
            pool_c = _pool(uc, band, pw, ps)
            if i % 2 == 0:
                xc = _dense_post(attn_c, pool_c, ret_c, xc, modtab, ctx_row, wo, g1, b1, wg, wu, wd, g2, b2,
                                 alpha)
            else:
                x1c, tc, rtab_c, rtab_tc, cnt_c = _outproj_ln_route(attn_c, pool_c, ret_c, xc, modtab, ctx_row, wo,
                                                                    g1, b1, alpha, router)
                ctx_mod = jnp.broadcast_to(modtab[b:b + 1], (b, 6, d))
                xc = _moe_layer(tc.reshape(b * lc * SUBLANES, LANES), rtab_c.reshape(b * lc, LANES), rtab_tc,
                                cnt_c, x1c.reshape(b * lc, d), ctx_mod, lc, wg, wu, wd, g2, b2,
                                alpha).reshape(b, lc, d)
    return x
```

```python
import functools
import math

import numpy as np
import jax
import jax.numpy as jnp
from jax import lax
from jax.experimental import pallas as pl
from jax.experimental.pallas import tpu as pltpu

F32 = jnp.float32
BF16 = jnp.bfloat16

D_MODEL = 1024
GRID_W = 64
HEAD_DIM = 64
N_HEADS = 8
N_KV_HEADS = 2
ATTN_WIDTH = N_HEADS * HEAD_DIM
ATTN_BLOCK = 128
ATTN_BLOCKS_PER_STEP = 8
ROPE_BASE = 10000.0
POOL_WINDOWS = (2, 4, 8, 16)
POOL_WIDTH = 256
POOL_GROUP = 64
POOL_TILE = 256
POOL_HALO = 16
RET_HEADS = 4
RET_DK = 32
RET_DV = 64
RET_WIDTH = 256
RET_KW = RET_HEADS * RET_DK
RET_CHUNK = 128
FFN_DIM = 2816
FFN_CHUNK = 256
N_EXPERTS = 8
ROUTER_ROWS = 16
ROUTE_TILE = 512
MOE_TILE = 512
ROW_TILE = 512
LN_EPS = 1e-5
NEG_INF = -1e30
LOG2E = math.log2(math.e)
LANES = 128
IN_WIDTH = 1792

VMEM_LIMIT = 56 * 1024 * 1024


def _cparams(sem):
    return pltpu.CompilerParams(dimension_semantics=sem, vmem_limit_bytes=VMEM_LIMIT)


def _dot(a, b):
    return jnp.dot(a, b, preferred_element_type=F32)


def _dot_nt(a, b):
    return lax.dot_general(a, b, (((1,), (1,)), ((), ())), preferred_element_type=F32)


def _dot_tn(a, b):
    return lax.dot_general(a, b, (((0,), (0,)), ((), ())), preferred_element_type=F32)


def _split_bf16(a):
    hi = a.astype(BF16)
    lo = (a - hi.astype(F32)).astype(BF16)
    return hi, lo


def _dot3(a, b):
    ah, al = _split_bf16(a)
    bh, bl = _split_bf16(b)
    return _dot(ah, bh) + _dot(ah, bl) + _dot(al, bh)


SUBLANES = 8
ROW_TILES = D_MODEL // LANES


def _store_tile_rows(ref, idx, val):
    rows = val.shape[0]
    for c in range(ROW_TILES):
        ref[idx + (pl.ds(c, rows, stride=SUBLANES), slice(None))] = val[:, c * LANES:(c + 1) * LANES]


def _load_tile_rows(ref, idx, rows):
    return jnp.concatenate([ref[idx + (pl.ds(c, rows, stride=SUBLANES), slice(None))]
                            for c in range(ROW_TILES)], axis=1)


def _layer_norm(z, g, b):
    mu = jnp.mean(z, axis=-1, keepdims=True)
    d = z - mu
    var = jnp.mean(d * d, axis=-1, keepdims=True)
    return d * lax.rsqrt(var + LN_EPS) * g + b


def _mod_kernel(c_ref, w_ref, b_ref, o_ref):
    s = jax.nn.silu(c_ref[...])
    o_ref[...] = _dot3(s, w_ref[...]) + b_ref[...]


def _modulation(c_all, w_mod, b_mod):
    depth, d, n6 = w_mod.shape
    r = c_all.shape[0]
    bn = 1536
    return pl.pallas_call(
        _mod_kernel,
        grid=(depth, n6 // bn),
        in_specs=[
            pl.BlockSpec((r, d), lambda i, j: (0, 0)),
            pl.BlockSpec((None, d, bn), lambda i, j: (i, 0, j)),
            pl.BlockSpec((None, 1, bn), lambda i, j: (i, 0, j)),
        ],
        out_specs=pl.BlockSpec((None, r, bn), lambda i, j: (i, 0, j)),
        out_shape=jax.ShapeDtypeStruct((depth, r, n6), F32),
        compiler_params=_cparams(("arbitrary", "arbitrary")),
        name="modulation",
    )(c_all, w_mod, b_mod.reshape(depth, 1, n6))


KV_WIDTH = N_KV_HEADS * HEAD_DIM
_IN_OUT_WIDTHS = (ATTN_WIDTH, KV_WIDTH, KV_WIDTH, POOL_WIDTH, RET_KW, RET_KW, RET_WIDTH, RET_WIDTH)
_IN_COL_STARTS = tuple(int(c) for c in np.cumsum((0,) + _IN_OUT_WIDTHS))


def _rope_groups(a, cos, s_prev, s_next):
    outs = []
    for g in range(a.shape[1] // LANES):
        ag = a[:, g * LANES:(g + 1) * LANES]
        outs.append(ag * cos + pltpu.roll(ag, 16, 1) * s_prev + pltpu.roll(ag, LANES - 16, 1) * s_next)
    return outs[0] if len(outs) == 1 else jnp.concatenate(outs, axis=1)


def _inproj_kernel(*refs, rope):
    if rope:
        x_ref, mod_ref, w_ref, cos_ref, sp_ref, sn_ref = refs[:6]
        outs = refs[6:]
    else:
        x_ref, mod_ref, w_ref = refs[:3]
        outs = refs[3:]
    q_ref, k_ref, v_ref, u_ref, rq_ref, rk_ref, rv_ref, rg_ref = outs
    m = mod_ref[...]
    h = (x_ref[...] * (1.0 + m[1:2, :]) + m[0:1, :]).astype(BF16)

    def mm(j0, j1):
        return _dot(h, w_ref[:, _IN_COL_STARTS[j0]:_IN_COL_STARTS[j1]])

    q = mm(0, 1)
    kv = mm(1, 3)
    k, v = kv[:, :KV_WIDTH], kv[:, KV_WIDTH:]
    if rope:
        cos, sp, sn = cos_ref[...], sp_ref[...], sn_ref[...]
        q = _rope_groups(q, cos, sp, sn)
        k = _rope_groups(k, cos, sp, sn)
    q_ref[...] = q.astype(BF16)
    k_ref[...] = k.astype(BF16)
    v_ref[...] = v.astype(BF16)
    u_ref[...] = mm(3, 4).astype(BF16)
    rqk = mm(4, 6)
    rq_ref[...] = rqk[:, :RET_KW].astype(BF16)
    rk_ref[...] = (rqk[:, RET_KW:] * (RET_DK ** -0.5)).astype(BF16)
    rv_ref[...] = mm(6, 7).astype(BF16)
    rg_ref[...] = mm(7, 8).astype(BF16)


def _inproj(x, modtab, mod_row_fn, w2, rope_tabs):
    b, ls, d = x.shape
    tm = min(1024, ls)
    nt = ls // tm
    rope = rope_tabs is not None
    in_specs = [
        pl.BlockSpec((None, tm, d), lambda j, bb: (bb, j, 0)),
        pl.BlockSpec((None, 6, d), lambda j, bb: (mod_row_fn(bb), 0, 0)),
        pl.BlockSpec((d, IN_WIDTH), lambda j, bb: (0, 0)),
    ]
    args = [x, modtab, w2]
    if rope:
        in_specs += [pl.BlockSpec((tm, LANES), lambda j, bb: (j, 0))] * 3
        args += list(rope_tabs)
    out_specs = [pl.BlockSpec((None, tm, w), lambda j, bb: (bb, j, 0)) for w in _IN_OUT_WIDTHS]
    out_shape = [jax.ShapeDtypeStruct((b, ls, w), BF16) for w in _IN_OUT_WIDTHS]
    return pl.pallas_call(
        functools.partial(_inproj_kernel, rope=rope),
        grid=(nt, b),
        in_specs=in_specs,
        out_specs=out_specs,
        out_shape=out_shape,
        compiler_params=_cparams(("arbitrary", "arbitrary")),
        name="inproj_rope" if rope else "inproj",
    )(*args)


def _rope_tables(l):
    p = np.arange(l)
    row = (p // GRID_W).astype(np.float64)
    col = (p % GRID_W).astype(np.float64)
    quarter = HEAD_DIM // 4
    inv = ROPE_BASE ** (-np.arange(quarter, dtype=np.float64) / quarter)
    lane = np.arange(LANES)
    j = lane % HEAD_DIM
    pos = np.where((j < HEAD_DIM // 2)[None, :], row[:, None], col[:, None])
    ang = pos * inv[j % quarter][None, :]
    cos, sin = np.cos(ang), np.sin(ang)
    second = ((lane % (2 * quarter)) >= quarter)[None, :]
    s_prev = np.where(second, sin, 0.0)
    s_next = np.where(second, 0.0, -sin)
    return tuple(jnp.asarray(a.astype(np.float32)) for a in (cos, s_prev, s_next))


def _attn_blocks(sink_ref, q_ref, o_ref, blocks):
    nq = ATTN_BLOCK
    lane = lax.broadcasted_iota(jnp.int32, (1, LANES), 1)
    rid = lax.broadcasted_iota(jnp.int32, (4, 1, 1), 0)
    chains = [(row0, kvh, keys[kvh], ok) for row0, keys, ok in blocks for kvh in range(N_KV_HEADS)]

    scores = []
    for row0, kvh, (k_lo, k_hi, _), _ in chains:
        rows = pl.ds(row0, nq)
        qg = jnp.concatenate([q_ref[rows, 256 * kvh:256 * kvh + LANES],
                              q_ref[rows, 256 * kvh + LANES:256 * kvh + 2 * LANES]], axis=0)
        scores.append(jnp.concatenate([_dot_nt(qg, k_lo), _dot_nt(qg, k_hi)], axis=0).reshape(4, nq, -1))

    probs, sink_terms = [], []
    for (row0, kvh, _, ok), s in zip(chains, scores):
        if ok is not None:
            ok_left, ok_right = ok
            s = jnp.concatenate([jnp.where(ok_left[None], s[:, :, 0:nq], NEG_INF), s[:, :, nq:2 * nq],
                                 jnp.where(ok_right[None], s[:, :, 2 * nq:3 * nq], NEG_INF), s[:, :, 3 * nq:]],
                                axis=2)
        h0 = 4 * kvh
        sk = jnp.where(rid == 0, sink_ref[h0],
                       jnp.where(rid == 1, sink_ref[h0 + 2],
                                 jnp.where(rid == 2, sink_ref[h0 + 1], sink_ref[h0 + 3]))) * LOG2E
        m = jnp.maximum(jnp.max(s, axis=2, keepdims=True), sk)
        probs.append(jnp.exp2((s - m).astype(BF16)).reshape(4 * nq, -1))
        sink_terms.append(jnp.exp2(sk - m).reshape(4 * nq, 1))

    outs = [_dot(e, v_ext) for e, (_, _, (_, _, v_ext), _) in zip(probs, chains)]

    for (row0, kvh, _, _), o, es in zip(chains, outs, sink_terms):
        rows = pl.ds(row0, nq)
        swapped = pltpu.roll(o, HEAD_DIM, 1)
        even = o[0:2 * nq] / (swapped[0:2 * nq] + es[0:2 * nq])
        odd = swapped[2 * nq:] / (o[2 * nq:] + es[2 * nq:])
        g0 = jnp.where(lane < HEAD_DIM, even[0:nq], odd[0:nq])
        g1 = jnp.where(lane < HEAD_DIM, even[nq:], odd[nq:])
        o_ref[rows, 256 * kvh:256 * kvh + LANES] = g0.astype(BF16)
        o_ref[rows, 256 * kvh + LANES:256 * kvh + 2 * LANES] = g1.astype(BF16)


def _attn_key_parts(k, v):
    lane = lax.broadcasted_iota(jnp.int32, (1, LANES), 1)
    low = lane < HEAD_DIM
    swap = lambda a: jnp.concatenate([a[:, HEAD_DIM:], a[:, :HEAD_DIM]], axis=1)
    ks, vs = swap(k), swap(v)
    zero, one = jnp.zeros_like(k), jnp.ones_like(v)
    return [(jnp.where(low, k, zero), jnp.where(low, zero, ks), jnp.where(low, v, one)),
            (jnp.where(low, ks, zero), jnp.where(low, zero, k), jnp.where(low, vs, one))]


def _attn_lat_kernel(sink_ref, q_ref, kl_ref, kc_ref, kr_ref, kx_ref,
                     vl_ref, vc_ref, vr_ref, vx_ref, o_ref, *, nsteps, nblk):
    i = pl.program_id(1)
    nq = ATTN_BLOCK
    rows = lax.broadcasted_iota(jnp.int32, (nq, nq), 0)
    cols = lax.broadcasted_iota(jnp.int32, (nq, nq), 1)
    upper = cols >= rows
    lower = cols <= rows
    parts = [_attn_key_parts(kr[...], vr[...])
             for kr, vr in ((kl_ref, vl_ref), (kc_ref, vc_ref), (kr_ref, vr_ref), (kx_ref, vx_ref))]
    per_head = []
    for kvh in range(N_KV_HEADS):
        left, cen, right, ctx = [p[kvh] for p in parts]
        own = [tuple(t[j * nq:(j + 1) * nq] for t in cen) for j in range(nblk)]
        per_head.append(([left] + own + [right], ctx))
    blocks = []
    for j in range(nblk):
        keys = [tuple(jnp.concatenate([p[t] for p in seq[j:j + 3]] + [ctx[t]], axis=0) for t in range(3))
                for seq, ctx in per_head]
        ok_left = upper & (i >= 1) if j == 0 else upper
        ok_right = lower & (i <= nsteps - 2) if j == nblk - 1 else lower
        blocks.append((j * nq, keys, (ok_left, ok_right)))
    _attn_blocks(sink_ref, q_ref, o_ref, blocks)


def _attn_ctx_kernel(sink_ref, q_ref, kx_ref, vx_ref, o_ref):
    _attn_blocks(sink_ref, q_ref, o_ref, [(0, _attn_key_parts(kx_ref[...], vx_ref[...]), None)])


def _attention_latent(sink, q, k, v, kc, vc):
    b, l, _ = q.shape
    lc = kc.shape[1]
    nq = ATTN_BLOCK
    nb = l // nq
    nblk = min(ATTN_BLOCKS_PER_STEP, nb)
    assert nb % nblk == 0
    nsteps = nb // nblk
    smem = pl.BlockSpec(memory_space=pltpu.SMEM)
    edge = lambda f: pl.BlockSpec((None, nq, KV_WIDTH), f)
    left = lambda bb, i: (bb, jnp.maximum(nblk * i - 1, 0), 0)
    cen = lambda bb, i: (bb, i, 0)
    right = lambda bb, i: (bb, jnp.minimum(nblk * i + nblk, nb - 1), 0)
    mid = pl.BlockSpec((None, nblk * nq, KV_WIDTH), cen)
    ctx = pl.BlockSpec((None, lc, KV_WIDTH), lambda bb, i: (bb, 0, 0))
    return pl.pallas_call(
        functools.partial(_attn_lat_kernel, nsteps=nsteps, nblk=nblk),
        grid=(b, nsteps),
        in_specs=[smem, pl.BlockSpec((None, nblk * nq, ATTN_WIDTH), cen),
                  edge(left), mid, edge(right), ctx,
                  edge(left), mid, edge(right), ctx],
        out_specs=pl.BlockSpec((None, nblk * nq, ATTN_WIDTH), cen),
        out_shape=jax.ShapeDtypeStruct((b, l, ATTN_WIDTH), BF16),
        compiler_params=_cparams(("arbitrary", "arbitrary")),
        name="attn_latent",
    )(sink, q, k, k, k, kc, v, v, v, vc)


def _attention_ctx(sink, qc, kc, vc):
    b, lc, _ = qc.shape
    nq = ATTN_BLOCK
    smem = pl.BlockSpec(memory_space=pltpu.SMEM)
    ctx = pl.BlockSpec((None, lc, KV_WIDTH), lambda bb, i: (bb, 0, 0))
    return pl.pallas_call(
        _attn_ctx_kernel,
        grid=(b, lc // nq),
        in_specs=[smem, pl.BlockSpec((None, nq, ATTN_WIDTH), lambda bb, i: (bb, i, 0)), ctx, ctx],
        out_specs=pl.BlockSpec((None, nq, ATTN_WIDTH), lambda bb, i: (bb, i, 0)),
        out_shape=jax.ShapeDtypeStruct((b, lc, ATTN_WIDTH), BF16),
        compiler_params=_cparams(("arbitrary", "arbitrary")),
        name="attn_ctx",
    )(sink, qc, kc, vc)


def _pool_band_matrices():
    r = np.arange(POOL_TILE)[:, None]
    a = np.arange(POOL_TILE + 2 * POOL_HALO)[None, :] - POOL_HALO
    mats = [((a >= r - w // 2) & (a < r + w - w // 2)) for w in POOL_WINDOWS]
    return jnp.asarray(np.concatenate(mats, axis=0).astype(np.float32), dtype=BF16)


def _pool_kernel(u_ref, a_ref, w_ref, sc_ref, o_ref, pad_ref, *, ls):
    halo = POOL_HALO
    zeros = jnp.zeros((halo, POOL_WIDTH), BF16)
    pad_ref[0:halo, :] = zeros
    pad_ref[halo + ls:2 * halo + ls, :] = zeros
    pad_ref[halo:halo + ls, :] = u_ref[...]
    lane = lax.broadcasted_iota(jnp.int32, (1, POOL_WIDTH), 1)
    grp = lane // POOL_GROUP
    wl = jnp.where(grp == 0, POOL_WINDOWS[0],
                   jnp.where(grp == 1, POOL_WINDOWS[1],
                             jnp.where(grp == 2, POOL_WINDOWS[2], POOL_WINDOWS[3])))
    half = wl // 2

    ntile = ls // POOL_TILE
    grp_tiles = min(8, ntile)
    assert ntile % grp_tiles == 0
    nw = len(POOL_WINDOWS)

    def body(gi, carry):
        t0s = [pl.multiple_of((gi * grp_tiles + j) * POOL_TILE, POOL_TILE) for j in range(grp_tiles)]
        sums = [_dot(a_ref[...], pad_ref[pl.ds(t0, POOL_TILE + 2 * halo), :]) for t0 in t0s]
        ds = []
        for t0, sm in zip(t0s, sums):
            acc = sm[0:POOL_TILE]
            for g in range(1, nw):
                acc = jnp.where(grp == g, sm[g * POOL_TILE:(g + 1) * POOL_TILE], acc)
            p = t0 + lax.broadcasted_iota(jnp.int32, (POOL_TILE, 1), 0)
            hi = jnp.minimum(p + (wl - half), ls)
            lo = jnp.maximum(p - half, 0)
            cnt = (hi - lo).astype(F32)
            ut = u_ref[pl.ds(t0, POOL_TILE), :].astype(F32)
            ds.append((acc / cnt - ut).astype(BF16))
        rows = pl.ds(t0s[0], grp_tiles * POOL_TILE)
        o_ref[rows, :] = (_dot(jnp.concatenate(ds, axis=0), w_ref[...]) * sc_ref[...]).astype(BF16)
        return carry

    lax.fori_loop(0, ntile // grp_tiles, body, 0)


def _pool(u, band, wblk, scale):
    b, ls, _ = u.shape
    return pl.pallas_call(
        functools.partial(_pool_kernel, ls=ls),
        grid=(b,),
        in_specs=[
            pl.BlockSpec((None, ls, POOL_WIDTH), lambda bb: (bb, 0, 0)),
            pl.BlockSpec(band.shape, lambda bb: (0, 0)),
            pl.BlockSpec((POOL_WIDTH, POOL_WIDTH), lambda bb: (0, 0)),
            pl.BlockSpec((1, POOL_WIDTH), lambda bb: (0, 0)),
        ],
        out_specs=pl.BlockSpec((None, ls, POOL_WIDTH), lambda bb: (bb, 0, 0)),
        out_shape=jax.ShapeDtypeStruct((b, ls, POOL_WIDTH), BF16),
        scratch_shapes=[pltpu.VMEM((ls + 2 * POOL_HALO, POOL_WIDTH), BF16)],
        compiler_params=_cparams(("arbitrary",)),
        name="pool",
    )(u, band, wblk, scale)


def _ret_kernel(lg_ref, rq_ref, rk_ref, rv_ref, rg_ref, s0f_ref, s0b_ref, gmat_ref,
                o_ref, sf_ref, sb_ref,
                kvf_ref, kvb_ref, sp_ref, dm_ref, tab_ref, *, nc):
    c_len = RET_CHUNK
    kw, vw = RET_KW, RET_WIDTH

    def per_head(idx, d):
        return jnp.where(idx == 0, lg_ref[d, 0],
                         jnp.where(idx == 1, lg_ref[d, 1],
                                   jnp.where(idx == 2, lg_ref[d, 2], lg_ref[d, 3])))

    hk = lax.broadcasted_iota(jnp.int32, (1, kw), 1) // RET_DK
    hv = lax.broadcasted_iota(jnp.int32, (1, vw), 1) // RET_DV

    @pl.when(pl.program_id(0) == 0)
    def _():
        n_col = lax.broadcasted_iota(jnp.int32, (c_len, 1), 0).astype(F32)
        lgk_f, lgk_b = per_head(hk, 0), per_head(hk, 1)
        tab_ref[0] = jnp.exp(lgk_f * (c_len - 1.0 - n_col))
        tab_ref[1] = jnp.exp(lgk_b * n_col)
        tab_ref[2] = jnp.exp(lgk_f * (n_col + 1.0))
        tab_ref[3] = jnp.exp(lgk_b * (c_len - n_col))
        hs = lax.broadcasted_iota(jnp.int32, (1, RET_HEADS * c_len), 1) // c_len
        m_idx = (lax.broadcasted_iota(jnp.int32, (c_len, RET_HEADS * c_len), 1) & (c_len - 1)).astype(F32)
        n_idx = lax.broadcasted_iota(jnp.int32, (c_len, RET_HEADS * c_len), 0).astype(F32)
        rel = n_idx - m_idx
        dm_ref[0] = jnp.where(rel >= 0, jnp.exp(per_head(hs, 0) * jnp.maximum(rel, 0.0)), 0.0)
        dm_ref[1] = jnp.where(rel <= 0, jnp.exp(per_head(hs, 1) * jnp.maximum(-rel, 0.0)), 0.0)

    bd = (lax.broadcasted_iota(jnp.int32, (kw, vw), 0) // RET_DK) == (lax.broadcasted_iota(jnp.int32, (kw, vw), 1) // RET_DV)
    hk_col = lax.broadcasted_iota(jnp.int32, (kw, 1), 0) // RET_DK
    cd_f = jnp.exp(per_head(hk_col, 0) * float(c_len))
    cd_b = jnp.exp(per_head(hk_col, 1) * float(c_len))

    def rows(c):
        return pl.ds(pl.multiple_of(c * c_len, c_len), c_len)

    grp = min(8, nc)
    assert nc % grp == 0

    def kv_body(gi, carry):
        cs = [gi * grp + j for j in range(grp)]
        ks = [rk_ref[rows(c), :].astype(F32) for c in cs]
        vs = [rv_ref[rows(c), :] for c in cs]
        kf = [(k * tab_ref[0]).astype(BF16) for k in ks]
        kb = [(k * tab_ref[1]).astype(BF16) for k in ks]
        pf = [_dot_tn(a, v) for a, v in zip(kf, vs)]
        pb = [_dot_tn(a, v) for a, v in zip(kb, vs)]
        for j, c in enumerate(cs):
            kvf_ref[c] = jnp.where(bd, pf[j], 0.0)
            kvb_ref[c] = jnp.where(bd, pb[j], 0.0)
        return carry

    lax.fori_loop(0, nc // grp, kv_body, 0)

    def scan_body(j, carry):
        s_f, s_b = carry
        cb = nc - 1 - j
        sp_ref[j, 0:kw, :] = s_f.astype(BF16)
        sp_ref[cb, kw:2 * kw, :] = s_b.astype(BF16)
        return cd_f * s_f + kvf_ref[j], cd_b * s_b + kvb_ref[cb]

    s_f, s_b = lax.fori_loop(0, nc, scan_body, (s0f_ref[...], s0b_ref[...]))
    sf_ref[...] = s_f
    sb_ref[...] = s_b

    def out_body(gi, carry):
        cs = [gi * grp + j for j in range(grp)]
        qs = [rq_ref[rows(c), :] for c in cs]
        ks = [rk_ref[rows(c), :] for c in cs]
        vs = [rv_ref[rows(c), :] for c in cs]
        zk, zv = jnp.zeros_like(ks[0]), jnp.zeros_like(vs[0])
        ksts = [jnp.concatenate([jnp.where(hk == h, k, zk) for h in range(RET_HEADS)], axis=0) for k in ks]
        vsts = [jnp.concatenate([jnp.where(hv == h, v, zv) for h in range(RET_HEADS)], axis=0) for v in vs]
        scs = [_dot_nt(q, kst) for q, kst in zip(qs, ksts)]
        p2s = [jnp.concatenate([(sc * dm_ref[0]).astype(BF16), (sc * dm_ref[1]).astype(BF16)], axis=0)
               for sc in scs]
        q2s = [jnp.concatenate([(q.astype(F32) * tab_ref[2]).astype(BF16),
                                (q.astype(F32) * tab_ref[3]).astype(BF16)], axis=1) for q in qs]
        o2s = [_dot(p2, vst) for p2, vst in zip(p2s, vsts)]
        ocs = [_dot(q2, sp_ref[c]) for q2, c in zip(q2s, cs)]
        o = jnp.concatenate([o2[0:c_len] + o2[c_len:] + oc for o2, oc in zip(o2s, ocs)], axis=0)
        mu = _dot(o.astype(BF16), gmat_ref[...])
        d = o - mu
        var = _dot((d * d).astype(BF16), gmat_ref[...])
        hn = d * lax.rsqrt(var + LN_EPS)
        grows = pl.ds(pl.multiple_of(gi * (grp * c_len), grp * c_len), grp * c_len)
        g = rg_ref[grows, :].astype(F32)
        o_ref[grows, :] = (jax.nn.silu(g) * hn).astype(BF16)
        return carry

    lax.fori_loop(0, nc // grp, out_body, 0)


def _retention(lg, rq, rk, rv, rg, s0f, s0b, gmat):
    b, ls, _ = rq.shape
    nc = ls // RET_CHUNK
    seq = lambda w: pl.BlockSpec((None, ls, w), lambda bb: (bb, 0, 0))
    st = pl.BlockSpec((None, RET_KW, RET_WIDTH), lambda bb: (bb, 0, 0))
    return pl.pallas_call(
        functools.partial(_ret_kernel, nc=nc),
        grid=(b,),
        in_specs=[pl.BlockSpec(memory_space=pltpu.SMEM), seq(RET_KW), seq(RET_KW), seq(RET_WIDTH), seq(RET_WIDTH),
                  st, st, pl.BlockSpec((RET_WIDTH, RET_WIDTH), lambda bb: (0, 0))],
        out_specs=[seq(RET_WIDTH), st, st],
        out_shape=[jax.ShapeDtypeStruct((b, ls, RET_WIDTH), BF16),
                   jax.ShapeDtypeStruct((b, RET_KW, RET_WIDTH), F32),
                   jax.ShapeDtypeStruct((b, RET_KW, RET_WIDTH), F32)],
        scratch_shapes=[
            pltpu.VMEM((nc, RET_KW, RET_WIDTH), F32),
            pltpu.VMEM((nc, RET_KW, RET_WIDTH), F32),
            pltpu.VMEM((nc, 2 * RET_KW, RET_WIDTH), BF16),
            pltpu.VMEM((2, RET_CHUNK, RET_HEADS * RET_CHUNK), F32),
            pltpu.VMEM((4, RET_CHUNK, RET_KW), F32),
        ],
        compiler_params=_cparams(("arbitrary",)),
        name="retention",
    )(lg, rq, rk, rv, rg, s0f, s0b, gmat)


def _mixer_out_ln1(attn_ref, pool_ref, ret_ref, x_ref, m, w_ref, g_ref, b_ref, alpha):
    y = _dot(jnp.concatenate([attn_ref[...], pool_ref[...], ret_ref[...]], axis=1), w_ref[...])
    x1 = _layer_norm(alpha * x_ref[...] + m[2:3, :] * y, g_ref[...], b_ref[...])
    return x1, x1 * (1.0 + m[4:5, :]) + m[3:4, :]


def _route_tile(logits_t, tri_ref, carry_ref):
    ne, tm = logits_t.shape
    row = lax.broadcasted_iota(jnp.int32, (ne, tm), 0)
    l = jnp.where(row < N_EXPERTS, logits_t, -jnp.inf)
    m1 = jnp.max(l, axis=0, keepdims=True)
    i1 = jnp.min(jnp.where(l == m1, row, ne), axis=0, keepdims=True)
    l2 = jnp.where(row == i1, -jnp.inf, l)
    m2 = jnp.max(l2, axis=0, keepdims=True)
    i2 = jnp.min(jnp.where(l2 == m2, row, ne), axis=0, keepdims=True)
    e = jnp.exp(m2 - m1)
    w1 = 1.0 / (1.0 + e)
    w2 = e / (1.0 + e)
    oh = jnp.where((row == i1) | (row == i2), 1.0, 0.0)
    carry = carry_ref[:, 0:1]
    cum = _dot(oh.astype(BF16), tri_ref[...]) + carry
    r1 = jnp.sum(jnp.where(row == i1, cum, 0.0), axis=0, keepdims=True)
    r2 = jnp.sum(jnp.where(row == i2, cum, 0.0), axis=0, keepdims=True)
    carry_ref[...] = jnp.broadcast_to(carry + jnp.sum(oh, axis=1, keepdims=True), carry_ref.shape)
    r8 = lax.broadcasted_iota(jnp.int32, (8, tm), 0)
    return jnp.where(r8 == 0, i1.astype(F32),
                     jnp.where(r8 == 1, i2.astype(F32),
                               jnp.where(r8 == 2, r1,
                                         jnp.where(r8 == 3, r2,
                                                   jnp.where(r8 == 4, w1,
                                                             jnp.where(r8 == 5, w2, 0.0))))))


def _outproj_kernel(attn_ref, pool_ref, ret_ref, x_ref, mod_ref, w_ref, g_ref, b_ref, router_ref, tri_ref,
                    x1_ref, t_ref, tab_ref, tabt_ref, cnt_ref, carry_ref, *, alpha):
    @pl.when((pl.program_id(0) == 0) & (pl.program_id(1) == 0))
    def _():
        carry_ref[...] = jnp.zeros_like(carry_ref)

    x1, t = _mixer_out_ln1(attn_ref, pool_ref, ret_ref, x_ref, mod_ref[...], w_ref, g_ref, b_ref, alpha)
    x1_ref[...] = x1
    _store_tile_rows(t_ref, (), t)
    tab_t = _route_tile(_dot_nt(router_ref[...], t.astype(BF16)), tri_ref, carry_ref)
    tabt_ref[...] = tab_t
    tm = tab_t.shape[1]
    tab_ref[...] = jnp.concatenate([tab_t, jnp.zeros((LANES - 8, tm), F32)], axis=0).T
    cnt_ref[...] = carry_ref[0:8, :]


def _mixer_row_specs(tm, d):
    row = lambda w: pl.BlockSpec((None, tm, w), lambda bb, j: (bb, j, 0))
    return [row(ATTN_WIDTH), row(POOL_WIDTH), row(RET_WIDTH), row(d)]


def _outproj_ln_route(attn, pool, ret, x, modtab, mod_row_fn, w_out, ln_g, ln_b, alpha, router):
    b, ls, d = x.shape
    tm = min(ROUTE_TILE, ls)
    nt = ls // tm
    tri = jnp.asarray(np.triu(np.ones((tm, tm), np.float32), 1), dtype=BF16)
    row = lambda w: pl.BlockSpec((None, tm, w), lambda bb, j: (bb, j, 0))
    const = lambda shape: pl.BlockSpec(shape, lambda bb, j: tuple(0 for _ in shape))
    return pl.pallas_call(
        functools.partial(_outproj_kernel, alpha=alpha),
        grid=(b, nt),
        in_specs=_mixer_row_specs(tm, d) + [
            pl.BlockSpec((None, 6, d), lambda bb, j: (mod_row_fn(bb), 0, 0)),
            const((d, d)), const((1, d)), const((1, d)), const((ROUTER_ROWS, d)), const((tm, tm))],
        out_specs=[row(d), pl.BlockSpec((None, tm * SUBLANES, LANES), lambda bb, j: (bb, j, 0)), row(LANES),
                   pl.BlockSpec((8, tm), lambda bb, j: (0, bb * nt + j)), const((8, LANES))],
        out_shape=[jax.ShapeDtypeStruct((b, ls, d), F32),
                   jax.ShapeDtypeStruct((b, ls * SUBLANES, LANES), F32),
                   jax.ShapeDtypeStruct((b, ls, LANES), F32),
                   jax.ShapeDtypeStruct((8, b * ls), F32),
                   jax.ShapeDtypeStruct((8, LANES), F32)],
        scratch_shapes=[pltpu.VMEM((ROUTER_ROWS, LANES), F32)],
        compiler_params=_cparams(("arbitrary", "arbitrary")),
        name="outproj_ln_route",
    )(attn, pool, ret, x, modtab, w_out, ln_g, ln_b, router, tri)


N_FFN_CHUNKS = FFN_DIM // FFN_CHUNK
N_FFN_GAPS = 3 * N_FFN_CHUNKS


def _swiglu_rows(tb, wg_ref, wu_ref, wd_ref, between=None):
    gap = (lambda j: None) if between is None else between
    acts = []
    for c in range(N_FFN_CHUNKS):
        sl = slice(c * FFN_CHUNK, (c + 1) * FFN_CHUNK)
        g = _dot(tb, wg_ref[:, sl])
        gap(3 * c)
        u = _dot(tb, wu_ref[:, sl])
        gap(3 * c + 1)
        acts.append((jax.nn.silu(g) * u).astype(BF16))
        gap(3 * c + 2)
    return _dot(jnp.concatenate(acts, axis=1), wd_ref[...])


def _dense_post_kernel(attn_ref, pool_ref, ret_ref, x_ref, mod_ref, w_ref, g1_ref, b1_ref,
                       wg_ref, wu_ref, wd_ref, g2_ref, b2_ref, o_ref, *, alpha):
    m = mod_ref[...]
    x1, t = _mixer_out_ln1(attn_ref, pool_ref, ret_ref, x_ref, m, w_ref, g1_ref, b1_ref, alpha)
    f = _swiglu_rows(t.astype(BF16), wg_ref, wu_ref, wd_ref)
    o_ref[...] = _layer_norm(alpha * x1 + m[5:6, :] * f, g2_ref[...], b2_ref[...])


def _dense_post(attn, pool, ret, x, modtab, mod_row_fn, w_out, g1, b1, wg, wu, wd, g2, b2, alpha):
    b, ls, d = x.shape
    tm = min(512, ls)
    nt = ls // tm
    const = lambda shape: pl.BlockSpec(shape, lambda bb, j: tuple(0 for _ in shape),
                                       pipeline_mode=pl.Buffered(1))
    return pl.pallas_call(
        functools.partial(_dense_post_kernel, alpha=alpha),
        grid=(b, nt),
        in_specs=_mixer_row_specs(tm, d) + [
            pl.BlockSpec((None, 6, d), lambda bb, j: (mod_row_fn(bb), 0, 0)),
            const((d, d)), const((1, d)), const((1, d)),
            const((d, FFN_DIM)), const((d, FFN_DIM)), const((FFN_DIM, d)),
            const((1, d)), const((1, d))],
        out_specs=pl.BlockSpec((None, tm, d), lambda bb, j: (bb, j, 0)),
        out_shape=jax.ShapeDtypeStruct((b, ls, d), F32),
        compiler_params=_cparams(("arbitrary", "arbitrary")),
        name="dense_post",
    )(attn, pool, ret, x, modtab, w_out, g1, b1, wg, wu, wd, g2, b2)


def _invert_kernel(p1_ref, p2_ref, lo_ref, hi_ref, dst_ref, *, n):
    for e in range(N_EXPERTS + 1):
        def fill(s, carry):
            dst_ref[s] = 2 * n + (s & (MOE_TILE - 1))
            return carry

        lax.fori_loop(lo_ref[e], hi_ref[e], fill, 0)

    def body(t, carry):
        dst_ref[p1_ref[t]] = t
        dst_ref[p2_ref[t]] = n + t
        return carry

    lax.fori_loop(0, n, body, 0, unroll=16)


def _invert(pos1, pos2, fill_lo, fill_hi, nslots):
    n = pos1.shape[0]
    smem = pl.BlockSpec(memory_space=pltpu.SMEM)
    return pl.pallas_call(
        functools.partial(_invert_kernel, n=n),
        in_specs=[smem, smem, smem, smem],
        out_specs=smem,
        out_shape=jax.ShapeDtypeStruct((nslots,), jnp.int32),
        name="invert",
    )(pos1, pos2, fill_lo, fill_hi)


def _slot_source_row(v, n):
    if n & (n - 1) == 0:
        return v & (n - 1)
    return jnp.where(v >= 2 * n, v - 2 * n, jnp.where(v >= n, v - n, v))


def _moe_kernel(te_ref, nt_ref, inv_ref, t_ref, wg_ref, wu_ref, wd_ref, y_ref, xbuf, obuf, gsem, ssem, *, n):
    del te_ref
    i = pl.program_id(0)
    nt = nt_ref[0]
    tm = MOE_TILE
    slot = i & 1
    other = 1 - slot

    def row_tile(r):
        return pl.ds(pl.multiple_of(r * SUBLANES, SUBLANES), SUBLANES)

    def gather_row(tile, r, buf):
        src = _slot_source_row(inv_ref[tile * tm + r], n)
        pltpu.make_async_copy(t_ref.at[row_tile(src)], xbuf.at[buf, row_tile(r)], gsem.at[buf]).start()

    def scatter_row(tile, r, buf, real):
        dst = jnp.where(real, inv_ref[tile * tm + r], 2 * n + r)
        pltpu.make_async_copy(obuf.at[buf, row_tile(r)], y_ref.at[row_tile(dst)], ssem.at[buf]).start()

    def wait_gather(buf):
        pltpu.make_async_copy(t_ref.at[pl.ds(0, tm * SUBLANES)], xbuf.at[buf], gsem.at[buf]).wait()

    def wait_scatter(buf):
        pltpu.make_async_copy(obuf.at[buf], y_ref.at[pl.ds(0, tm * SUBLANES)], ssem.at[buf]).wait()

    @pl.when(i == 0)
    def _():
        obuf[1] = jnp.zeros(obuf.shape[1:], obuf.dtype)

        def first(r, carry):
            gather_row(0, r, 0)
            return carry

        lax.fori_loop(0, tm, first, 0, unroll=8)

    def main(cur, oth):
        wait_gather(cur)

        @pl.when(i >= 1)
        def _():
            wait_scatter(cur)

        nxt = jnp.minimum(i + 1, nt - 1)
        prv = jnp.maximum(i - 1, 0)
        real = i >= 1
        n_issue_gaps = (4 * N_FFN_GAPS) // 5
        n_gather_gaps = n_issue_gaps // 2
        per_g = -(-tm // n_gather_gaps)
        per_s = -(-tm // (n_issue_gaps - n_gather_gaps))

        def between(j):
            if j < n_gather_gaps:
                for r in range(j * per_g, min((j + 1) * per_g, tm)):
                    gather_row(nxt, r, oth)
            elif j < n_issue_gaps:
                k = j - n_gather_gaps
                for r in range(k * per_s, min((k + 1) * per_s, tm)):
                    scatter_row(prv, r, oth, real)

        x = _load_tile_rows(xbuf, (cur,), tm).astype(BF16)
        _store_tile_rows(obuf, (cur,), _swiglu_rows(x, wg_ref, wu_ref, wd_ref, between))

    @pl.when((i < nt) & (slot == 0))
    def _():
        main(0, 1)

    @pl.when((i < nt) & (slot == 1))
    def _():
        main(1, 0)

    @pl.when(i == nt)
    def _():
        last = nt - 1
        buf = last & 1
        wait_scatter(nt & 1)

        def tail(r, carry):
            scatter_row(last, r, buf, True)
            return carry

        lax.fori_loop(0, tm, tail, 0, unroll=8)
        wait_scatter(buf)
        wait_gather(nt & 1)


def _moe_ffn(tile_expert, ntiles, inv, t, wg, wu, wd):
    n = t.shape[0] // SUBLANES
    d = D_MODEL
    tm = MOE_TILE
    assert n >= tm
    nt_max = inv.shape[0] // tm
    wspec = lambda shape: pl.BlockSpec((None,) + shape, lambda i, te, nt, iv: (te[i], 0, 0))
    return pl.pallas_call(
        functools.partial(_moe_kernel, n=n),
        grid_spec=pltpu.PrefetchScalarGridSpec(
            num_scalar_prefetch=3,
            grid=(nt_max,),
            in_specs=[pl.BlockSpec(memory_space=pl.ANY),
                      wspec((d, FFN_DIM)), wspec((d, FFN_DIM)), wspec((FFN_DIM, d))],
            out_specs=pl.BlockSpec(memory_space=pl.ANY),
            scratch_shapes=[pltpu.VMEM((2, tm * SUBLANES, LANES), F32), pltpu.VMEM((2, tm * SUBLANES, LANES), F32),
                            pltpu.SemaphoreType.DMA((2,)), pltpu.SemaphoreType.DMA((2,))],
        ),
        out_shape=jax.ShapeDtypeStruct(((2 * n + tm) * SUBLANES, LANES), F32),
        compiler_params=_cparams(("arbitrary",)),
        name="moe_ffn",
    )(tile_expert, ntiles, inv, t, wg, wu, wd)


def _combine_kernel(y1_ref, y2_ref, x1_ref, rt_ref, mod_ref, g_ref, b_ref, o_ref, *, alpha):
    rt = rt_ref[...]
    tm = x1_ref.shape[0]
    f = rt[:, 4:5] * _load_tile_rows(y1_ref, (), tm) + rt[:, 5:6] * _load_tile_rows(y2_ref, (), tm)
    m = mod_ref[...]
    o_ref[...] = _layer_norm(alpha * x1_ref[...] + m[5:6, :] * f, g_ref[...], b_ref[...])


def _combine_ln(y, x1, rtab, modtab, l, ln_g, ln_b, alpha):
    n, d = x1.shape
    tm = min(ROW_TILE, n)
    per_b = l // tm
    nblk = n // tm
    const = lambda shape: pl.BlockSpec(shape, lambda i: tuple(0 for _ in shape))
    return pl.pallas_call(
        functools.partial(_combine_kernel, alpha=alpha),
        grid=(nblk,),
        in_specs=[pl.BlockSpec((tm * SUBLANES, LANES), lambda i: (i, 0)),
                  pl.BlockSpec((tm * SUBLANES, LANES), lambda i: (nblk + i, 0)),
                  pl.BlockSpec((tm, d), lambda i: (i, 0)),
                  pl.BlockSpec((tm, LANES), lambda i: (i, 0)),
                  pl.BlockSpec((None, 6, d), lambda i: (i // per_b, 0, 0)),
                  const((1, d)), const((1, d))],
        out_specs=pl.BlockSpec((tm, d), lambda i: (i, 0)),
        out_shape=jax.ShapeDtypeStruct((n, d), F32),
        compiler_params=_cparams(("arbitrary",)),
        name="combine_ln",
    )(y, y, x1, rtab, modtab, ln_g, ln_b)


def _moe_layer(t, rtab, rtab_t, cnt, x1, modtab, l, wg, wu, wd, ln_g, ln_b, alpha):
    n = x1.shape[0]
    counts = cnt[:N_EXPERTS, 0].astype(jnp.int32)
    tiles_e = (counts + MOE_TILE - 1) // MOE_TILE
    tile_end = jnp.cumsum(tiles_e)
    tile_start = tile_end - tiles_e
    ntiles = tile_end[-1:]
    nt_max = (2 * n) // MOE_TILE + N_EXPERTS + 1
    tile_expert = jnp.minimum(
        jnp.sum((jnp.arange(nt_max)[:, None] >= tile_end[None, :]).astype(jnp.int32), axis=1),
        N_EXPERTS - 1).astype(jnp.int32)
    e1 = rtab_t[0].astype(jnp.int32)
    e2 = rtab_t[1].astype(jnp.int32)
    row_start = tile_start * MOE_TILE
    pos1 = row_start[e1] + rtab_t[2].astype(jnp.int32)
    pos2 = row_start[e2] + rtab_t[3].astype(jnp.int32)
    fill_lo = jnp.concatenate([row_start + counts, tile_end[-1:] * MOE_TILE]).astype(jnp.int32)
    fill_hi = jnp.concatenate([tile_end * MOE_TILE, jnp.full((1,), nt_max * MOE_TILE)]).astype(jnp.int32)
    inv = _invert(pos1, pos2, fill_lo, fill_hi, nt_max * MOE_TILE)
    y = _moe_ffn(tile_expert, ntiles.astype(jnp.int32), inv, t, wg, wu, wd)
    return _combine_ln(y, x1, rtab, modtab, l, ln_g, ln_b, alpha)


def _prep_w_in(w):
    q = w[:, 0:ATTN_WIDTH] * (HEAD_DIM ** -0.5 * LOG2E)
    return jnp.concatenate([q, w[:, ATTN_WIDTH:]], axis=1).astype(BF16)


def _block_diag(mats):
    n = len(mats)
    rows = []
    for i, m in enumerate(mats):
        rows.append(jnp.concatenate([m if j == i else jnp.zeros_like(m) for j in range(n)], axis=1))
    return jnp.concatenate(rows, axis=0)


def kernel(x, c, ctx, c_ctx, w_mod, b_mod, w_in, attn_sink, pool_w, pool_scale, ret_log_decay_fwd,
           ret_log_decay_bwd, w_out, ln1_g, ln1_b, ln2_g, ln2_b, ffn_w_gate, ffn_w_up, ffn_w_down,
           moe_router, moe_w_gate, moe_w_up, moe_w_down):
    b, l, d = x.shape
    lc = ctx.shape[1]
    depth = w_in.shape[0]
    alpha = (2.0 * depth) ** 0.25
    assert d == D_MODEL and l % 256 == 0 and lc % 256 == 0 and l % GRID_W == 0

    n_rows = ((b + 1 + 7) // 8) * 8
    c_all = jnp.concatenate([c, c_ctx[None, :], jnp.zeros((n_rows - b - 1, d), F32)], axis=0)
    mod_all = _modulation(c_all, w_mod, b_mod).reshape(depth, n_rows, 6, d)
    lat_row = lambda bb: bb
    ctx_row = lambda bb: b

    rope_tabs = _rope_tables(l)
    band = _pool_band_matrices()
    gmat = jnp.asarray(np.kron(np.eye(RET_HEADS), np.ones((RET_DV, RET_DV))) / RET_DV, dtype=BF16)
    zero_state = jnp.zeros((b, RET_KW, RET_WIDTH), F32)

    xc = ctx
    for i in range(depth):
        last = i == depth - 1
        modtab = mod_all[i]
        w2 = _prep_w_in(w_in[i])
        wo = w_out[i].astype(BF16)
        sink = attn_sink[i].astype(F32)
        pw = _block_diag([pool_w[i, g] for g in range(len(POOL_WINDOWS))]).astype(BF16)
        ps = pool_scale[i].reshape(1, POOL_WIDTH).astype(F32)
        lg = jnp.stack([ret_log_decay_fwd[i], ret_log_decay_bwd[i]]).astype(F32)
        g1, b1 = ln1_g[i].reshape(1, d), ln1_b[i].reshape(1, d)
        g2, b2 = ln2_g[i].reshape(1, d), ln2_b[i].reshape(1, d)

        qc, kc, vc, uc, rqc, rkc, rvc, rgc = _inproj(xc, modtab, ctx_row, w2, None)
        ret_c, s_f, s_b = _retention(lg, rqc, rkc, rvc, rgc, zero_state, zero_state, gmat)

        q, k, v, u, rq, rk, rv, rg = _inproj(x, modtab, lat_row, w2, rope_tabs)
        attn = _attention_latent(sink, q, k, v, kc, vc)
        pool = _pool(u, band, pw, ps)
        ret, _, _ = _retention(lg, rq, rk, rv, rg, s_f, s_b, gmat)

        j = i // 2
        if i % 2 == 0:
            wg, wu, wd = ffn_w_gate[j].astype(BF16), ffn_w_up[j].astype(BF16), ffn_w_down[j].astype(BF16)
            x = _dense_post(attn, pool, ret, x, modtab, lat_row, wo, g1, b1, wg, wu, wd, g2, b2, alpha)
        else:
            wg, wu, wd = moe_w_gate[j].astype(BF16), moe_w_up[j].astype(BF16), moe_w_down[j].astype(BF16)
            router = jnp.pad(moe_router[j].T, ((0, ROUTER_ROWS - N_EXPERTS), (0, 0))).astype(BF16)
            x1, t, rtab, rtab_t, cnt = _outproj_ln_route(attn, pool, ret, x, modtab, lat_row, wo, g1, b1, alpha,
                                                         router)
            x = _moe_layer(t.reshape(b * l * SUBLANES, LANES), rtab.reshape(b * l, LANES), rtab_t, cnt,
                           x1.reshape(b * l, d), modtab, l, wg, wu, wd, g2, b2, alpha).reshape(b, l, d)

        if not last:
            attn_c = _attention_ctx(sink, qc, kc, vc)
            pool_c = _pool(uc, band, pw, ps)
            if i % 2 == 0:
                xc = _dense_post(attn_c, pool_c, ret_c, xc, modtab, ctx_row, wo, g1, b1, wg, wu, wd, g2, b2,
                                 alpha)
            else:
                x1c, tc, rtab_c, rtab_tc, cnt_c = _outproj_ln_route(attn_c, pool_c, ret_c, xc, modtab, ctx_row, wo,
                                                                    g1, b1, alpha, router)
                ctx_mod = jnp.broadcast_to(modtab[b:b + 1], (b, 6, d))
                xc = _moe_layer(tc.reshape(b * lc * SUBLANES, LANES), rtab_c.reshape(b * lc, LANES), rtab_tc,
                                cnt_c, x1c.reshape(b * lc, d), ctx_mod, lc, wg, wu, wd, g2, b2,
                                alpha).reshape(b, lc, d)
    return x
```

```python
import functools
import math

import numpy as np
import jax
import jax.numpy as jnp
from jax import lax
from jax.experimental import pallas as pl
from jax.experimental.pallas import tpu as pltpu

F32 = jnp.float32
BF16 = jnp.bfloat16

D_MODEL = 1024
GRID_W = 64
HEAD_DIM = 64
N_HEADS = 8
N_KV_HEADS = 2
ATTN_WIDTH = N_HEADS * HEAD_DIM
ATTN_BLOCK = 128
ATTN_BLOCKS_PER_STEP = 8
ROPE_BASE = 10000.0
POOL_WINDOWS = (2, 4, 8, 16)
POOL_WIDTH = 256
POOL_GROUP = 64
POOL_TILE = 256
POOL_HALO = 16
RET_HEADS = 4
RET_DK = 32
RET_DV = 64
RET_WIDTH = 256
RET_KW = RET_HEADS * RET_DK
RET_CHUNK = 128
FFN_DIM = 2816
FFN_CHUNK = 256
N_EXPERTS = 8
ROUTER_ROWS = 16
ROUTE_TILE = 512
MOE_TILE = 512
ROW_TILE = 512
LN_EPS = 1e-5
NEG_INF = -1e30
LOG2E = math.log2(math.e)
LANES = 128
IN_WIDTH = 1792

VMEM_LIMIT = 56 * 1024 * 1024


def _cparams(sem):
    return pltpu.CompilerParams(dimension_semantics=sem, vmem_limit_bytes=VMEM_LIMIT)


def _dot(a, b):
    return jnp.dot(a, b, preferred_element_type=F32)


def _dot_nt(a, b):
    return lax.dot_general(a, b, (((1,), (1,)), ((), ())), preferred_element_type=F32)


def _dot_tn(a, b):
    return lax.dot_general(a, b, (((0,), (0,)), ((), ())), preferred_element_type=F32)


def _split_bf16(a):
    hi = a.astype(BF16)
    lo = (a - hi.astype(F32)).astype(BF16)
    return hi, lo


def _dot3(a, b):
    ah, al = _split_bf16(a)
    bh, bl = _split_bf16(b)
    return _dot(ah, bh) + _dot(ah, bl) + _dot(al, bh)


SUBLANES = 8
ROW_TILES = D_MODEL // LANES


def _store_tile_rows(ref, idx, val):
    rows = val.shape[0]
    for c in range(ROW_TILES):
        ref[idx + (pl.ds(c, rows, stride=SUBLANES), slice(None))] = val[:, c * LANES:(c + 1) * LANES]


def _load_tile_rows(ref, idx, rows):
    return jnp.concatenate([ref[idx + (pl.ds(c, rows, stride=SUBLANES), slice(None))]
                            for c in range(ROW_TILES)], axis=1)


def _layer_norm(z, g, b):
    mu = jnp.mean(z, axis=-1, keepdims=True)
    d = z - mu
    var = jnp.mean(d * d, axis=-1, keepdims=True)
    return d * lax.rsqrt(var + LN_EPS) * g + b


def _mod_kernel(c_ref, w_ref, b_ref, o_ref):
    s = jax.nn.silu(c_ref[...])
    o_ref[...] = _dot3(s, w_ref[...]) + b_ref[...]


def _modulation(c_all, w_mod, b_mod):
    depth, d, n6 = w_mod.shape
    r = c_all.shape[0]
    bn = 1536
    return pl.pallas_call(
        _mod_kernel,
        grid=(depth, n6 // bn),
        in_specs=[
            pl.BlockSpec((r, d), lambda i, j: (0, 0)),
            pl.BlockSpec((None, d, bn), lambda i, j: (i, 0, j)),
            pl.BlockSpec((None, 1, bn), lambda i, j: (i, 0, j)),
        ],
        out_specs=pl.BlockSpec((None, r, bn), lambda i, j: (i, 0, j)),
        out_shape=jax.ShapeDtypeStruct((depth, r, n6), F32),
        compiler_params=_cparams(("arbitrary", "arbitrary")),
        name="modulation",
    )(c_all, w_mod, b_mod.reshape(depth, 1, n6))


KV_WIDTH = N_KV_HEADS * HEAD_DIM
_IN_OUT_WIDTHS = (ATTN_WIDTH, KV_WIDTH, KV_WIDTH, POOL_WIDTH, RET_KW, RET_KW, RET_WIDTH, RET_WIDTH)
_IN_COL_STARTS = tuple(int(c) for c in np.cumsum((0,) + _IN_OUT_WIDTHS))


def _rope_groups(a, cos, s_prev, s_next):
    outs = []
    for g in range(a.shape[1] // LANES):
        ag = a[:, g * LANES:(g + 1) * LANES]
        outs.append(ag * cos + pltpu.roll(ag, 16, 1) * s_prev + pltpu.roll(ag, LANES - 16, 1) * s_next)
    return outs[0] if len(outs) == 1 else jnp.concatenate(outs, axis=1)


def _inproj_kernel(*refs, rope):
    if rope:
        x_ref, mod_ref, w_ref, cos_ref, sp_ref, sn_ref = refs[:6]
        outs = refs[6:]
    else:
        x_ref, mod_ref, w_ref = refs[:3]
        outs = refs[3:]
    q_ref, k_ref, v_ref, u_ref, rq_ref, rk_ref, rv_ref, rg_ref = outs
    m = mod_ref[...]
    h = (x_ref[...] * (1.0 + m[1:2, :]) + m[0:1, :]).astype(BF16)

    def mm(j0, j1):
        return _dot(h, w_ref[:, _IN_COL_STARTS[j0]:_IN_COL_STARTS[j1]])

    q = mm(0, 1)
    kv = mm(1, 3)
    k, v = kv[:, :KV_WIDTH], kv[:, KV_WIDTH:]
    if rope:
        cos, sp, sn = cos_ref[...], sp_ref[...], sn_ref[...]
        q = _rope_groups(q, cos, sp, sn)
        k = _rope_groups(k, cos, sp, sn)
    q_ref[...] = q.astype(BF16)
    k_ref[...] = k.astype(BF16)
    v_ref[...] = v.astype(BF16)
    u_ref[...] = mm(3, 4).astype(BF16)
    rqk = mm(4, 6)
    rq_ref[...] = rqk[:, :RET_KW].astype(BF16)
    rk_ref[...] = (rqk[:, RET_KW:] * (RET_DK ** -0.5)).astype(BF16)
    rv_ref[...] = mm(6, 7).astype(BF16)
    rg_ref[...] = mm(7, 8).astype(BF16)


def _inproj(x, modtab, mod_row_fn, w2, rope_tabs):
    b, ls, d = x.shape
    tm = min(1024, ls)
    nt = ls // tm
    rope = rope_tabs is not None
    in_specs = [
        pl.BlockSpec((None, tm, d), lambda j, bb: (bb, j, 0)),
        pl.BlockSpec((None, 6, d), lambda j, bb: (mod_row_fn(bb), 0, 0)),
        pl.BlockSpec((d, IN_WIDTH), lambda j, bb: (0, 0)),
    ]
    args = [x, modtab, w2]
    if rope:
        in_specs += [pl.BlockSpec((tm, LANES), lambda j, bb: (j, 0))] * 3
        args += list(rope_tabs)
    out_specs = [pl.BlockSpec((None, tm, w), lambda j, bb: (bb, j, 0)) for w in _IN_OUT_WIDTHS]
    out_shape = [jax.ShapeDtypeStruct((b, ls, w), BF16) for w in _IN_OUT_WIDTHS]
    return pl.pallas_call(
        functools.partial(_inproj_kernel, rope=rope),
        grid=(nt, b),
        in_specs=in_specs,
        out_specs=out_specs,
        out_shape=out_shape,
        compiler_params=_cparams(("arbitrary", "arbitrary")),
        name="inproj_rope" if rope else "inproj",
    )(*args)


def _rope_tables(l):
    p = np.arange(l)
    row = (p // GRID_W).astype(np.float64)
    col = (p % GRID_W).astype(np.float64)
    quarter = HEAD_DIM // 4
    inv = ROPE_BASE ** (-np.arange(quarter, dtype=np.float64) / quarter)
    lane = np.arange(LANES)
    j = lane % HEAD_DIM
    pos = np.where((j < HEAD_DIM // 2)[None, :], row[:, None], col[:, None])
    ang = pos * inv[j % quarter][None, :]
    cos, sin = np.cos(ang), np.sin(ang)
    second = ((lane % (2 * quarter)) >= quarter)[None, :]
    s_prev = np.where(second, sin, 0.0)
    s_next = np.where(second, 0.0, -sin)
    return tuple(jnp.asarray(a.astype(np.float32)) for a in (cos, s_prev, s_next))


def _attn_blocks(sink_ref, q_ref, o_ref, blocks):
    nq = ATTN_BLOCK
    lane = lax.broadcasted_iota(jnp.int32, (1, LANES), 1)
    rid = lax.broadcasted_iota(jnp.int32, (4, 1, 1), 0)
    chains = [(row0, kvh, keys[kvh], ok) for row0, keys, ok in blocks for kvh in range(N_KV_HEADS)]

    scores = []
    for row0, kvh, (k_lo, k_hi, _), _ in chains:
        rows = pl.ds(row0, nq)
        qg = jnp.concatenate([q_ref[rows, 256 * kvh:256 * kvh + LANES],
                              q_ref[rows, 256 * kvh + LANES:256 * kvh + 2 * LANES]], axis=0)
        scores.append(jnp.concatenate([_dot_nt(qg, k_lo), _dot_nt(qg, k_hi)], axis=0).reshape(4, nq, -1))

    probs, sink_terms = [], []
    for (row0, kvh, _, ok), s in zip(chains, scores):
        if ok is not None:
            ok_left, ok_right = ok
            s = jnp.concatenate([jnp.where(ok_left[None], s[:, :, 0:nq], NEG_INF), s[:, :, nq:2 * nq],
                                 jnp.where(ok_right[None], s[:, :, 2 * nq:3 * nq], NEG_INF), s[:, :, 3 * nq:]],
                                axis=2)
        h0 = 4 * kvh
        sk = jnp.where(rid == 0, sink_ref[h0],
                       jnp.where(rid == 1, sink_ref[h0 + 2],
                                 jnp.where(rid == 2, sink_ref[h0 + 1], sink_ref[h0 + 3]))) * LOG2E
        m = jnp.maximum(jnp.max(s, axis=2, keepdims=True), sk)
        probs.append(jnp.exp2((s - m).astype(BF16)).reshape(4 * nq, -1))
        sink_terms.append(jnp.exp2(sk - m).reshape(4 * nq, 1))

    outs = [_dot(e, v_ext) for e, (_, _, (_, _, v_ext), _) in zip(probs, chains)]

    for (row0, kvh, _, _), o, es in zip(chains, outs, sink_terms):
        rows = pl.ds(row0, nq)
        swapped = pltpu.roll(o, HEAD_DIM, 1)
        even = o[0:2 * nq] / (swapped[0:2 * nq] + es[0:2 * nq])
        odd = swapped[2 * nq:] / (o[2 * nq:] + es[2 * nq:])
        g0 = jnp.where(lane < HEAD_DIM, even[0:nq], odd[0:nq])
        g1 = jnp.where(lane < HEAD_DIM, even[nq:], odd[nq:])
        o_ref[rows, 256 * kvh:256 * kvh + LANES] = g0.astype(BF16)
        o_ref[rows, 256 * kvh + LANES:256 * kvh + 2 * LANES] = g1.astype(BF16)


def _attn_key_parts(k, v):
    lane = lax.broadcasted_iota(jnp.int32, (1, LANES), 1)
    low = lane < HEAD_DIM
    swap = lambda a: jnp.concatenate([a[:, HEAD_DIM:], a[:, :HEAD_DIM]], axis=1)
    ks, vs = swap(k), swap(v)
    zero, one = jnp.zeros_like(k), jnp.ones_like(v)
    return [(jnp.where(low, k, zero), jnp.where(low, zero, ks), jnp.where(low, v, one)),
            (jnp.where(low, ks, zero), jnp.where(low, zero, k), jnp.where(low, vs, one))]


def _attn_lat_kernel(sink_ref, q_ref, kl_ref, kc_ref, kr_ref, kx_ref,
                     vl_ref, vc_ref, vr_ref, vx_ref, o_ref, *, nsteps, nblk):
    i = pl.program_id(1)
    nq = ATTN_BLOCK
    rows = lax.broadcasted_iota(jnp.int32, (nq, nq), 0)
    cols = lax.broadcasted_iota(jnp.int32, (nq, nq), 1)
    upper = cols >= rows
    lower = cols <= rows
    parts = [_attn_key_parts(kr[...], vr[...])
             for kr, vr in ((kl_ref, vl_ref), (kc_ref, vc_ref), (kr_ref, vr_ref), (kx_ref, vx_ref))]
    per_head = []
    for kvh in range(N_KV_HEADS):
        left, cen, right, ctx = [p[kvh] for p in parts]
        own = [tuple(t[j * nq:(j + 1) * nq] for t in cen) for j in range(nblk)]
        per_head.append(([left] + own + [right], ctx))
    blocks = []
    for j in range(nblk):
        keys = [tuple(jnp.concatenate([p[t] for p in seq[j:j + 3]] + [ctx[t]], axis=0) for t in range(3))
                for seq, ctx in per_head]
        ok_left = upper & (i >= 1) if j == 0 else upper
        ok_right = lower & (i <= nsteps - 2) if j == nblk - 1 else lower
        blocks.append((j * nq, keys, (ok_left, ok_right)))
    _attn_blocks(sink_ref, q_ref, o_ref, blocks)


def _attn_ctx_kernel(sink_ref, q_ref, kx_ref, vx_ref, o_ref):
    _attn_blocks(sink_ref, q_ref, o_ref, [(0, _attn_key_parts(kx_ref[...], vx_ref[...]), None)])


def _attention_latent(sink, q, kd, vd, kdc, vdc):
    b, l, _ = q.shape
    lc = kdc.shape[1]
    nq = ATTN_BLOCK
    nb = l // nq
    nblk = min(ATTN_BLOCKS_PER_STEP, nb)
    assert nb % nblk == 0
    nsteps = nb // nblk
    smem = pl.BlockSpec(memory_space=pltpu.SMEM)
    edge = lambda f: pl.BlockSpec((None, nq, KV_WIDTH), f)
    left = lambda bb, i: (bb, jnp.maximum(nblk * i - 1, 0), 0)
    cen = lambda bb, i: (bb, i, 0)
    right = lambda bb, i: (bb, jnp.minimum(nblk * i + nblk, nb - 1), 0)
    mid = pl.BlockSpec((None, nblk * nq, KV_WIDTH), cen)
    ctx = pl.BlockSpec((None, lc, KV_WIDTH), lambda bb, i: (bb, 0, 0))
    return pl.pallas_call(
        functools.partial(_attn_lat_kernel, nsteps=nsteps, nblk=nblk),
        grid=(b, nsteps),
        in_specs=[smem, pl.BlockSpec((None, nblk * nq, ATTN_WIDTH), cen),
                  edge(left), mid, edge(right), ctx,
                  edge(left), mid, edge(right), ctx],
        out_specs=pl.BlockSpec((None, nblk * nq, ATTN_WIDTH), cen),
        out_shape=jax.ShapeDtypeStruct((b, l, ATTN_WIDTH), BF16),
        compiler_params=_cparams(("arbitrary", "arbitrary")),
        name="attn_latent",
    )(sink, q, kd, kd, kd, kdc, vd, vd, vd, vdc)


def _attention_ctx(sink, qc, kdc, vdc):
    b, lc, _ = qc.shape
    nq = ATTN_BLOCK
    smem = pl.BlockSpec(memory_space=pltpu.SMEM)
    ctx = pl.BlockSpec((None, lc, KV_WIDTH), lambda bb, i: (bb, 0, 0))
    return pl.pallas_call(
        _attn_ctx_kernel,
        grid=(b, lc // nq),
        in_specs=[smem, pl.BlockSpec((None, nq, ATTN_WIDTH), lambda bb, i: (bb, i, 0)), ctx, ctx],
        out_specs=pl.BlockSpec((None, nq, ATTN_WIDTH), lambda bb, i: (bb, i, 0)),
        out_shape=jax.ShapeDtypeStruct((b, lc, ATTN_WIDTH), BF16),
        compiler_params=_cparams(("arbitrary", "arbitrary")),
        name="attn_ctx",
    )(sink, qc, kdc, vdc)


def _pool_band_matrices():
    r = np.arange(POOL_TILE)[:, None]
    a = np.arange(POOL_TILE + 2 * POOL_HALO)[None, :] - POOL_HALO
    mats = [((a >= r - w // 2) & (a < r + w - w // 2)) for w in POOL_WINDOWS]
    return jnp.asarray(np.concatenate(mats, axis=0).astype(np.float32), dtype=BF16)


def _pool_kernel(u_ref, a_ref, w_ref, sc_ref, o_ref, pad_ref, *, ls):
    halo = POOL_HALO
    zeros = jnp.zeros((halo, POOL_WIDTH), BF16)
    pad_ref[0:halo, :] = zeros
    pad_ref[halo + ls:2 * halo + ls, :] = zeros
    pad_ref[halo:halo + ls, :] = u_ref[...]
    lane = lax.broadcasted_iota(jnp.int32, (1, POOL_WIDTH), 1)
    grp = lane // POOL_GROUP
    wl = jnp.where(grp == 0, POOL_WINDOWS[0],
                   jnp.where(grp == 1, POOL_WINDOWS[1],
                             jnp.where(grp == 2, POOL_WINDOWS[2], POOL_WINDOWS[3])))
    half = wl // 2

    ntile = ls // POOL_TILE
    grp_tiles = min(8, ntile)
    assert ntile % grp_tiles == 0
    nw = len(POOL_WINDOWS)

    def body(gi, carry):
        t0s = [pl.multiple_of((gi * grp_tiles + j) * POOL_TILE, POOL_TILE) for j in range(grp_tiles)]
        sums = [_dot(a_ref[...], pad_ref[pl.ds(t0, POOL_TILE + 2 * halo), :]) for t0 in t0s]
        ds = []
        for t0, sm in zip(t0s, sums):
            acc = sm[0:POOL_TILE]
            for g in range(1, nw):
                acc = jnp.where(grp == g, sm[g * POOL_TILE:(g + 1) * POOL_TILE], acc)
            p = t0 + lax.broadcasted_iota(jnp.int32, (POOL_TILE, 1), 0)
            hi = jnp.minimum(p + (wl - half), ls)
            lo = jnp.maximum(p - half, 0)
            cnt = (hi - lo).astype(F32)
            ut = u_ref[pl.ds(t0, POOL_TILE), :].astype(F32)
            ds.append((acc / cnt - ut).astype(BF16))
        rows = pl.ds(t0s[0], grp_tiles * POOL_TILE)
        o_ref[rows, :] = (_dot(jnp.concatenate(ds, axis=0), w_ref[...]) * sc_ref[...]).astype(BF16)
        return carry

    lax.fori_loop(0, ntile // grp_tiles, body, 0)


def _pool(u, band, wblk, scale):
    b, ls, _ = u.shape
    return pl.pallas_call(
        functools.partial(_pool_kernel, ls=ls),
        grid=(b,),
        in_specs=[
            pl.BlockSpec((None, ls, POOL_WIDTH), lambda bb: (bb, 0, 0)),
            pl.BlockSpec(band.shape, lambda bb: (0, 0)),
            pl.BlockSpec((POOL_WIDTH, POOL_WIDTH), lambda bb: (0, 0)),
            pl.BlockSpec((1, POOL_WIDTH), lambda bb: (0, 0)),
        ],
        out_specs=pl.BlockSpec((None, ls, POOL_WIDTH), lambda bb: (bb, 0, 0)),
        out_shape=jax.ShapeDtypeStruct((b, ls, POOL_WIDTH), BF16),
        scratch_shapes=[pltpu.VMEM((ls + 2 * POOL_HALO, POOL_WIDTH), BF16)],
        compiler_params=_cparams(("arbitrary",)),
        name="pool",
    )(u, band, wblk, scale)


def _ret_kernel(lg_ref, rq_ref, rk_ref, rv_ref, rg_ref, s0f_ref, s0b_ref, gmat_ref,
                o_ref, sf_ref, sb_ref,
                kvf_ref, kvb_ref, sp_ref, dm_ref, tab_ref, *, nc):
    c_len = RET_CHUNK
    kw, vw = RET_KW, RET_WIDTH

    def per_head(idx, d):
        return jnp.where(idx == 0, lg_ref[d, 0],
                         jnp.where(idx == 1, lg_ref[d, 1],
                                   jnp.where(idx == 2, lg_ref[d, 2], lg_ref[d, 3])))

    hk = lax.broadcasted_iota(jnp.int32, (1, kw), 1) // RET_DK
    hv = lax.broadcasted_iota(jnp.int32, (1, vw), 1) // RET_DV

    @pl.when(pl.program_id(0) == 0)
    def _():
        n_col = lax.broadcasted_iota(jnp.int32, (c_len, 1), 0).astype(F32)
        lgk_f, lgk_b = per_head(hk, 0), per_head(hk, 1)
        tab_ref[0] = jnp.exp(lgk_f * (c_len - 1.0 - n_col))
        tab_ref[1] = jnp.exp(lgk_b * n_col)
        tab_ref[2] = jnp.exp(lgk_f * (n_col + 1.0))
        tab_ref[3] = jnp.exp(lgk_b * (c_len - n_col))
        hs = lax.broadcasted_iota(jnp.int32, (1, RET_HEADS * c_len), 1) // c_len
        m_idx = (lax.broadcasted_iota(jnp.int32, (c_len, RET_HEADS * c_len), 1) & (c_len - 1)).astype(F32)
        n_idx = lax.broadcasted_iota(jnp.int32, (c_len, RET_HEADS * c_len), 0).astype(F32)
        rel = n_idx - m_idx
        dm_ref[0] = jnp.where(rel >= 0, jnp.exp(per_head(hs, 0) * jnp.maximum(rel, 0.0)), 0.0)
        dm_ref[1] = jnp.where(rel <= 0, jnp.exp(per_head(hs, 1) * jnp.maximum(-rel, 0.0)), 0.0)

    bd = (lax.broadcasted_iota(jnp.int32, (kw, vw), 0) // RET_DK) == (lax.broadcasted_iota(jnp.int32, (kw, vw), 1) // RET_DV)
    hk_col = lax.broadcasted_iota(jnp.int32, (kw, 1), 0) // RET_DK
    cd_f = jnp.exp(per_head(hk_col, 0) * float(c_len))
    cd_b = jnp.exp(per_head(hk_col, 1) * float(c_len))

    def rows(c):
        return pl.ds(pl.multiple_of(c * c_len, c_len), c_len)

    grp = min(8, nc)
    assert nc % grp == 0

    def kv_body(gi, carry):
        cs = [gi * grp + j for j in range(grp)]
        ks = [rk_ref[rows(c), :].astype(F32) for c in cs]
        vs = [rv_ref[rows(c), :] for c in cs]
        kf = [(k * tab_ref[0]).astype(BF16) for k in ks]
        kb = [(k * tab_ref[1]).astype(BF16) for k in ks]
        pf = [_dot_tn(a, v) for a, v in zip(kf, vs)]
        pb = [_dot_tn(a, v) for a, v in zip(kb, vs)]
        for j, c in enumerate(cs):
            kvf_ref[c] = jnp.where(bd, pf[j], 0.0)
            kvb_ref[c] = jnp.where(bd, pb[j], 0.0)
        return carry

    lax.fori_loop(0, nc // grp, kv_body, 0)

    def scan_body(j, carry):
        s_f, s_b = carry
        cb = nc - 1 - j
        sp_ref[j, 0:kw, :] = s_f.astype(BF16)
        sp_ref[cb, kw:2 * kw, :] = s_b.astype(BF16)
        return cd_f * s_f + kvf_ref[j], cd_b * s_b + kvb_ref[cb]

    s_f, s_b = lax.fori_loop(0, nc, scan_body, (s0f_ref[...], s0b_ref[...]))
    sf_ref[...] = s_f
    sb_ref[...] = s_b

    def out_body(gi, carry):
        cs = [gi * grp + j for j in range(grp)]
        qs = [rq_ref[rows(c), :] for c in cs]
        ks = [rk_ref[rows(c), :] for c in cs]
        vs = [rv_ref[rows(c), :] for c in cs]
        zk, zv = jnp.zeros_like(ks[0]), jnp.zeros_like(vs[0])
        ksts = [jnp.concatenate([jnp.where(hk == h, k, zk) for h in range(RET_HEADS)], axis=0) for k in ks]
        vsts = [jnp.concatenate([jnp.where(hv == h, v, zv) for h in range(RET_HEADS)], axis=0) for v in vs]
        scs = [_dot_nt(q, kst) for q, kst in zip(qs, ksts)]
        p2s = [jnp.concatenate([(sc * dm_ref[0]).astype(BF16), (sc * dm_ref[1]).astype(BF16)], axis=0)
               for sc in scs]
        q2s = [jnp.concatenate([(q.astype(F32) * tab_ref[2]).astype(BF16),
                                (q.astype(F32) * tab_ref[3]).astype(BF16)], axis=1) for q in qs]
        o2s = [_dot(p2, vst) for p2, vst in zip(p2s, vsts)]
        ocs = [_dot(q2, sp_ref[c]) for q2, c in zip(q2s, cs)]
        o = jnp.concatenate([o2[0:c_len] + o2[c_len:] + oc for o2, oc in zip(o2s, ocs)], axis=0)
        mu = _dot(o.astype(BF16), gmat_ref[...])
        d = o - mu
        var = _dot((d * d).astype(BF16), gmat_ref[...])
        hn = d * lax.rsqrt(var + LN_EPS)
        grows = pl.ds(pl.multiple_of(gi * (grp * c_len), grp * c_len), grp * c_len)
        g = rg_ref[grows, :].astype(F32)
        o_ref[grows, :] = (jax.nn.silu(g) * hn).astype(BF16)
        return carry

    lax.fori_loop(0, nc // grp, out_body, 0)


def _retention(lg, rq, rk, rv, rg, s0f, s0b, gmat):
    b, ls, _ = rq.shape
    nc = ls // RET_CHUNK
    seq = lambda w: pl.BlockSpec((None, ls, w), lambda bb: (bb, 0, 0))
    st = pl.BlockSpec((None, RET_KW, RET_WIDTH), lambda bb: (bb, 0, 0))
    return pl.pallas_call(
        functools.partial(_ret_kernel, nc=nc),
        grid=(b,),
        in_specs=[pl.BlockSpec(memory_space=pltpu.SMEM), seq(RET_KW), seq(RET_KW), seq(RET_WIDTH), seq(RET_WIDTH),
                  st, st, pl.BlockSpec((RET_WIDTH, RET_WIDTH), lambda bb: (0, 0))],
        out_specs=[seq(RET_WIDTH), st, st],
        out_shape=[jax.ShapeDtypeStruct((b, ls, RET_WIDTH), BF16),
                   jax.ShapeDtypeStruct((b, RET_KW, RET_WIDTH), F32),
                   jax.ShapeDtypeStruct((b, RET_KW, RET_WIDTH), F32)],
        scratch_shapes=[
            pltpu.VMEM((nc, RET_KW, RET_WIDTH), F32),
            pltpu.VMEM((nc, RET_KW, RET_WIDTH), F32),
            pltpu.VMEM((nc, 2 * RET_KW, RET_WIDTH), BF16),
            pltpu.VMEM((2, RET_CHUNK, RET_HEADS * RET_CHUNK), F32),
            pltpu.VMEM((4, RET_CHUNK, RET_KW), F32),
        ],
        compiler_params=_cparams(("arbitrary",)),
        name="retention",
    )(lg, rq, rk, rv, rg, s0f, s0b, gmat)


def _mixer_out_ln1(attn_ref, pool_ref, ret_ref, x_ref, m, w_ref, g_ref, b_ref, alpha):
    y = (_dot(attn_ref[...], w_ref[0:ATTN_WIDTH, :])
         + _dot(pool_ref[...], w_ref[ATTN_WIDTH:ATTN_WIDTH + POOL_WIDTH, :])
         + _dot(ret_ref[...], w_ref[ATTN_WIDTH + POOL_WIDTH:, :]))
    x1 = _layer_norm(alpha * x_ref[...] + m[2:3, :] * y, g_ref[...], b_ref[...])
    return x1, x1 * (1.0 + m[4:5, :]) + m[3:4, :]


def _route_tile(logits_t, tri_ref, carry_ref):
    ne, tm = logits_t.shape
    row = lax.broadcasted_iota(jnp.int32, (ne, tm), 0)
    l = jnp.where(row < N_EXPERTS, logits_t, -jnp.inf)
    m1 = jnp.max(l, axis=0, keepdims=True)
    i1 = jnp.min(jnp.where(l == m1, row, ne), axis=0, keepdims=True)
    l2 = jnp.where(row == i1, -jnp.inf, l)
    m2 = jnp.max(l2, axis=0, keepdims=True)
    i2 = jnp.min(jnp.where(l2 == m2, row, ne), axis=0, keepdims=True)
    e = jnp.exp(m2 - m1)
    w1 = 1.0 / (1.0 + e)
    w2 = e / (1.0 + e)
    oh = jnp.where((row == i1) | (row == i2), 1.0, 0.0)
    carry = carry_ref[:, 0:1]
    cum = _dot(oh.astype(BF16), tri_ref[...]) + carry
    r1 = jnp.sum(jnp.where(row == i1, cum, 0.0), axis=0, keepdims=True)
    r2 = jnp.sum(jnp.where(row == i2, cum, 0.0), axis=0, keepdims=True)
    carry_ref[...] = jnp.broadcast_to(carry + jnp.sum(oh, axis=1, keepdims=True), carry_ref.shape)
    r8 = lax.broadcasted_iota(jnp.int32, (8, tm), 0)
    return jnp.where(r8 == 0, i1.astype(F32),
                     jnp.where(r8 == 1, i2.astype(F32),
                               jnp.where(r8 == 2, r1,
                                         jnp.where(r8 == 3, r2,
                                                   jnp.where(r8 == 4, w1,
                                                             jnp.where(r8 == 5, w2, 0.0))))))


def _outproj_kernel(attn_ref, pool_ref, ret_ref, x_ref, mod_ref, w_ref, g_ref, b_ref, router_ref, tri_ref,
                    x1_ref, t_ref, tab_ref, tabt_ref, cnt_ref, carry_ref, *, alpha):
    @pl.when((pl.program_id(0) == 0) & (pl.program_id(1) == 0))
    def _():
        carry_ref[...] = jnp.zeros_like(carry_ref)

    x1, t = _mixer_out_ln1(attn_ref, pool_ref, ret_ref, x_ref, mod_ref[...], w_ref, g_ref, b_ref, alpha)
    x1_ref[...] = x1
    _store_tile_rows(t_ref, (), t)
    tab_t = _route_tile(_dot_nt(router_ref[...], t.astype(BF16)), tri_ref, carry_ref)
    tabt_ref[...] = tab_t
    tm = tab_t.shape[1]
    tab_ref[...] = jnp.concatenate([tab_t, jnp.zeros((LANES - 8, tm), F32)], axis=0).T
    cnt_ref[...] = carry_ref[0:8, :]


def _mixer_row_specs(tm, d):
    row = lambda w: pl.BlockSpec((None, tm, w), lambda bb, j: (bb, j, 0))
    return [row(ATTN_WIDTH), row(POOL_WIDTH), row(RET_WIDTH), row(d)]


def _outproj_ln_route(attn, pool, ret, x, modtab, mod_row_fn, w_out, ln_g, ln_b, alpha, router):
    b, ls, d = x.shape
    tm = min(ROUTE_TILE, ls)
    nt = ls // tm
    tri = jnp.asarray(np.triu(np.ones((tm, tm), np.float32), 1), dtype=BF16)
    row = lambda w: pl.BlockSpec((None, tm, w), lambda bb, j: (bb, j, 0))
    const = lambda shape: pl.BlockSpec(shape, lambda bb, j: tuple(0 for _ in shape))
    return pl.pallas_call(
        functools.partial(_outproj_kernel, alpha=alpha),
        grid=(b, nt),
        in_specs=_mixer_row_specs(tm, d) + [
            pl.BlockSpec((None, 6, d), lambda bb, j: (mod_row_fn(bb), 0, 0)),
            const((d, d)), const((1, d)), const((1, d)), const((ROUTER_ROWS, d)), const((tm, tm))],
        out_specs=[row(d), pl.BlockSpec((None, tm * SUBLANES, LANES), lambda bb, j: (bb, j, 0)), row(LANES),
                   pl.BlockSpec((8, tm), lambda bb, j: (0, bb * nt + j)), const((8, LANES))],
        out_shape=[jax.ShapeDtypeStruct((b, ls, d), F32),
                   jax.ShapeDtypeStruct((b, ls * SUBLANES, LANES), F32),
                   jax.ShapeDtypeStruct((b, ls, LANES), F32),
                   jax.ShapeDtypeStruct((8, b * ls), F32),
                   jax.ShapeDtypeStruct((8, LANES), F32)],
        scratch_shapes=[pltpu.VMEM((ROUTER_ROWS, LANES), F32)],
        compiler_params=_cparams(("arbitrary", "arbitrary")),
        name="outproj_ln_route",
    )(attn, pool, ret, x, modtab, w_out, ln_g, ln_b, router, tri)


N_FFN_CHUNKS = FFN_DIM // FFN_CHUNK
N_FFN_GAPS = 3 * N_FFN_CHUNKS


def _swiglu_rows(tb, wg_ref, wu_ref, wd_ref, between=None):
    gap = (lambda j: None) if between is None else between
    acts = []
    for c in range(N_FFN_CHUNKS):
        sl = slice(c * FFN_CHUNK, (c + 1) * FFN_CHUNK)
        g = _dot(tb, wg_ref[:, sl])
        gap(3 * c)
        u = _dot(tb, wu_ref[:, sl])
        gap(3 * c + 1)
        acts.append((jax.nn.silu(g) * u).astype(BF16))
        gap(3 * c + 2)
    return _dot(jnp.concatenate(acts, axis=1), wd_ref[...])


def _dense_post_kernel(attn_ref, pool_ref, ret_ref, x_ref, mod_ref, w_ref, g1_ref, b1_ref,
                       wg_ref, wu_ref, wd_ref, g2_ref, b2_ref, o_ref, *, alpha):
    m = mod_ref[...]
    x1, t = _mixer_out_ln1(attn_ref, pool_ref, ret_ref, x_ref, m, w_ref, g1_ref, b1_ref, alpha)
    f = _swiglu_rows(t.astype(BF16), wg_ref, wu_ref, wd_ref)
    o_ref[...] = _layer_norm(alpha * x1 + m[5:6, :] * f, g2_ref[...], b2_ref[...])


def _dense_post(attn, pool, ret, x, modtab, mod_row_fn, w_out, g1, b1, wg, wu, wd, g2, b2, alpha):
    b, ls, d = x.shape
    tm = min(512, ls)
    nt = ls // tm
    const = lambda shape: pl.BlockSpec(shape, lambda bb, j: tuple(0 for _ in shape),
                                       pipeline_mode=pl.Buffered(1))
    return pl.pallas_call(
        functools.partial(_dense_post_kernel, alpha=alpha),
        grid=(b, nt),
        in_specs=_mixer_row_specs(tm, d) + [
            pl.BlockSpec((None, 6, d), lambda bb, j: (mod_row_fn(bb), 0, 0)),
            const((d, d)), const((1, d)), const((1, d)),
            const((d, FFN_DIM)), const((d, FFN_DIM)), const((FFN_DIM, d)),
            const((1, d)), const((1, d))],
        out_specs=pl.BlockSpec((None, tm, d), lambda bb, j: (bb, j, 0)),
        out_shape=jax.ShapeDtypeStruct((b, ls, d), F32),
        compiler_params=_cparams(("arbitrary", "arbitrary")),
        name="dense_post",
    )(attn, pool, ret, x, modtab, w_out, g1, b1, wg, wu, wd, g2, b2)


def _invert_kernel(p1_ref, p2_ref, lo_ref, hi_ref, dst_ref, *, n):
    for e in range(N_EXPERTS + 1):
        def fill(s, carry):
            dst_ref[s] = 2 * n + (s & (MOE_TILE - 1))
            return carry

        lax.fori_loop(lo_ref[e], hi_ref[e], fill, 0)

    def body(t, carry):
        dst_ref[p1_ref[t]] = t
        dst_ref[p2_ref[t]] = n + t
        return carry

    lax.fori_loop(0, n, body, 0, unroll=16)


def _invert(pos1, pos2, fill_lo, fill_hi, nslots):
    n = pos1.shape[0]
    smem = pl.BlockSpec(memory_space=pltpu.SMEM)
    return pl.pallas_call(
        functools.partial(_invert_kernel, n=n),
        in_specs=[smem, smem, smem, smem],
        out_specs=smem,
        out_shape=jax.ShapeDtypeStruct((nslots,), jnp.int32),
        name="invert",
    )(pos1, pos2, fill_lo, fill_hi)


def _slot_source_row(v, n):
    if n & (n - 1) == 0:
        return v & (n - 1)
    return jnp.where(v >= 2 * n, v - 2 * n, jnp.where(v >= n, v - n, v))


def _moe_kernel(te_ref, nt_ref, inv_ref, t_ref, wg_ref, wu_ref, wd_ref, y_ref, xbuf, obuf, gsem, ssem, *, n):
    del te_ref
    i = pl.program_id(0)
    nt = nt_ref[0]
    tm = MOE_TILE
    slot = i & 1
    other = 1 - slot

    def row_tile(r):
        return pl.ds(pl.multiple_of(r * SUBLANES, SUBLANES), SUBLANES)

    def gather_row(tile, r, buf, priority=0):
        src = _slot_source_row(inv_ref[tile * tm + r], n)
        pltpu.make_async_copy(t_ref.at[row_tile(src)], xbuf.at[buf, row_tile(r)],
                              gsem.at[buf]).start(priority=priority)

    def scatter_row(tile, r, buf, real, priority=0):
        dst = jnp.where(real, inv_ref[tile * tm + r], 2 * n + r)
        pltpu.make_async_copy(obuf.at[buf, row_tile(r)], y_ref.at[row_tile(dst)],
                              ssem.at[buf]).start(priority=priority)

    def wait_gather(buf):
        pltpu.make_async_copy(t_ref.at[pl.ds(0, tm * SUBLANES)], xbuf.at[buf], gsem.at[buf]).wait()

    def wait_scatter(buf):
        pltpu.make_async_copy(obuf.at[buf], y_ref.at[pl.ds(0, tm * SUBLANES)], ssem.at[buf]).wait()

    @pl.when(i == 0)
    def _():
        obuf[1] = jnp.zeros(obuf.shape[1:], obuf.dtype)

        def first(r, carry):
            gather_row(0, r, 0)
            return carry

        lax.fori_loop(0, tm, first, 0, unroll=8)

    def main(cur, oth):
        wait_gather(cur)

        @pl.when(i >= 1)
        def _():
            wait_scatter(cur)

        nxt = jnp.minimum(i + 1, nt - 1)
        prv = jnp.maximum(i - 1, 0)
        real = i >= 1
        n_issue_gaps = (4 * N_FFN_GAPS) // 5
        n_gather_gaps = n_issue_gaps // 2
        per_g = -(-tm // n_gather_gaps)
        per_s = -(-tm // (n_issue_gaps - n_gather_gaps))

        def between(j):
            if j < n_gather_gaps:
                for r in range(j * per_g, min((j + 1) * per_g, tm)):
                    gather_row(nxt, r, oth, priority=r % 2)
            elif j < n_issue_gaps:
                k = j - n_gather_gaps
                for r in range(k * per_s, min((k + 1) * per_s, tm)):
                    scatter_row(prv, r, oth, real, priority=r % 2)

        x = _load_tile_rows(xbuf, (cur,), tm).astype(BF16)
        _store_tile_rows(obuf, (cur,), _swiglu_rows(x, wg_ref, wu_ref, wd_ref, between))

    @pl.when((i < nt) & (slot == 0))
    def _():
        main(0, 1)

    @pl.when((i < nt) & (slot == 1))
    def _():
        main(1, 0)

    @pl.when(i == nt)
    def _():
        last = nt - 1
        buf = last & 1
        wait_scatter(nt & 1)

        def tail(r, carry):
            scatter_row(last, r, buf, True)
            return carry

        lax.fori_loop(0, tm, tail, 0, unroll=8)
        wait_scatter(buf)
        wait_gather(nt & 1)


def _moe_ffn(tile_expert, ntiles, inv, t, wg, wu, wd):
    n = t.shape[0] // SUBLANES
    d = D_MODEL
    tm = MOE_TILE
    assert n >= tm
    nt_max = inv.shape[0] // tm
    wspec = lambda shape: pl.BlockSpec((None,) + shape, lambda i, te, nt, iv: (te[i], 0, 0))
    return pl.pallas_call(
        functools.partial(_moe_kernel, n=n),
        grid_spec=pltpu.PrefetchScalarGridSpec(
            num_scalar_prefetch=3,
            grid=(nt_max,),
            in_specs=[pl.BlockSpec(memory_space=pl.ANY),
                      wspec((d, FFN_DIM)), wspec((d, FFN_DIM)), wspec((FFN_DIM, d))],
            out_specs=pl.BlockSpec(memory_space=pl.ANY),
            scratch_shapes=[pltpu.VMEM((2, tm * SUBLANES, LANES), F32), pltpu.VMEM((2, tm * SUBLANES, LANES), F32),
                            pltpu.SemaphoreType.DMA((2,)), pltpu.SemaphoreType.DMA((2,))],
        ),
        out_shape=jax.ShapeDtypeStruct(((2 * n + tm) * SUBLANES, LANES), F32),
        compiler_params=_cparams(("arbitrary",)),
        name="moe_ffn",
    )(tile_expert, ntiles, inv, t, wg, wu, wd)


def _combine_kernel(y1_ref, y2_ref, x1_ref, rt_ref, mod_ref, g_ref, b_ref, o_ref, *, alpha):
    rt = rt_ref[...]
    tm = x1_ref.shape[0]
    f = rt[:, 4:5] * _load_tile_rows(y1_ref, (), tm) + rt[:, 5:6] * _load_tile_rows(y2_ref, (), tm)
    m = mod_ref[...]
    o_ref[...] = _layer_norm(alpha * x1_ref[...] + m[5:6, :] * f, g_ref[...], b_ref[...])


def _combine_ln(y, x1, rtab, modtab, l, ln_g, ln_b, alpha):
    n, d = x1.shape
    tm = min(ROW_TILE, n)
    per_b = l // tm
    nblk = n // tm
    const = lambda shape: pl.BlockSpec(shape, lambda i: tuple(0 for _ in shape))
    return pl.pallas_call(
        functools.partial(_combine_kernel, alpha=alpha),
        grid=(nblk,),
        in_specs=[pl.BlockSpec((tm * SUBLANES, LANES), lambda i: (i, 0)),
                  pl.BlockSpec((tm * SUBLANES, LANES), lambda i: (nblk + i, 0)),
                  pl.BlockSpec((tm, d), lambda i: (i, 0)),
                  pl.BlockSpec((tm, LANES), lambda i: (i, 0)),
                  pl.BlockSpec((None, 6, d), lambda i: (i // per_b, 0, 0)),
                  const((1, d)), const((1, d))],
        out_specs=pl.BlockSpec((tm, d), lambda i: (i, 0)),
        out_shape=jax.ShapeDtypeStruct((n, d), F32),
        compiler_params=_cparams(("arbitrary",)),
        name="combine_ln",
    )(y, y, x1, rtab, modtab, ln_g, ln_b)


def _moe_layer(t, rtab, rtab_t, cnt, x1, modtab, l, wg, wu, wd, ln_g, ln_b, alpha):
    n = x1.shape[0]
    counts = cnt[:N_EXPERTS, 0].astype(jnp.int32)
    tiles_e = (counts + MOE_TILE - 1) // MOE_TILE
    tile_end = jnp.cumsum(tiles_e)
    tile_start = tile_end - tiles_e
    ntiles = tile_end[-1:]
    nt_max = (2 * n) // MOE_TILE + N_EXPERTS + 1
    tile_expert = jnp.minimum(
        jnp.sum((jnp.arange(nt_max)[:, None] >= tile_end[None, :]).astype(jnp.int32), axis=1),
        N_EXPERTS - 1).astype(jnp.int32)
    e1 = rtab_t[0].astype(jnp.int32)
    e2 = rtab_t[1].astype(jnp.int32)
    row_start = tile_start * MOE_TILE
    pos1 = row_start[e1] + rtab_t[2].astype(jnp.int32)
    pos2 = row_start[e2] + rtab_t[3].astype(jnp.int32)
    fill_lo = jnp.concatenate([row_start + counts, tile_end[-1:] * MOE_TILE]).astype(jnp.int32)
    fill_hi = jnp.concatenate([tile_end * MOE_TILE, jnp.full((1,), nt_max * MOE_TILE)]).astype(jnp.int32)
    inv = _invert(pos1, pos2, fill_lo, fill_hi, nt_max * MOE_TILE)
    y = _moe_ffn(tile_expert, ntiles.astype(jnp.int32), inv, t, wg, wu, wd)
    return _combine_ln(y, x1, rtab, modtab, l, ln_g, ln_b, alpha)


def _prep_w_in(w):
    q = w[:, 0:ATTN_WIDTH] * (HEAD_DIM ** -0.5 * LOG2E)
    return jnp.concatenate([q, w[:, ATTN_WIDTH:]], axis=1).astype(BF16)


def _block_diag(mats):
    n = len(mats)
    rows = []
    for i, m in enumerate(mats):
        rows.append(jnp.concatenate([m if j == i else jnp.zeros_like(m) for j in range(n)], axis=1))
    return jnp.concatenate(rows, axis=0)


def kernel(x, c, ctx, c_ctx, w_mod, b_mod, w_in, attn_sink, pool_w, pool_scale, ret_log_decay_fwd,
           ret_log_decay_bwd, w_out, ln1_g, ln1_b, ln2_g, ln2_b, ffn_w_gate, ffn_w_up, ffn_w_down,
           moe_router, moe_w_gate, moe_w_up, moe_w_down):
    b, l, d = x.shape
    lc = ctx.shape[1]
    depth = w_in.shape[0]
    alpha = (2.0 * depth) ** 0.25
    assert d == D_MODEL and l % 256 == 0 and lc % 256 == 0 and l % GRID_W == 0

    n_rows = ((b + 1 + 7) // 8) * 8
    c_all = jnp.concatenate([c, c_ctx[None, :], jnp.zeros((n_rows - b - 1, d), F32)], axis=0)
    mod_all = _modulation(c_all, w_mod, b_mod).reshape(depth, n_rows, 6, d)
    lat_row = lambda bb: bb
    ctx_row = lambda bb: b

    rope_tabs = _rope_tables(l)
    band = _pool_band_matrices()
    gmat = jnp.asarray(np.kron(np.eye(RET_HEADS), np.ones((RET_DV, RET_DV))) / RET_DV, dtype=BF16)
    zero_state = jnp.zeros((b, RET_KW, RET_WIDTH), F32)

    xc = ctx
    for i in range(depth):
        last = i == depth - 1
        modtab = mod_all[i]
        w2 = _prep_w_in(w_in[i])
        wo = w_out[i].astype(BF16)
        sink = attn_sink[i].astype(F32)
        pw = _block_diag([pool_w[i, g] for g in range(len(POOL_WINDOWS))]).astype(BF16)
        ps = pool_scale[i].reshape(1, POOL_WIDTH).astype(F32)
        lg = jnp.stack([ret_log_decay_fwd[i], ret_log_decay_bwd[i]]).astype(F32)
        g1, b1 = ln1_g[i].reshape(1, d), ln1_b[i].reshape(1, d)
        g2, b2 = ln2_g[i].reshape(1, d), ln2_b[i].reshape(1, d)

        qc, kdc, vdc, uc, rqc, rkc, rvc, rgc = _inproj(xc, modtab, ctx_row, w2, None)
        ret_c, s_f, s_b = _retention(lg, rqc, rkc, rvc, rgc, zero_state, zero_state, gmat)

        q, kd, vd, u, rq, rk, rv, rg = _inproj(x, modtab, lat_row, w2, rope_tabs)
        attn = _attention_latent(sink, q, kd, vd, kdc, vdc)
        pool = _pool(u, band, pw, ps)
        ret, _, _ = _retention(lg, rq, rk, rv, rg, s_f, s_b, gmat)

        j = i // 2
        if i % 2 == 0:
            wg, wu, wd = ffn_w_gate[j].astype(BF16), ffn_w_up[j].astype(BF16), ffn_w_down[j].astype(BF16)
            x = _dense_post(attn, pool, ret, x, modtab, lat_row, wo, g1, b1, wg, wu, wd, g2, b2, alpha)
        else:
            wg, wu, wd = moe_w_gate[j].astype(BF16), moe_w_up[j].astype(BF16), moe_w_down[j].astype(BF16)
            router = jnp.pad(moe_router[j].T, ((0, ROUTER_ROWS - N_EXPERTS), (0, 0))).astype(BF16)
            x1, t, rtab, rtab_t, cnt = _outproj_ln_route(attn, pool, ret, x, modtab, lat_row, wo, g1, b1, alpha,
                                                         router)
            x = _moe_layer(t.reshape(b * l * SUBLANES, LANES), rtab.reshape(b * l, LANES), rtab_t, cnt,
                           x1.reshape(b * l, d), modtab, l, wg, wu, wd, g2, b2, alpha).reshape(b, l, d)

        if not last:
            attn_c = _attention_ctx(sink, qc, kdc, vdc)
            pool_c = _pool(uc, band, pw, ps)
            if i % 2 == 0:
                xc = _dense_post(attn_c, pool_c, ret_c, xc, modtab, ctx_row, wo, g1, b1, wg, wu, wd, g2, b2,
                                 alpha)
            else:
                x1c, tc, rtab_c, rtab_tc, cnt_c = _outproj_ln_route(attn_c, pool_c, ret_c, xc, modtab, ctx_row, wo,
                                                                    g1, b1, alpha, router)
                ctx_mod = jnp.broadcast_to(modtab[b:b + 1], (b, 6, d))
                xc = _moe_layer(tc.reshape(b * lc * SUBLANES, LANES), rtab_c.reshape(b * lc, LANES), rtab_tc,
                                cnt_c, x1c.reshape(b * lc, d), ctx_mod, lc, wg, wu, wd, g2, b2,
                                alpha).reshape(b, lc, d)
    return x
```
